```python
import jax, jax.numpy as jnp
from jax import lax
import numpy as np

D_MODEL = 1024
BATCH = 16
SEQ = 256
DEPTH = 4
DEC_BATCH = 2
DEC_SEQ = 1024
PAST_LEN = 512

GRID_W = 64
ATT_WIDTH = D_MODEL // 2
CONV_WIDTH = D_MODEL - ATT_WIDTH
HEAD_DIM = 64
N_ATT_HEADS = ATT_WIDTH // HEAD_DIM
CONV_K = 3
WIN_ROWS_MAX = 8
WIN_COLS = 16
COL_BLOCK = 16
KEY_SPAN = COL_BLOCK + WIN_COLS
N_COL_BLOCKS = GRID_W // COL_BLOCK
CTX_Q_BLOCK = 128
N_GROUPS = 4
EXPERTS_PER_GROUP = 4
N_EXPERTS = N_GROUPS * EXPERTS_PER_GROUP
TOP_K = 2
D_EXPERT = D_MODEL // 4
ALPHA = (2 * DEPTH) ** 0.25
BETA = (8 * DEPTH) ** -0.25
LN_EPS = 1e-5
NEG_INF = -1e30

kernel_name = "hybrid_natten_shortconv_hmoe_diffusion_step"


def _layer_norm(x, g, b):
    xf = x.astype(jnp.float32)
    mu = jnp.mean(xf, axis=-1, keepdims=True)
    var = jnp.mean(jnp.square(xf - mu), axis=-1, keepdims=True)
    return ((xf - mu) * lax.rsqrt(var + LN_EPS)).astype(x.dtype) * g + b


def _modulation(cond, w_ada, b_ada):
    m = jax.nn.silu(cond) @ w_ada + b_ada
    return jnp.split(m, 6, axis=-1)


def _heads(t):
    B, L, _ = t.shape
    return t.reshape(B, L, N_ATT_HEADS, HEAD_DIM).transpose(0, 2, 1, 3)


def _project(h, w_in):
    z = h @ w_in
    splits = [ATT_WIDTH, 2 * ATT_WIDTH, 3 * ATT_WIDTH,
              3 * ATT_WIDTH + CONV_WIDTH, 3 * ATT_WIDTH + 2 * CONV_WIDTH]
    q, k, v, bg, cg, u = jnp.split(z, splits, axis=-1)
    return _heads(q), _heads(k), _heads(v), bg, cg, u


def _short_conv(bg, cg, u, w):
    y = cg * u
    yp = jnp.pad(y, ((0, 0), (1, 1), (0, 0)))
    conv = w[0] * yp[:, :-2] + w[1] * yp[:, 1:-1] + w[2] * yp[:, 2:]
    return bg * conv


def _merge(att, conv, w_out):
    B, H, L, Dh = att.shape
    a = att.transpose(0, 2, 1, 3).reshape(B, L, H * Dh)
    return jnp.concatenate([a, conv], axis=-1) @ w_out


def _context_attention(q, k, v):
    B, H, L, Dh = q.shape
    nb = L // CTX_Q_BLOCK
    qb = q.reshape(B, H, nb, CTX_Q_BLOCK, Dh).transpose(2, 0, 1, 3, 4)
    scale = HEAD_DIM ** -0.5

    def block(qi):
        s = jnp.einsum("bhqd,bhkd->bhqk", qi, k, preferred_element_type=jnp.float32) * scale
        p = jax.nn.softmax(s, axis=-1).astype(v.dtype)
        return jnp.einsum("bhqk,bhkd->bhqd", p, v)

    o = lax.map(block, qb)
    return o.transpose(1, 2, 0, 3, 4).reshape(B, H, L, Dh)


def _window_geometry():
    qcol = np.arange(GRID_W).reshape(N_COL_BLOCKS, COL_BLOCK)
    col_start = np.clip(qcol - WIN_COLS // 2, 0, GRID_W - WIN_COLS)
    key_start = np.clip(np.arange(N_COL_BLOCKS) * COL_BLOCK - WIN_COLS // 2, 0, GRID_W - KEY_SPAN)
    key_col = key_start[:, None] + np.arange(KEY_SPAN)[None, :]
    kc = key_col[:, None, :]
    valid = (kc >= col_start[:, :, None]) & (kc < col_start[:, :, None] + WIN_COLS)
    dc_idx = np.clip(kc - qcol[:, :, None] + WIN_COLS - 1, 0, 2 * WIN_COLS - 2)
    return key_col, valid, dc_idx


def _latent_attention(q, k, v, ck, cv, rpb):
    B, H, T, Dh = q.shape
    rows = T // GRID_W
    wr = min(WIN_ROWS_MAX, rows)
    key_col, valid, dc_idx = _window_geometry()
    scale = HEAD_DIM ** -0.5
    qg = q.reshape(B, H, rows, N_COL_BLOCKS, COL_BLOCK, Dh)
    kblk = jnp.take(k.reshape(B, H, rows, GRID_W, Dh), key_col, axis=3)
    vblk = jnp.take(v.reshape(B, H, rows, GRID_W, Dh), key_col, axis=3)
    mask = valid[:, :, None, :]
    n_loc = wr * KEY_SPAN

    def row(r):
        rs = jnp.clip(r - WIN_ROWS_MAX // 2, 0, rows - wr)
        q_r = lax.dynamic_index_in_dim(qg, r, axis=2, keepdims=False)
        kb = lax.dynamic_slice_in_dim(kblk, rs, wr, axis=2)
        vb = lax.dynamic_slice_in_dim(vblk, rs, wr, axis=2)
        dr_idx = rs + jnp.arange(wr) - r + WIN_ROWS_MAX - 1
        bias = rpb[:, dr_idx[None, None, :, None], dc_idx[:, :, None, :]]
        s_loc = jnp.einsum("bhncd,bhwnkd->bhncwk", q_r, kb,
                           preferred_element_type=jnp.float32) * scale + bias[None]
        s_loc = jnp.where(mask, s_loc, NEG_INF).reshape(B, H, N_COL_BLOCKS, COL_BLOCK, n_loc)
        s_ctx = jnp.einsum("bhncd,bhld->bhncl", q_r, ck,
                           preferred_element_type=jnp.float32) * scale
        p = jax.nn.softmax(jnp.concatenate([s_loc, s_ctx], axis=-1), axis=-1).astype(v.dtype)
        p_loc = p[..., :n_loc].reshape(B, H, N_COL_BLOCKS, COL_BLOCK, wr, KEY_SPAN)
        return (jnp.einsum("bhncwk,bhwnkd->bhncd", p_loc, vb)
                + jnp.einsum("bhncl,bhld->bhncd", p[..., n_loc:], cv))

    o = lax.map(row, jnp.arange(rows))
    return o.transpose(1, 2, 0, 3, 4, 5).reshape(B, H, T, Dh)


def _hier_moe(h, wg, bg, we, be, w_exp_in, w_exp_out):
    shape = h.shape
    t = h.reshape(-1, shape[-1])
    g_prob = jax.nn.softmax((t @ wg + bg).astype(jnp.float32), axis=-1)
    g_p, g_idx = lax.top_k(g_prob, 1)
    e_logits = (t @ we + be).astype(jnp.float32).reshape(-1, N_GROUPS, EXPERTS_PER_GROUP)
    e_prob = jax.nn.softmax(jnp.take_along_axis(e_logits, g_idx[:, :, None], axis=1)[:, 0], axis=-1)
    e_p, e_idx = lax.top_k(e_prob, TOP_K)
    w = g_p * e_p / jnp.sum(e_p, axis=-1, keepdims=True)
    ids = g_idx * EXPERTS_PER_GROUP + e_idx
    gate = jnp.sum(jax.nn.one_hot(ids, N_EXPERTS, dtype=jnp.float32) * w[..., None], axis=1).astype(h.dtype)
    hid = jnp.einsum("nd,edf->nef", t, w_exp_in)
    a, up = jnp.split(hid, 2, axis=-1)
    act = jax.nn.silu(a) * up * gate[:, :, None]
    return jnp.einsum("nef,efd->nd", act, w_exp_out).reshape(shape)


def setup_inputs(seed: int = 0) -> dict:
    key = jax.random.key(seed)
    ks = jax.random.split(key, 22)
    D = D_MODEL
    mix = ATT_WIDTH + CONV_WIDTH

    def nrm(k, shape, s):
        return s * jax.random.normal(k, shape, jnp.float32)

    return {
        "x_prompt": nrm(ks[0], (BATCH, SEQ, D), 1.0),
        "x_sample": nrm(ks[1], (DEC_BATCH, DEC_SEQ, D), 1.0),
        "cache_k": nrm(ks[2], (DEC_BATCH, DEPTH, N_ATT_HEADS, PAST_LEN, HEAD_DIM), 1.0),
        "cache_v": nrm(ks[3], (DEC_BATCH, DEPTH, N_ATT_HEADS, PAST_LEN, HEAD_DIM), 1.0),
        "c": nrm(ks[4], (DEC_BATCH, D), 1.0),
        "c_ctx": nrm(ks[5], (D,), 1.0),
        "w_ada": nrm(ks[6], (DEPTH, D, 6 * D), 0.5 * D ** -0.5),
        "b_ada": nrm(ks[7], (DEPTH, 6 * D), 0.02),
        "w_in": nrm(ks[8], (DEPTH, D, 3 * ATT_WIDTH + 3 * CONV_WIDTH), D ** -0.5),
        "conv_w": nrm(ks[9], (DEPTH, CONV_K, CONV_WIDTH), CONV_K ** -0.5),
        "rpb": nrm(ks[10], (DEPTH, N_ATT_HEADS, 2 * WIN_ROWS_MAX - 1, 2 * WIN_COLS - 1), 0.1),
        "w_out": nrm(ks[11], (DEPTH, mix, D), BETA * mix ** -0.5),
        "ln1_g": 1.0 + nrm(ks[12], (DEPTH, D), 0.02),
        "ln1_b": nrm(ks[13], (DEPTH, D), 0.02),
        "w_router_group": nrm(ks[14], (DEPTH, D, N_GROUPS), D ** -0.5),
        "b_router_group": nrm(ks[15], (DEPTH, N_GROUPS), 0.01),
        "w_router_expert": nrm(ks[16], (DEPTH, D, N_EXPERTS), D ** -0.5),
        "b_router_expert": nrm(ks[17], (DEPTH, N_EXPERTS), 0.01),
        "w_expert_in": nrm(ks[18], (DEPTH, N_EXPERTS, D, 2 * D_EXPERT), D ** -0.5),
        "w_expert_out": nrm(ks[19], (DEPTH, N_EXPERTS, D_EXPERT, D), BETA * D_EXPERT ** -0.5),
        "ln2_g": 1.0 + nrm(ks[20], (DEPTH, D), 0.02),
        "ln2_b": nrm(ks[21], (DEPTH, D), 0.02),
    }


def reference(x_prompt, x_sample, cache_k, cache_v, c, c_ctx, w_ada, b_ada, w_in, conv_w, rpb,
              w_out, ln1_g, ln1_b, w_router_group, b_router_group, w_router_expert,
              b_router_expert, w_expert_in, w_expert_out, ln2_g, ln2_b):
    xp = x_prompt
    xs = x_sample
    new_k = []
    new_v = []
    for l in range(DEPTH):
        moe_params = (w_router_group[l], b_router_group[l], w_router_expert[l],
                      b_router_expert[l], w_expert_in[l], w_expert_out[l])
        sa, ca, ga, sf, cf, gf = _modulation(c_ctx[None, None, :], w_ada[l], b_ada[l])
        q, k, v, bg, cg, u = _project(xp * (1 + ca) + sa, w_in[l])
        mix = _merge(_context_attention(q, k, v), _short_conv(bg, cg, u, conv_w[l]), w_out[l])
        xp = _layer_norm(ALPHA * xp + ga * mix, ln1_g[l], ln1_b[l])
        xp = _layer_norm(ALPHA * xp + gf * _hier_moe(xp * (1 + cf) + sf, *moe_params),
                         ln2_g[l], ln2_b[l])
        new_k.append(k)
        new_v.append(v)
        sa, ca, ga, sf, cf, gf = _modulation(c[:, None, :], w_ada[l], b_ada[l])
        q, k, v, bg, cg, u = _project(xs * (1 + ca) + sa, w_in[l])
        att = _latent_attention(q, k, v, cache_k[:, l], cache_v[:, l], rpb[l])
        mix = _merge(att, _short_conv(bg, cg, u, conv_w[l]), w_out[l])
        xs = _layer_norm(ALPHA * xs + ga * mix, ln1_g[l], ln1_b[l])
        xs = _layer_norm(ALPHA * xs + gf * _hier_moe(xs * (1 + cf) + sf, *moe_params),
                         ln2_g[l], ln2_b[l])
    new_cache_k = jnp.stack(new_k, axis=1)
    new_cache_v = jnp.stack(new_v, axis=1)
    return (xp, xs, new_cache_k, new_cache_v)
```

```python
import functools

import jax
import jax.numpy as jnp
import numpy as np
from jax import lax
from jax.experimental import pallas as pl
from jax.experimental.pallas import tpu as pltpu

D_MODEL = 1024
DEPTH = 4
GRID_W = 64
ATT_WIDTH = D_MODEL // 2
CONV_WIDTH = D_MODEL - ATT_WIDTH
HEAD_DIM = 64
N_HEADS = ATT_WIDTH // HEAD_DIM
WIN_ROWS = 8
WIN_COLS = 16
N_GROUPS = 4
EXPERTS_PER_GROUP = 4
N_EXPERTS = N_GROUPS * EXPERTS_PER_GROUP
D_EXPERT = D_MODEL // 4
ALPHA = (2 * DEPTH) ** 0.25
LN_EPS = 1e-5
NEG_INF = -1e30
QK_SCALE = HEAD_DIM ** -0.5

F32 = jnp.float32
BF16 = jnp.bfloat16

LANES = 128
VMEM_LIMIT_BYTES = 56 * 1024 * 1024

CTX_SEQ_PER_STEP = 2
LAT_Q_CHUNK = 256
MOE_TOKENS = 1024
MOD_COLS = 1024


def _dot(a, b):
    return jnp.dot(a, b, preferred_element_type=F32)


def _dot_nt(a, b):
    return lax.dot_general(a, b, (((1,), (1,)), ((), ())), preferred_element_type=F32)


def _silu(x):
    return x * (1.0 / (1.0 + jnp.exp(-x)))


def _layer_norm(r, g, b):
    mu = jnp.mean(r, axis=-1, keepdims=True)
    d = r - mu
    var = jnp.mean(d * d, axis=-1, keepdims=True)
    return d * lax.rsqrt(var + LN_EPS) * g + b


def _mod_kernel(cond_ref, w_ref, b_ref, o_ref):
    s = _silu(cond_ref[...]).astype(BF16)
    o_ref[...] = _dot(s, w_ref[...].astype(BF16)) + b_ref[...]


def _modulation(cond, w_ada, b_ada):
    n_out = w_ada.shape[-1]
    return pl.pallas_call(
        _mod_kernel,
        grid=(DEPTH, n_out // MOD_COLS),
        in_specs=[
            pl.BlockSpec((8, D_MODEL), lambda l, j: (0, 0)),
            pl.BlockSpec((None, D_MODEL, MOD_COLS), lambda l, j: (l, 0, j)),
            pl.BlockSpec((None, 1, MOD_COLS), lambda l, j: (l, 0, j)),
        ],
        out_specs=pl.BlockSpec((None, 8, MOD_COLS), lambda l, j: (l, 0, j)),
        out_shape=jax.ShapeDtypeStruct((DEPTH, 8, n_out), F32),
        compiler_params=pltpu.CompilerParams(
            dimension_semantics=("arbitrary", "arbitrary"),
            vmem_limit_bytes=VMEM_LIMIT_BYTES),
        name="adaln_modulation",
    )(cond, w_ada, b_ada.reshape(DEPTH, 1, n_out))


def _project(x, mod, win_ref, convw_ref, q_scr, k_scr, v_scr, mrg_scr, seq_len, k_out=None, v_out=None):
    m = x.shape[0]
    sa = mod[:, 0:D_MODEL]
    ca = mod[:, D_MODEL:2 * D_MODEL]
    h = (x * (1.0 + ca) + sa).astype(BF16)

    zq = _dot(h, win_ref[:, 0:ATT_WIDTH]) * QK_SCALE
    for hd in range(N_HEADS):
        q_scr[hd] = zq[:, hd * HEAD_DIM:(hd + 1) * HEAD_DIM].astype(BF16)
    for z_idx, (scr, out) in enumerate(((k_scr, k_out), (v_scr, v_out))):
        z = _dot(h, win_ref[:, (1 + z_idx) * ATT_WIDTH:(2 + z_idx) * ATT_WIDTH])
        for hd in range(N_HEADS):
            zh = z[:, hd * HEAD_DIM:(hd + 1) * HEAD_DIM]
            scr[hd] = zh.astype(BF16)
            if out is not None:
                for s in range(m // seq_len):
                    out[s, hd] = zh[s * seq_len:(s + 1) * seq_len, :]

    c0 = 3 * ATT_WIDTH
    bg = _dot(h, win_ref[:, c0:c0 + CONV_WIDTH])
    cg = _dot(h, win_ref[:, c0 + CONV_WIDTH:c0 + 2 * CONV_WIDTH])
    u = _dot(h, win_ref[:, c0 + 2 * CONV_WIDTH:c0 + 3 * CONV_WIDTH])
    y = cg * u
    t = lax.broadcasted_iota(jnp.int32, (m, 1), 0) % seq_len
    y_prev = jnp.where(t == 0, 0.0, pltpu.roll(y, 1, 0))
    y_next = jnp.where(t == seq_len - 1, 0.0, pltpu.roll(y, m - 1, 0))
    cw = convw_ref[...]
    conv = cw[0:1, :] * y_prev + cw[1:2, :] * y + cw[2:3, :] * y_next
    mrg_scr[:, ATT_WIDTH:] = bg * conv


def _merge_and_norm(x, mod, att_scr, mrg_scr, wout_ref, g_ref, b_ref, o_ref):
    for hd in range(N_HEADS):
        mrg_scr[:, hd * HEAD_DIM:(hd + 1) * HEAD_DIM] = att_scr[hd]
    mix = _dot(mrg_scr[...].astype(BF16), wout_ref[...])
    ga = mod[:, 2 * D_MODEL:3 * D_MODEL]
    o_ref[...] = _layer_norm(ALPHA * x + ga * mix, g_ref[...], b_ref[...]).reshape(o_ref.shape)


def _softmax_rows(parts):
    mx = functools.reduce(jnp.maximum, [jnp.max(p, axis=-1, keepdims=True) for p in parts])
    es = [jnp.exp(p - mx) for p in parts]
    inv = 1.0 / functools.reduce(jnp.add, [jnp.sum(e, axis=-1, keepdims=True) for e in es])
    return [(e * inv).astype(BF16) for e in es]


def _ctx_mixer_kernel(x_ref, mod_ref, win_ref, wout_ref, convw_ref, g_ref, b_ref,
                      o_ref, k_ref, v_ref, q_scr, k_scr, v_scr, att_scr, mrg_scr):
    sb, seq_len, _ = x_ref.shape
    x = x_ref[...].reshape(sb * seq_len, D_MODEL)
    mod = mod_ref[0:1, :]
    _project(x, mod, win_ref, convw_ref, q_scr, k_scr, v_scr, mrg_scr, seq_len, k_ref, v_ref)

    def head(hd, carry):
        for s in range(sb):
            rows = pl.ds(s * seq_len, seq_len)
            (p,) = _softmax_rows([_dot_nt(q_scr[hd, rows, :], k_scr[hd, rows, :])])
            att_scr[hd, rows, :] = _dot(p, v_scr[hd, rows, :])
        return carry

    lax.fori_loop(0, N_HEADS, head, 0)
    _merge_and_norm(x, mod, att_scr, mrg_scr, wout_ref, g_ref, b_ref, o_ref)


def _ctx_mixer(l, xp, mod, w_in, w_out, conv_w, ln_g, ln_b):
    batch, seq_len, _ = xp.shape
    sb = CTX_SEQ_PER_STEP
    m = sb * seq_len
    kv_shape = jax.ShapeDtypeStruct((batch, N_HEADS, seq_len, HEAD_DIM), F32)
    kv_spec = pl.BlockSpec((sb, N_HEADS, seq_len, HEAD_DIM), lambda i: (i, 0, 0, 0))
    return pl.pallas_call(
        _ctx_mixer_kernel,
        grid=(batch // sb,),
        in_specs=[
            pl.BlockSpec((sb, seq_len, D_MODEL), lambda i: (i, 0, 0)),
            pl.BlockSpec((None, 8, 6 * D_MODEL), lambda i: (l, 0, 0)),
            pl.BlockSpec((None, D_MODEL, 3 * D_MODEL), lambda i: (l, 0, 0), pipeline_mode=pl.Buffered(1)),
            pl.BlockSpec((None, D_MODEL, D_MODEL), lambda i: (l, 0, 0), pipeline_mode=pl.Buffered(1)),
            pl.BlockSpec((None, 3, CONV_WIDTH), lambda i: (l, 0, 0)),
            pl.BlockSpec((None, 1, D_MODEL), lambda i: (l, 0, 0)),
            pl.BlockSpec((None, 1, D_MODEL), lambda i: (l, 0, 0)),
        ],
        out_specs=[
            pl.BlockSpec((sb, seq_len, D_MODEL), lambda i: (i, 0, 0)),
            kv_spec, kv_spec,
        ],
        out_shape=[jax.ShapeDtypeStruct(xp.shape, F32), kv_shape, kv_shape],
        scratch_shapes=[
            pltpu.VMEM((N_HEADS, m, HEAD_DIM), BF16),
            pltpu.VMEM((N_HEADS, m, HEAD_DIM), BF16),
            pltpu.VMEM((N_HEADS, m, HEAD_DIM), BF16),
            pltpu.VMEM((N_HEADS, m, HEAD_DIM), F32),
            pltpu.VMEM((m, D_MODEL), F32),
        ],
        compiler_params=pltpu.CompilerParams(
            dimension_semantics=("arbitrary",), vmem_limit_bytes=VMEM_LIMIT_BYTES),
        name="ctx_mixer",
    )(xp, mod, w_in, w_out, conv_w, ln_g, ln_b)


def _window_start(r, rows):
    return min(max(r - WIN_ROWS // 2, 0), rows - WIN_ROWS)


def _lat_mixer_kernel(x_ref, mod_ref, win_ref, wout_ref, convw_ref, g_ref, b_ref,
                      ck_ref, cv_ref, tbl_ref, o_ref,
                      q_scr, k_scr, v_scr, att_scr, mrg_scr, bias_scr):
    b = pl.program_id(0)
    hd = pl.program_id(1)
    seq_len = x_ref.shape[0]
    rows = seq_len // GRID_W
    mod = mod_ref[pl.ds(1 + b, 1), :]

    @pl.when(hd == 0)
    def _():
        _project(x_ref[...], mod, win_ref, convw_ref, q_scr, k_scr, v_scr, mrg_scr, seq_len)

    for r in range(rows):
        rs = _window_start(r, rows)
        qrows = slice(r * GRID_W, (r + 1) * GRID_W)
        bias_scr[qrows, :] = jnp.full((GRID_W, seq_len), NEG_INF, F32)
        dr0 = rs - r + WIN_ROWS - 1
        bias_scr[qrows, rs * GRID_W:(rs + WIN_ROWS) * GRID_W] = (
            tbl_ref[:, dr0 * GRID_W:(dr0 + WIN_ROWS) * GRID_W])

    ck = ck_ref[...].astype(BF16)
    cv = cv_ref[...].astype(BF16)

    def chunk(qc, carry):
        qrows = pl.ds(pl.multiple_of(qc * LAT_Q_CHUNK, LAT_Q_CHUNK), LAT_Q_CHUNK)
        q = q_scr[hd, qrows, :]
        s_loc = _dot_nt(q, k_scr[hd]) + bias_scr[qrows, :]
        s_ctx = _dot_nt(q, ck)
        p_loc, p_ctx = _softmax_rows([s_loc, s_ctx])
        att_scr[hd, qrows, :] = _dot(p_loc, v_scr[hd]) + _dot(p_ctx, cv)
        return carry

    lax.fori_loop(0, seq_len // LAT_Q_CHUNK, chunk, 0)

    @pl.when(hd == N_HEADS - 1)
    def _():
        _merge_and_norm(x_ref[...], mod, att_scr, mrg_scr, wout_ref, g_ref, b_ref, o_ref)


def _lat_mixer(l, xs, mod, w_in, w_out, conv_w, ln_g, ln_b, cache_k, cache_v, tbl):
    batch, seq_len, _ = xs.shape
    past = cache_k.shape[3]
    n_tbl = tbl.shape[-1]
    cache_spec = pl.BlockSpec((None, None, None, past, HEAD_DIM), lambda b, h: (b, l, h, 0, 0))
    return pl.pallas_call(
        _lat_mixer_kernel,
        grid=(batch, N_HEADS),
        in_specs=[
            pl.BlockSpec((None, seq_len, D_MODEL), lambda b, h: (b, 0, 0)),
            pl.BlockSpec((None, 8, 6 * D_MODEL), lambda b, h: (l, 0, 0)),
            pl.BlockSpec((None, D_MODEL, 3 * D_MODEL), lambda b, h: (l, 0, 0), pipeline_mode=pl.Buffered(1)),
            pl.BlockSpec((None, D_MODEL, D_MODEL), lambda b, h: (l, 0, 0), pipeline_mode=pl.Buffered(1)),
            pl.BlockSpec((None, 3, CONV_WIDTH), lambda b, h: (l, 0, 0)),
            pl.BlockSpec((None, 1, D_MODEL), lambda b, h: (l, 0, 0)),
            pl.BlockSpec((None, 1, D_MODEL), lambda b, h: (l, 0, 0)),
            cache_spec, cache_spec,
            pl.BlockSpec((None, None, GRID_W, n_tbl), lambda b, h: (l, h, 0, 0)),
        ],
        out_specs=pl.BlockSpec((None, seq_len, D_MODEL), lambda b, h: (b, 0, 0)),
        out_shape=jax.ShapeDtypeStruct(xs.shape, F32),
        scratch_shapes=[
            pltpu.VMEM((N_HEADS, seq_len, HEAD_DIM), BF16),
            pltpu.VMEM((N_HEADS, seq_len, HEAD_DIM), BF16),
            pltpu.VMEM((N_HEADS, seq_len, HEAD_DIM), BF16),
            pltpu.VMEM((N_HEADS, seq_len, HEAD_DIM), F32),
            pltpu.VMEM((seq_len, D_MODEL), F32),
            pltpu.VMEM((seq_len, seq_len), F32),
        ],
        compiler_params=pltpu.CompilerParams(
            dimension_semantics=("arbitrary", "arbitrary"), vmem_limit_bytes=VMEM_LIMIT_BYTES),
        name="lat_mixer",
    )(xs, mod, w_in, w_out, conv_w, ln_g, ln_b, cache_k, cache_v, tbl)


def _bias_table(rpb):
    c = np.arange(GRID_W)[:, None]
    cp = np.arange(GRID_W)[None, :]
    col_start = np.clip(c - WIN_COLS // 2, 0, GRID_W - WIN_COLS)
    valid = (cp >= col_start) & (cp < col_start + WIN_COLS)
    dc = np.clip(cp - c + WIN_COLS - 1, 0, 2 * WIN_COLS - 2)
    t = rpb[:, :, :, dc]
    t = jnp.where(valid[None, None, None], t, NEG_INF)
    t = t.transpose(0, 1, 3, 2, 4)
    return t.reshape(rpb.shape[0], rpb.shape[1], GRID_W, -1)


def _route(logits):
    lane = lax.broadcasted_iota(jnp.int32, logits.shape, 1)
    lane_f = lane.astype(F32)
    big = jnp.float32(LANES)

    def first_lane(cond):
        return jnp.min(jnp.where(cond, lane_f, big), axis=-1, keepdims=True)

    gmask = (lane >= N_EXPERTS) & (lane < N_EXPERTS + N_GROUPS)
    gl = jnp.where(gmask, logits, NEG_INF)
    gexp = jnp.exp(gl - jnp.max(gl, axis=-1, keepdims=True))
    gprob = gexp / jnp.sum(gexp, axis=-1, keepdims=True)
    g_p = jnp.max(gprob, axis=-1, keepdims=True)
    g_idx = first_lane(gmask & (gprob == g_p)) - N_EXPERTS

    lane_group = jnp.floor(lane_f * (1.0 / EXPERTS_PER_GROUP))
    emask = (lane < N_EXPERTS) & (lane_group == g_idx)
    el = jnp.where(emask, logits, NEG_INF)
    eexp = jnp.exp(el - jnp.max(el, axis=-1, keepdims=True))
    eprob = eexp / jnp.sum(eexp, axis=-1, keepdims=True)
    p1 = jnp.max(eprob, axis=-1, keepdims=True)
    i1 = first_lane(emask & (eprob == p1))
    rest = emask & (lane_f != i1)
    p2 = jnp.max(jnp.where(rest, eprob, -1.0), axis=-1, keepdims=True)
    i2 = first_lane(rest & (eprob == p2))
    denom = p1 + p2
    return (jnp.where(lane_f == i1, g_p * p1 / denom, 0.0)
            + jnp.where(lane_f == i2, g_p * p2 / denom, 0.0))


def _moe_kernel(x_ref, mod_ref, wr_ref, br_ref, win_ref, wout_ref, g_ref, b_ref, o_ref,
                h_scr, gate_scr, acc_scr):
    e = pl.program_id(1)

    @pl.when(e == 0)
    def _():
        sf = mod_ref[:, 3 * D_MODEL:4 * D_MODEL]
        cf = mod_ref[:, 4 * D_MODEL:5 * D_MODEL]
        h = x_ref[...] * (1.0 + cf) + sf
        h_scr[...] = h.astype(BF16)
        logits = jnp.dot(h, wr_ref[...], preferred_element_type=F32,
                         precision=lax.Precision.HIGHEST) + br_ref[...]
        gate_scr[...] = _route(logits)
        acc_scr[...] = jnp.zeros_like(acc_scr)

    hid = _dot(h_scr[...], win_ref[...].astype(BF16))
    a = hid[:, :D_EXPERT]
    up = hid[:, D_EXPERT:]
    lane = lax.broadcasted_iota(jnp.int32, gate_scr.shape, 1)
    gate_e = jnp.sum(jnp.where(lane == e, gate_scr[...], 0.0), axis=-1, keepdims=True)
    act = _silu(a) * up * gate_e
    acc_scr[...] += _dot(act.astype(BF16), wout_ref[...].astype(BF16))

    @pl.when(e == N_EXPERTS - 1)
    def _():
        gf = mod_ref[:, 5 * D_MODEL:6 * D_MODEL]
        o_ref[...] = _layer_norm(ALPHA * x_ref[...] + gf * acc_scr[...], g_ref[...], b_ref[...])


def _moe(l, x, mod_rows, w_router, b_router, w_exp_in, w_exp_out, ln_g, ln_b):
    n = x.shape[0]
    tm = MOE_TOKENS
    return pl.pallas_call(
        _moe_kernel,
        grid=(n // tm, N_EXPERTS),
        in_specs=[
            pl.BlockSpec((tm, D_MODEL), lambda i, e: (i, 0)),
            pl.BlockSpec((None, None, 1, 6 * D_MODEL), lambda i, e: (l, i, 0, 0)),
            pl.BlockSpec((None, D_MODEL, LANES), lambda i, e: (l, 0, 0)),
            pl.BlockSpec((None, 1, LANES), lambda i, e: (l, 0, 0)),
            pl.BlockSpec((None, None, D_MODEL, 2 * D_EXPERT), lambda i, e: (l, e, 0, 0)),
            pl.BlockSpec((None, None, D_EXPERT, D_MODEL), lambda i, e: (l, e, 0, 0)),
            pl.BlockSpec((None, 1, D_MODEL), lambda i, e: (l, 0, 0)),
            pl.BlockSpec((None, 1, D_MODEL), lambda i, e: (l, 0, 0)),
        ],
        out_specs=pl.BlockSpec((tm, D_MODEL), lambda i, e: (i, 0)),
        out_shape=jax.ShapeDtypeStruct(x.shape, F32),
        scratch_shapes=[
            pltpu.VMEM((tm, D_MODEL), BF16),
            pltpu.VMEM((tm, LANES), F32),
            pltpu.VMEM((tm, D_MODEL), F32),
        ],
        compiler_params=pltpu.CompilerParams(
            dimension_semantics=("arbitrary", "arbitrary"), vmem_limit_bytes=VMEM_LIMIT_BYTES),
        name="hier_moe",
    )(x, mod_rows, w_router, b_router, w_exp_in, w_exp_out, ln_g, ln_b)


def kernel(x_prompt, x_sample, cache_k, cache_v, c, c_ctx, w_ada, b_ada, w_in, conv_w, rpb, w_out,
           ln1_g, ln1_b, w_router_group, b_router_group, w_router_expert, b_router_expert,
           w_expert_in, w_expert_out, ln2_g, ln2_b):
    batch, seq_len, _ = x_prompt.shape
    dec_batch, dec_seq, _ = x_sample.shape
    assert dec_batch + 1 <= 8 and dec_seq == MOE_TOKENS and (batch * seq_len) % MOE_TOKENS == 0

    cond = jnp.concatenate([c_ctx[None, :], c, jnp.zeros((8 - 1 - dec_batch, D_MODEL), F32)], axis=0)
    mod = _modulation(cond, w_ada, b_ada)
    n_ctx_tiles = batch * seq_len // MOE_TOKENS
    mod_rows_ctx = jnp.broadcast_to(mod[:, 0:1, None, :], (DEPTH, n_ctx_tiles, 1, 6 * D_MODEL))
    mod_rows_lat = mod[:, 1:1 + dec_batch, None, :]

    w_in_bf = w_in.astype(BF16)
    w_out_bf = w_out.astype(BF16)
    pad = jnp.zeros((DEPTH, D_MODEL, LANES - N_EXPERTS - N_GROUPS), F32)
    w_router = jnp.concatenate([w_router_expert, w_router_group, pad], axis=-1)
    b_router = jnp.concatenate([b_router_expert, b_router_group, pad[:, 0, :]], axis=-1)[:, None, :]
    tbl = _bias_table(rpb)
    ln1_g3, ln1_b3 = ln1_g[:, None, :], ln1_b[:, None, :]
    ln2_g3, ln2_b3 = ln2_g[:, None, :], ln2_b[:, None, :]

    xp, xs = x_prompt, x_sample
    new_k, new_v = [], []
    for l in range(DEPTH):
        moe = functools.partial(_moe, l, w_router=w_router, b_router=b_router, w_exp_in=w_expert_in,
                                w_exp_out=w_expert_out, ln_g=ln2_g3, ln_b=ln2_b3)
        xp, k, v = _ctx_mixer(l, xp, mod, w_in_bf, w_out_bf, conv_w, ln1_g3, ln1_b3)
        xp = moe(xp.reshape(-1, D_MODEL), mod_rows_ctx).reshape(xp.shape)
        new_k.append(k)
        new_v.append(v)
        xs = _lat_mixer(l, xs, mod, w_in_bf, w_out_bf, conv_w, ln1_g3, ln1_b3, cache_k, cache_v, tbl)
        xs = moe(xs.reshape(-1, D_MODEL), mod_rows_lat).reshape(xs.shape)
    return (xp, xs, jnp.stack(new_k, axis=1), jnp.stack(new_v, axis=1))
```

```python
import functools

import jax
import jax.numpy as jnp
from jax import lax
from jax.experimental import pallas as pl
from jax.experimental.pallas import tpu as pltpu

D_MODEL = 1024
DEPTH = 4
GRID_W = 64
ATT_WIDTH = D_MODEL // 2
CONV_WIDTH = D_MODEL - ATT_WIDTH
HEAD_DIM = 64
N_HEADS = ATT_WIDTH // HEAD_DIM
WIN_ROWS = 8
WIN_COLS = 16
N_GROUPS = 4
EXPERTS_PER_GROUP = 4
N_EXPERTS = N_GROUPS * EXPERTS_PER_GROUP
D_EXPERT = D_MODEL // 4
ALPHA = (2 * DEPTH) ** 0.25
LN_EPS = 1e-5
NEG_INF = -1e30
QK_SCALE = HEAD_DIM ** -0.5

F32 = jnp.float32
BF16 = jnp.bfloat16

LANES = 128
VMEM_LIMIT_BYTES = 56 * 1024 * 1024

CTX_SEQ_PER_STEP = 2
LAT_Q_CHUNK = 256
MOE_TOKENS = 1024
MOD_COLS = 1024


def _dot(a, b):
    return jnp.dot(a, b, preferred_element_type=F32)


def _dot_nt(a, b):
    return lax.dot_general(a, b, (((1,), (1,)), ((), ())), preferred_element_type=F32)


def _silu(x):
    return x * (1.0 / (1.0 + jnp.exp(-x)))


def _layer_norm(r, g, b):
    mu = jnp.mean(r, axis=-1, keepdims=True)
    d = r - mu
    var = jnp.mean(d * d, axis=-1, keepdims=True)
    return d * lax.rsqrt(var + LN_EPS) * g + b


def _mod_kernel(cond_ref, w_ref, b_ref, o_ref):
    s = _silu(cond_ref[...]).astype(BF16)
    o_ref[...] = _dot(s, w_ref[...].astype(BF16)) + b_ref[...]


def _modulation(cond, w_ada, b_ada):
    n_out = w_ada.shape[-1]
    return pl.pallas_call(
        _mod_kernel,
        grid=(DEPTH, n_out // MOD_COLS),
        in_specs=[
            pl.BlockSpec((8, D_MODEL), lambda l, j: (0, 0)),
            pl.BlockSpec((None, D_MODEL, MOD_COLS), lambda l, j: (l, 0, j)),
            pl.BlockSpec((None, 1, MOD_COLS), lambda l, j: (l, 0, j)),
        ],
        out_specs=pl.BlockSpec((None, 8, MOD_COLS), lambda l, j: (l, 0, j)),
        out_shape=jax.ShapeDtypeStruct((DEPTH, 8, n_out), F32),
        compiler_params=pltpu.CompilerParams(
            dimension_semantics=("arbitrary", "arbitrary"),
            vmem_limit_bytes=VMEM_LIMIT_BYTES),
        name="adaln_modulation",
    )(cond, w_ada, b_ada.reshape(DEPTH, 1, n_out))


def _project(x, mod, wqc_ref, wkvT_ref, convw_ref, q_scr, kvT_scr, mrg_scr, seq_len, kT_out=None, vT_out=None):
    m = x.shape[0]
    sa = mod[:, 0:D_MODEL]
    ca = mod[:, D_MODEL:2 * D_MODEL]
    h = (x * (1.0 + ca) + sa).astype(BF16)

    zq = _dot(h, wqc_ref[:, 0:ATT_WIDTH]) * QK_SCALE
    for hd in range(N_HEADS):
        q_scr[hd] = zq[:, hd * HEAD_DIM:(hd + 1) * HEAD_DIM].astype(BF16)

    zkvT = _dot_nt(wkvT_ref[...], h)
    kvT_scr[...] = zkvT.astype(BF16)
    for out, c0 in ((kT_out, 0), (vT_out, ATT_WIDTH)):
        if out is not None:
            for s in range(m // seq_len):
                for hd in range(N_HEADS):
                    out[s, hd] = zkvT[c0 + hd * HEAD_DIM:c0 + (hd + 1) * HEAD_DIM, s * seq_len:(s + 1) * seq_len]

    c0 = ATT_WIDTH
    bg = _dot(h, wqc_ref[:, c0:c0 + CONV_WIDTH])
    cg = _dot(h, wqc_ref[:, c0 + CONV_WIDTH:c0 + 2 * CONV_WIDTH])
    u = _dot(h, wqc_ref[:, c0 + 2 * CONV_WIDTH:c0 + 3 * CONV_WIDTH])
    y = cg * u
    t = lax.broadcasted_iota(jnp.int32, (m, 1), 0) % seq_len
    y_prev = jnp.where(t == 0, 0.0, pltpu.roll(y, 1, 0))
    y_next = jnp.where(t == seq_len - 1, 0.0, pltpu.roll(y, m - 1, 0))
    cw = convw_ref[...]
    conv = cw[0:1, :] * y_prev + cw[1:2, :] * y + cw[2:3, :] * y_next
    mrg_scr[:, ATT_WIDTH:] = bg * conv


def _merge_and_norm(x, mod, att_scr, mrg_scr, wout_ref, g_ref, b_ref, o_ref):
    for hd in range(N_HEADS):
        mrg_scr[:, hd * HEAD_DIM:(hd + 1) * HEAD_DIM] = att_scr[hd]
    mix = _dot(mrg_scr[...].astype(BF16), wout_ref[...])
    ga = mod[:, 2 * D_MODEL:3 * D_MODEL]
    o_ref[...] = _layer_norm(ALPHA * x + ga * mix, g_ref[...], b_ref[...]).reshape(o_ref.shape)


def _softmax_rows(parts):
    mx = functools.reduce(jnp.maximum, [jnp.max(p, axis=-1, keepdims=True) for p in parts])
    es = [jnp.exp(p - mx) for p in parts]
    inv = 1.0 / functools.reduce(jnp.add, [jnp.sum(e, axis=-1, keepdims=True) for e in es])
    return [(e * inv).astype(BF16) for e in es]


def _head_rows(hd):
    return pl.ds(pl.multiple_of(hd * HEAD_DIM, HEAD_DIM), HEAD_DIM)


def _weight_spec(shape, index_map):
    return pl.BlockSpec(shape, index_map, pipeline_mode=pl.Buffered(1))


def _ctx_mixer_kernel(n_alias, x_ref, mod_ref, wqc_ref, wkvT_ref, wout_ref, convw_ref, g_ref, b_ref, *rest):
    o_ref, kT_ref, vT_ref, q_scr, kvT_scr, att_scr, mrg_scr = rest[n_alias:]
    sb, seq_len, _ = x_ref.shape
    x = x_ref[...].reshape(sb * seq_len, D_MODEL)
    mod = mod_ref[0:1, :]
    _project(x, mod, wqc_ref, wkvT_ref, convw_ref, q_scr, kvT_scr, mrg_scr, seq_len, kT_ref, vT_ref)

    def head(hd, carry):
        for s in range(sb):
            tok = pl.ds(s * seq_len, seq_len)
            kT = kvT_scr[_head_rows(hd), tok]
            vT = kvT_scr[_head_rows(N_HEADS + hd), tok]
            (p,) = _softmax_rows([_dot(q_scr[hd, tok, :], kT)])
            att_scr[hd, tok, :] = _dot_nt(p, vT)
        return carry

    lax.fori_loop(0, N_HEADS, head, 0)
    _merge_and_norm(x, mod, att_scr, mrg_scr, wout_ref, g_ref, b_ref, o_ref)


def _ctx_mixer(l, xp, mod, w_qc, w_kvT, w_out, conv_w, ln_g, ln_b, kv_bufs):
    batch, seq_len, _ = xp.shape
    sb = CTX_SEQ_PER_STEP
    m = sb * seq_len
    kv_shape = jax.ShapeDtypeStruct((batch, DEPTH, N_HEADS, HEAD_DIM, seq_len), F32)
    kv_spec = pl.BlockSpec((sb, None, N_HEADS, HEAD_DIM, seq_len), lambda i: (i, l, 0, 0, 0))
    in_specs = [
        pl.BlockSpec((sb, seq_len, D_MODEL), lambda i: (i, 0, 0)),
        pl.BlockSpec((None, 8, 6 * D_MODEL), lambda i: (l, 0, 0)),
        _weight_spec((None, D_MODEL, ATT_WIDTH + 3 * CONV_WIDTH), lambda i: (l, 0, 0)),
        _weight_spec((None, 2 * ATT_WIDTH, D_MODEL), lambda i: (l, 0, 0)),
        _weight_spec((None, D_MODEL, D_MODEL), lambda i: (l, 0, 0)),
        pl.BlockSpec((None, 3, CONV_WIDTH), lambda i: (l, 0, 0)),
        pl.BlockSpec((None, 1, D_MODEL), lambda i: (l, 0, 0)),
        pl.BlockSpec((None, 1, D_MODEL), lambda i: (l, 0, 0)),
    ]
    args = [xp, mod, w_qc, w_kvT, w_out, conv_w, ln_g, ln_b]
    aliases = {}
    if kv_bufs is not None:
        aliases = {len(args): 1, len(args) + 1: 2}
        in_specs += [pl.BlockSpec(memory_space=pl.ANY)] * 2
        args += list(kv_bufs)
    return pl.pallas_call(
        functools.partial(_ctx_mixer_kernel, len(aliases)),
        grid=(batch // sb,),
        in_specs=in_specs,
        out_specs=[pl.BlockSpec((sb, seq_len, D_MODEL), lambda i: (i, 0, 0)), kv_spec, kv_spec],
        out_shape=[jax.ShapeDtypeStruct(xp.shape, F32), kv_shape, kv_shape],
        input_output_aliases=aliases,
        scratch_shapes=[
            pltpu.VMEM((N_HEADS, m, HEAD_DIM), BF16),
            pltpu.VMEM((2 * ATT_WIDTH, m), BF16),
            pltpu.VMEM((N_HEADS, m, HEAD_DIM), F32),
            pltpu.VMEM((m, D_MODEL), F32),
        ],
        compiler_params=pltpu.CompilerParams(
            dimension_semantics=("arbitrary",), vmem_limit_bytes=VMEM_LIMIT_BYTES),
        name="ctx_mixer",
    )(*args)


def _window_start(r, rows):
    return min(max(r - WIN_ROWS // 2, 0), rows - WIN_ROWS)


def _lat_mixer_kernel(x_ref, mod_ref, wqc_ref, wkvT_ref, wout_ref, convw_ref, g_ref, b_ref,
                      ckT_ref, cvT_ref, tbl_ref, o_ref,
                      q_scr, kvT_scr, att_scr, mrg_scr, bias_scr):
    b = pl.program_id(0)
    hd = pl.program_id(1)
    seq_len = x_ref.shape[0]
    rows = seq_len // GRID_W
    mod = mod_ref[pl.ds(1 + b, 1), :]

    @pl.when(hd == 0)
    def _():
        _project(x_ref[...], mod, wqc_ref, wkvT_ref, convw_ref, q_scr, kvT_scr, mrg_scr, seq_len)

    for r in range(rows):
        rs = _window_start(r, rows)
        qrows = slice(r * GRID_W, (r + 1) * GRID_W)
        bias_scr[qrows, :] = jnp.full((GRID_W, seq_len), NEG_INF, F32)
        dr0 = rs - r + WIN_ROWS - 1
        bias_scr[qrows, rs * GRID_W:(rs + WIN_ROWS) * GRID_W] = (
            tbl_ref[:, dr0 * GRID_W:(dr0 + WIN_ROWS) * GRID_W])

    kT = kvT_scr[_head_rows(hd), :]
    vT = kvT_scr[_head_rows(N_HEADS + hd), :]
    ckT = ckT_ref[...].astype(BF16)
    cvT = cvT_ref[...].astype(BF16)

    def chunk(qc, carry):
        qrows = pl.ds(pl.multiple_of(qc * LAT_Q_CHUNK, LAT_Q_CHUNK), LAT_Q_CHUNK)
        q = q_scr[hd, qrows, :]
        s_loc = _dot(q, kT) + bias_scr[qrows, :]
        s_ctx = _dot(q, ckT)
        p_loc, p_ctx = _softmax_rows([s_loc, s_ctx])
        att_scr[hd, qrows, :] = _dot_nt(p_loc, vT) + _dot_nt(p_ctx, cvT)
        return carry

    lax.fori_loop(0, seq_len // LAT_Q_CHUNK, chunk, 0)

    @pl.when(hd == N_HEADS - 1)
    def _():
        _merge_and_norm(x_ref[...], mod, att_scr, mrg_scr, wout_ref, g_ref, b_ref, o_ref)


def _lat_mixer(l, xs, mod, w_qc, w_kvT, w_out, conv_w, ln_g, ln_b, cache_kT, cache_vT, tbl):
    batch, seq_len, _ = xs.shape
    past = cache_kT.shape[-1]
    n_tbl = tbl.shape[-1]
    cache_spec = pl.BlockSpec((None, None, None, HEAD_DIM, past), lambda b, h: (b, l, h, 0, 0))
    return pl.pallas_call(
        _lat_mixer_kernel,
        grid=(batch, N_HEADS),
        in_specs=[
            pl.BlockSpec((None, seq_len, D_MODEL), lambda b, h: (b, 0, 0)),
            pl.BlockSpec((None, 8, 6 * D_MODEL), lambda b, h: (l, 0, 0)),
            _weight_spec((None, D_MODEL, ATT_WIDTH + 3 * CONV_WIDTH), lambda b, h: (l, 0, 0)),
            _weight_spec((None, 2 * ATT_WIDTH, D_MODEL), lambda b, h: (l, 0, 0)),
            _weight_spec((None, D_MODEL, D_MODEL), lambda b, h: (l, 0, 0)),
            pl.BlockSpec((None, 3, CONV_WIDTH), lambda b, h: (l, 0, 0)),
            pl.BlockSpec((None, 1, D_MODEL), lambda b, h: (l, 0, 0)),
            pl.BlockSpec((None, 1, D_MODEL), lambda b, h: (l, 0, 0)),
            cache_spec, cache_spec,
            pl.BlockSpec((None, None, GRID_W, n_tbl), lambda b, h: (l, h, 0, 0)),
        ],
        out_specs=pl.BlockSpec((None, seq_len, D_MODEL), lambda b, h: (b, 0, 0)),
        out_shape=jax.ShapeDtypeStruct(xs.shape, F32),
        scratch_shapes=[
            pltpu.VMEM((N_HEADS, seq_len, HEAD_DIM), BF16),
            pltpu.VMEM((2 * ATT_WIDTH, seq_len), BF16),
            pltpu.VMEM((N_HEADS, seq_len, HEAD_DIM), F32),
            pltpu.VMEM((seq_len, D_MODEL), F32),
            pltpu.VMEM((seq_len, seq_len), F32),
        ],
        compiler_params=pltpu.CompilerParams(
            dimension_semantics=("arbitrary", "arbitrary"), vmem_limit_bytes=VMEM_LIMIT_BYTES),
        name="lat_mixer",
    )(xs, mod, w_qc, w_kvT, w_out, conv_w, ln_g, ln_b, cache_kT, cache_vT, tbl)


def _bias_table(rpb):
    depth, heads, n_dr, n_dc = rpb.shape
    lane0 = GRID_W - WIN_COLS
    rpb_pad = jnp.pad(rpb, ((0, 0), (0, 0), (0, 16 - n_dr), (lane0, LANES - lane0 - n_dc)))
    rpb_pad = rpb_pad.reshape(depth * heads, 16, LANES)

    def body(r_ref, o_ref):
        c = lax.broadcasted_iota(jnp.int32, (GRID_W, GRID_W), 0)
        cp = lax.broadcasted_iota(jnp.int32, (GRID_W, GRID_W), 1)
        col_start = jnp.clip(c - WIN_COLS // 2, 0, GRID_W - WIN_COLS)
        valid = (cp >= col_start) & (cp < col_start + WIN_COLS)
        for dr in range(n_dr):
            row = jnp.broadcast_to(r_ref[dr:dr + 1, :], (GRID_W, LANES))
            skew = pltpu.roll(row, LANES - GRID_W + 1, 1, stride=1, stride_axis=0)
            o_ref[:, dr * GRID_W:(dr + 1) * GRID_W] = jnp.where(valid, skew[:, :GRID_W], NEG_INF)

    tbl = pl.pallas_call(
        body,
        grid=(depth * heads,),
        in_specs=[pl.BlockSpec((None, 16, LANES), lambda i: (i, 0, 0))],
        out_specs=pl.BlockSpec((None, GRID_W, n_dr * GRID_W), lambda i: (i, 0, 0)),
        out_shape=jax.ShapeDtypeStruct((depth * heads, GRID_W, n_dr * GRID_W), F32),
        compiler_params=pltpu.CompilerParams(dimension_semantics=("arbitrary",)),
        name="bias_table",
    )(rpb_pad)
    return tbl.reshape(depth, heads, GRID_W, n_dr * GRID_W)


def _route(logits):
    lane = lax.broadcasted_iota(jnp.int32, logits.shape, 1)
    lane_f = lane.astype(F32)
    big = jnp.float32(LANES)

    def first_lane(cond):
        return jnp.min(jnp.where(cond, lane_f, big), axis=-1, keepdims=True)

    gmask = (lane >= N_EXPERTS) & (lane < N_EXPERTS + N_GROUPS)
    gl = jnp.where(gmask, logits, NEG_INF)
    gexp = jnp.exp(gl - jnp.max(gl, axis=-1, keepdims=True))
    gprob = gexp / jnp.sum(gexp, axis=-1, keepdims=True)
    g_p = jnp.max(gprob, axis=-1, keepdims=True)
    g_idx = first_lane(gmask & (gprob == g_p)) - N_EXPERTS

    lane_group = jnp.floor(lane_f * (1.0 / EXPERTS_PER_GROUP))
    emask = (lane < N_EXPERTS) & (lane_group == g_idx)
    el = jnp.where(emask, logits, NEG_INF)
    eexp = jnp.exp(el - jnp.max(el, axis=-1, keepdims=True))
    eprob = eexp / jnp.sum(eexp, axis=-1, keepdims=True)
    p1 = jnp.max(eprob, axis=-1, keepdims=True)
    i1 = first_lane(emask & (eprob == p1))
    rest = emask & (lane_f != i1)
    p2 = jnp.max(jnp.where(rest, eprob, -1.0), axis=-1, keepdims=True)
    i2 = first_lane(rest & (eprob == p2))
    denom = p1 + p2
    return (jnp.where(lane_f == i1, g_p * p1 / denom, 0.0)
            + jnp.where(lane_f == i2, g_p * p2 / denom, 0.0))


def _moe_kernel(x_ref, mod_ref, wr_ref, br_ref, win_ref, wout_ref, g_ref, b_ref, o_ref,
                h_scr, gate_scr, acc_scr):
    e = pl.program_id(1)

    @pl.when(e == 0)
    def _():
        sf = mod_ref[:, 3 * D_MODEL:4 * D_MODEL]
        cf = mod_ref[:, 4 * D_MODEL:5 * D_MODEL]
        h = x_ref[...] * (1.0 + cf) + sf
        h_scr[...] = h.astype(BF16)
        logits = jnp.dot(h, wr_ref[...], preferred_element_type=F32,
                         precision=lax.Precision.HIGHEST) + br_ref[...]
        gate_scr[...] = _route(logits)
        acc_scr[...] = jnp.zeros_like(acc_scr)

    hid = _dot(h_scr[...], win_ref[...].astype(BF16))
    a = hid[:, :D_EXPERT]
    up = hid[:, D_EXPERT:]
    lane = lax.broadcasted_iota(jnp.int32, gate_scr.shape, 1)
    gate_e = jnp.sum(jnp.where(lane == e, gate_scr[...], 0.0), axis=-1, keepdims=True)
    act = _silu(a) * up * gate_e
    acc_scr[...] += _dot(act.astype(BF16), wout_ref[...].astype(BF16))

    @pl.when(e == N_EXPERTS - 1)
    def _():
        gf = mod_ref[:, 5 * D_MODEL:6 * D_MODEL]
        o_ref[...] = _layer_norm(ALPHA * x_ref[...] + gf * acc_scr[...], g_ref[...], b_ref[...])


def _moe(l, x, mod_rows, w_router, b_router, w_exp_in, w_exp_out, ln_g, ln_b):
    n = x.shape[0]
    tm = MOE_TOKENS
    return pl.pallas_call(
        _moe_kernel,
        grid=(n // tm, N_EXPERTS),
        in_specs=[
            pl.BlockSpec((tm, D_MODEL), lambda i, e: (i, 0)),
            pl.BlockSpec((None, None, 1, 6 * D_MODEL), lambda i, e: (l, i, 0, 0)),
            pl.BlockSpec((None, D_MODEL, LANES), lambda i, e: (l, 0, 0)),
            pl.BlockSpec((None, 1, LANES), lambda i, e: (l, 0, 0)),
            pl.BlockSpec((None, None, D_MODEL, 2 * D_EXPERT), lambda i, e: (l, e, 0, 0)),
            pl.BlockSpec((None, None, D_EXPERT, D_MODEL), lambda i, e: (l, e, 0, 0)),
            pl.BlockSpec((None, 1, D_MODEL), lambda i, e: (l, 0, 0)),
            pl.BlockSpec((None, 1, D_MODEL), lambda i, e: (l, 0, 0)),
        ],
        out_specs=pl.BlockSpec((tm, D_MODEL), lambda i, e: (i, 0)),
        out_shape=jax.ShapeDtypeStruct(x.shape, F32),
        scratch_shapes=[
            pltpu.VMEM((tm, D_MODEL), BF16),
            pltpu.VMEM((tm, LANES), F32),
            pltpu.VMEM((tm, D_MODEL), F32),
        ],
        compiler_params=pltpu.CompilerParams(
            dimension_semantics=("arbitrary", "arbitrary"), vmem_limit_bytes=VMEM_LIMIT_BYTES),
        name="hier_moe",
    )(x, mod_rows, w_router, b_router, w_exp_in, w_exp_out, ln_g, ln_b)


def kernel(x_prompt, x_sample, cache_k, cache_v, c, c_ctx, w_ada, b_ada, w_in, conv_w, rpb, w_out,
           ln1_g, ln1_b, w_router_group, b_router_group, w_router_expert, b_router_expert,
           w_expert_in, w_expert_out, ln2_g, ln2_b):
    batch, seq_len, _ = x_prompt.shape
    dec_batch, dec_seq, _ = x_sample.shape
    assert dec_batch + 1 <= 8 and dec_seq == MOE_TOKENS and (batch * seq_len) % MOE_TOKENS == 0

    cond = jnp.concatenate([c_ctx[None, :], c, jnp.zeros((8 - 1 - dec_batch, D_MODEL), F32)], axis=0)
    mod = _modulation(cond, w_ada, b_ada)
    n_ctx_tiles = batch * seq_len // MOE_TOKENS
    mod_rows_ctx = jnp.broadcast_to(mod[:, 0:1, None, :], (DEPTH, n_ctx_tiles, 1, 6 * D_MODEL))
    mod_rows_lat = mod[:, 1:1 + dec_batch, None, :]

    w_qc = jnp.concatenate([w_in[:, :, :ATT_WIDTH], w_in[:, :, 3 * ATT_WIDTH:]], axis=-1).astype(BF16)
    w_kvT = jnp.swapaxes(w_in[:, :, ATT_WIDTH:3 * ATT_WIDTH], 1, 2).astype(BF16)
    w_out_bf = w_out.astype(BF16)
    pad = jnp.zeros((DEPTH, D_MODEL, LANES - N_EXPERTS - N_GROUPS), F32)
    w_router = jnp.concatenate([w_router_expert, w_router_group, pad], axis=-1)
    b_router = jnp.concatenate([b_router_expert, b_router_group, pad[:, 0, :]], axis=-1)[:, None, :]
    tbl = _bias_table(rpb)
    cache_kT = jnp.swapaxes(cache_k, -1, -2)
    cache_vT = jnp.swapaxes(cache_v, -1, -2)
    ln1_g3, ln1_b3 = ln1_g[:, None, :], ln1_b[:, None, :]
    ln2_g3, ln2_b3 = ln2_g[:, None, :], ln2_b[:, None, :]

    xp, xs = x_prompt, x_sample
    kv_bufs = None
    for l in range(DEPTH):
        moe = functools.partial(_moe, l, w_router=w_router, b_router=b_router, w_exp_in=w_expert_in,
                                w_exp_out=w_expert_out, ln_g=ln2_g3, ln_b=ln2_b3)
        xp, *kv_bufs = _ctx_mixer(l, xp, mod, w_qc, w_kvT, w_out_bf, conv_w, ln1_g3, ln1_b3, kv_bufs)
        xp = moe(xp.reshape(-1, D_MODEL), mod_rows_ctx).reshape(xp.shape)
        xs = _lat_mixer(l, xs, mod, w_qc, w_kvT, w_out_bf, conv_w, ln1_g3, ln1_b3, cache_kT, cache_vT, tbl)
        xs = moe(xs.reshape(-1, D_MODEL), mod_rows_lat).reshape(xs.shape)
    new_kT, new_vT = kv_bufs
    return (xp, xs, jnp.swapaxes(new_kT, -1, -2), jnp.swapaxes(new_vT, -1, -2))
```

```python
import functools

import jax
import jax.numpy as jnp
from jax import lax
from jax.experimental import pallas as pl
from jax.experimental.pallas import tpu as pltpu

D_MODEL = 1024
DEPTH = 4
GRID_W = 64
ATT_WIDTH = D_MODEL // 2
CONV_WIDTH = D_MODEL - ATT_WIDTH
HEAD_DIM = 64
N_HEADS = ATT_WIDTH // HEAD_DIM
WIN_ROWS = 8
WIN_COLS = 16
N_GROUPS = 4
EXPERTS_PER_GROUP = 4
N_EXPERTS = N_GROUPS * EXPERTS_PER_GROUP
D_EXPERT = D_MODEL // 4
ALPHA = (2 * DEPTH) ** 0.25
LN_EPS = 1e-5
NEG_INF = -1e30
QK_SCALE = HEAD_DIM ** -0.5

F32 = jnp.float32
BF16 = jnp.bfloat16

LANES = 128
VMEM_LIMIT_BYTES = 56 * 1024 * 1024

CTX_SEQ_PER_STEP = 2
LAT_Q_CHUNK = 256
MOE_TOKENS = 1024
MOD_COLS = 1024


def _dot(a, b):
    return jnp.dot(a, b, preferred_element_type=F32)


def _dot_nt(a, b):
    return lax.dot_general(a, b, (((1,), (1,)), ((), ())), preferred_element_type=F32)


def _silu(x):
    return x * (1.0 / (1.0 + jnp.exp(-x)))


def _layer_norm(r, g, b):
    mu = jnp.mean(r, axis=-1, keepdims=True)
    d = r - mu
    var = jnp.mean(d * d, axis=-1, keepdims=True)
    return d * lax.rsqrt(var + LN_EPS) * g + b


def _mod_kernel(cond_ref, w_ref, b_ref, o_ref):
    s = _silu(cond_ref[...]).astype(BF16)
    o_ref[...] = _dot(s, w_ref[...].astype(BF16)) + b_ref[...]


def _modulation(cond, w_ada, b_ada):
    n_out = w_ada.shape[-1]
    return pl.pallas_call(
        _mod_kernel,
        grid=(DEPTH, n_out // MOD_COLS),
        in_specs=[
            pl.BlockSpec((8, D_MODEL), lambda l, j: (0, 0)),
            pl.BlockSpec((None, D_MODEL, MOD_COLS), lambda l, j: (l, 0, j)),
            pl.BlockSpec((None, 1, MOD_COLS), lambda l, j: (l, 0, j)),
        ],
        out_specs=pl.BlockSpec((None, 8, MOD_COLS), lambda l, j: (l, 0, j)),
        out_shape=jax.ShapeDtypeStruct((DEPTH, 8, n_out), F32),
        compiler_params=pltpu.CompilerParams(
            dimension_semantics=("arbitrary", "arbitrary"),
            vmem_limit_bytes=VMEM_LIMIT_BYTES),
        name="adaln_modulation",
    )(cond, w_ada, b_ada.reshape(DEPTH, 1, n_out))


def _project(x, mod, win_ref, convw_ref, q_scr, kvT_scr, conv_scr, seq_len, kT_out=None, vT_out=None):
    m = x.shape[0]
    sa = mod[:, 0:D_MODEL]
    ca = mod[:, D_MODEL:2 * D_MODEL]
    h = (x * (1.0 + ca) + sa).astype(BF16)

    zq = _dot(h, win_ref[:, 0:ATT_WIDTH]) * QK_SCALE
    for hd in range(N_HEADS):
        q_scr[hd] = zq[:, hd * HEAD_DIM:(hd + 1) * HEAD_DIM].astype(BF16)

    zkvT = _dot(h, win_ref[:, ATT_WIDTH:3 * ATT_WIDTH]).T
    kvT_scr[...] = zkvT.astype(BF16)
    for out, c0 in ((kT_out, 0), (vT_out, ATT_WIDTH)):
        if out is not None:
            for s in range(m // seq_len):
                for hd in range(N_HEADS):
                    out[s, hd] = zkvT[c0 + hd * HEAD_DIM:c0 + (hd + 1) * HEAD_DIM, s * seq_len:(s + 1) * seq_len]

    c0 = 3 * ATT_WIDTH
    bg = _dot(h, win_ref[:, c0:c0 + CONV_WIDTH])
    cg = _dot(h, win_ref[:, c0 + CONV_WIDTH:c0 + 2 * CONV_WIDTH])
    u = _dot(h, win_ref[:, c0 + 2 * CONV_WIDTH:c0 + 3 * CONV_WIDTH])
    y = cg * u
    t = lax.broadcasted_iota(jnp.int32, (m, 1), 0) % seq_len
    y_prev = jnp.where(t == 0, 0.0, pltpu.roll(y, 1, 0))
    y_next = jnp.where(t == seq_len - 1, 0.0, pltpu.roll(y, m - 1, 0))
    cw = convw_ref[...]
    conv = cw[0:1, :] * y_prev + cw[1:2, :] * y + cw[2:3, :] * y_next
    conv_scr[...] = (bg * conv).astype(BF16)


def _merge_and_norm(x, mod, attT_scr, conv_scr, wout_ref, g_ref, b_ref, o_ref):
    att = attT_scr[...].T.astype(BF16)
    mix = _dot(att, wout_ref[0:ATT_WIDTH, :]) + _dot(conv_scr[...], wout_ref[ATT_WIDTH:, :])
    ga = mod[:, 2 * D_MODEL:3 * D_MODEL]
    o_ref[...] = _layer_norm(ALPHA * x + ga * mix, g_ref[...], b_ref[...]).reshape(o_ref.shape)


def _softmax_rows(parts):
    mx = functools.reduce(jnp.maximum, [jnp.max(p, axis=-1, keepdims=True) for p in parts])
    es = [jnp.exp(p - mx) for p in parts]
    inv = 1.0 / functools.reduce(jnp.add, [jnp.sum(e, axis=-1, keepdims=True) for e in es])
    return [(e * inv).astype(BF16) for e in es]


def _head_rows(hd):
    return pl.ds(pl.multiple_of(hd * HEAD_DIM, HEAD_DIM), HEAD_DIM)


def _weight_spec(shape, index_map):
    return pl.BlockSpec(shape, index_map, pipeline_mode=pl.Buffered(1))


def _ctx_mixer_kernel(n_alias, x_ref, mod_ref, win_ref, wout_ref, convw_ref, g_ref, b_ref, *rest):
    o_ref, kT_ref, vT_ref, q_scr, kvT_scr, attT_scr, conv_scr = rest[n_alias:]
    sb, seq_len, _ = x_ref.shape
    x = x_ref[...].reshape(sb * seq_len, D_MODEL)
    mod = mod_ref[0:1, :]
    _project(x, mod, win_ref, convw_ref, q_scr, kvT_scr, conv_scr, seq_len, kT_ref, vT_ref)

    for hd in range(N_HEADS):
        krows = slice(hd * HEAD_DIM, (hd + 1) * HEAD_DIM)
        vrows = slice(ATT_WIDTH + hd * HEAD_DIM, ATT_WIDTH + (hd + 1) * HEAD_DIM)
        for s in range(sb):
            tok = slice(s * seq_len, (s + 1) * seq_len)
            (p,) = _softmax_rows([_dot(q_scr[hd, tok, :], kvT_scr[krows, tok])])
            attT_scr[krows, tok] = _dot_nt(kvT_scr[vrows, tok], p)

    _merge_and_norm(x, mod, attT_scr, conv_scr, wout_ref, g_ref, b_ref, o_ref)


def _ctx_mixer(l, xp, mod, w_in, w_out, conv_w, ln_g, ln_b, kv_bufs):
    batch, seq_len, _ = xp.shape
    sb = CTX_SEQ_PER_STEP
    m = sb * seq_len
    kv_shape = jax.ShapeDtypeStruct((batch, DEPTH, N_HEADS, HEAD_DIM, seq_len), F32)
    kv_spec = pl.BlockSpec((sb, None, N_HEADS, HEAD_DIM, seq_len), lambda i: (i, l, 0, 0, 0))
    in_specs = [
        pl.BlockSpec((sb, seq_len, D_MODEL), lambda i: (i, 0, 0)),
        pl.BlockSpec((None, 8, 6 * D_MODEL), lambda i: (l, 0, 0)),
        _weight_spec((None, D_MODEL, 3 * ATT_WIDTH + 3 * CONV_WIDTH), lambda i: (l, 0, 0)),
        _weight_spec((None, D_MODEL, D_MODEL), lambda i: (l, 0, 0)),
        pl.BlockSpec((None, 3, CONV_WIDTH), lambda i: (l, 0, 0)),
        pl.BlockSpec((None, 1, D_MODEL), lambda i: (l, 0, 0)),
        pl.BlockSpec((None, 1, D_MODEL), lambda i: (l, 0, 0)),
    ]
    args = [xp, mod, w_in, w_out, conv_w, ln_g, ln_b]
    aliases = {}
    if kv_bufs is not None:
        aliases = {len(args): 1, len(args) + 1: 2}
        in_specs += [pl.BlockSpec(memory_space=pl.ANY)] * 2
        args += list(kv_bufs)
    return pl.pallas_call(
        functools.partial(_ctx_mixer_kernel, len(aliases)),
        grid=(batch // sb,),
        in_specs=in_specs,
        out_specs=[pl.BlockSpec((sb, seq_len, D_MODEL), lambda i: (i, 0, 0)), kv_spec, kv_spec],
        out_shape=[jax.ShapeDtypeStruct(xp.shape, F32), kv_shape, kv_shape],
        input_output_aliases=aliases,
        scratch_shapes=[
            pltpu.VMEM((N_HEADS, m, HEAD_DIM), BF16),
            pltpu.VMEM((2 * ATT_WIDTH, m), BF16),
            pltpu.VMEM((ATT_WIDTH, m), F32),
            pltpu.VMEM((m, CONV_WIDTH), BF16),
        ],
        compiler_params=pltpu.CompilerParams(
            dimension_semantics=("arbitrary",), vmem_limit_bytes=VMEM_LIMIT_BYTES),
        name="ctx_mixer",
    )(*args)


def _window_start(r, rows):
    return min(max(r - WIN_ROWS // 2, 0), rows - WIN_ROWS)


def _lat_mixer_kernel(x_ref, mod_ref, win_ref, wout_ref, convw_ref, g_ref, b_ref,
                      ckT_ref, cvT_ref, tbl_ref, o_ref,
                      q_scr, kvT_scr, attT_scr, conv_scr, bias_scr):
    b = pl.program_id(0)
    hd = pl.program_id(1)
    seq_len = x_ref.shape[0]
    rows = seq_len // GRID_W
    mod = mod_ref[pl.ds(1 + b, 1), :]

    @pl.when(hd == 0)
    def _():
        _project(x_ref[...], mod, win_ref, convw_ref, q_scr, kvT_scr, conv_scr, seq_len)

    for r in range(rows):
        rs = _window_start(r, rows)
        qrows = slice(r * GRID_W, (r + 1) * GRID_W)
        bias_scr[qrows, :] = jnp.full((GRID_W, seq_len), NEG_INF, F32)
        dr0 = rs - r + WIN_ROWS - 1
        bias_scr[qrows, rs * GRID_W:(rs + WIN_ROWS) * GRID_W] = (
            tbl_ref[:, dr0 * GRID_W:(dr0 + WIN_ROWS) * GRID_W])

    k_rows = _head_rows(hd)
    v_rows = _head_rows(N_HEADS + hd)
    ckT = ckT_ref[...].astype(BF16)
    cvT = cvT_ref[...].astype(BF16)

    chunk_rows = LAT_Q_CHUNK // GRID_W
    for qc in range(seq_len // LAT_Q_CHUNK):
        r0 = qc * chunk_rows
        k_lo = _window_start(r0, rows) // 2 * 2
        k_hi = -(-(_window_start(r0 + chunk_rows - 1, rows) + WIN_ROWS) // 2) * 2
        keys = slice(k_lo * GRID_W, k_hi * GRID_W)
        qrows = slice(qc * LAT_Q_CHUNK, (qc + 1) * LAT_Q_CHUNK)
        q = q_scr[hd, qrows, :]
        s_loc = _dot(q, kvT_scr[k_rows, keys]) + bias_scr[qrows, keys]
        s_ctx = _dot(q, ckT)
        p_loc, p_ctx = _softmax_rows([s_loc, s_ctx])
        attT_scr[k_rows, qrows] = _dot_nt(kvT_scr[v_rows, keys], p_loc) + _dot_nt(cvT, p_ctx)

    @pl.when(hd == N_HEADS - 1)
    def _():
        _merge_and_norm(x_ref[...], mod, attT_scr, conv_scr, wout_ref, g_ref, b_ref, o_ref)


def _lat_mixer(l, xs, mod, w_in, w_out, conv_w, ln_g, ln_b, cache_kT, cache_vT, tbl):
    batch, seq_len, _ = xs.shape
    past = cache_kT.shape[-1]
    n_tbl = tbl.shape[-1]
    cache_spec = pl.BlockSpec((None, None, None, HEAD_DIM, past), lambda b, h: (b, l, h, 0, 0))
    return pl.pallas_call(
        _lat_mixer_kernel,
        grid=(batch, N_HEADS),
        in_specs=[
            pl.BlockSpec((None, seq_len, D_MODEL), lambda b, h: (b, 0, 0)),
            pl.BlockSpec((None, 8, 6 * D_MODEL), lambda b, h: (l, 0, 0)),
            _weight_spec((None, D_MODEL, 3 * ATT_WIDTH + 3 * CONV_WIDTH), lambda b, h: (l, 0, 0)),
            _weight_spec((None, D_MODEL, D_MODEL), lambda b, h: (l, 0, 0)),
            pl.BlockSpec((None, 3, CONV_WIDTH), lambda b, h: (l, 0, 0)),
            pl.BlockSpec((None, 1, D_MODEL), lambda b, h: (l, 0, 0)),
            pl.BlockSpec((None, 1, D_MODEL), lambda b, h: (l, 0, 0)),
            cache_spec, cache_spec,
            pl.BlockSpec((None, None, GRID_W, n_tbl), lambda b, h: (l, h, 0, 0)),
        ],
        out_specs=pl.BlockSpec((None, seq_len, D_MODEL), lambda b, h: (b, 0, 0)),
        out_shape=jax.ShapeDtypeStruct(xs.shape, F32),
        scratch_shapes=[
            pltpu.VMEM((N_HEADS, seq_len, HEAD_DIM), BF16),
            pltpu.VMEM((2 * ATT_WIDTH, seq_len), BF16),
            pltpu.VMEM((ATT_WIDTH, seq_len), F32),
            pltpu.VMEM((seq_len, CONV_WIDTH), BF16),
            pltpu.VMEM((seq_len, seq_len), F32),
        ],
        compiler_params=pltpu.CompilerParams(
            dimension_semantics=("arbitrary", "arbitrary"), vmem_limit_bytes=VMEM_LIMIT_BYTES),
        name="lat_mixer",
    )(xs, mod, w_in, w_out, conv_w, ln_g, ln_b, cache_kT, cache_vT, tbl)


def _bias_table(rpb):
    depth, heads, n_dr, n_dc = rpb.shape
    lane0 = GRID_W - WIN_COLS
    rpb_pad = jnp.pad(rpb, ((0, 0), (0, 0), (0, 16 - n_dr), (lane0, LANES - lane0 - n_dc)))
    rpb_pad = rpb_pad.reshape(depth * heads, 16, LANES)

    def body(r_ref, o_ref):
        c = lax.broadcasted_iota(jnp.int32, (GRID_W, GRID_W), 0)
        cp = lax.broadcasted_iota(jnp.int32, (GRID_W, GRID_W), 1)
        col_start = jnp.clip(c - WIN_COLS // 2, 0, GRID_W - WIN_COLS)
        valid = (cp >= col_start) & (cp < col_start + WIN_COLS)
        for dr in range(n_dr):
            row = jnp.broadcast_to(r_ref[dr:dr + 1, :], (GRID_W, LANES))
            skew = pltpu.roll(row, LANES - GRID_W + 1, 1, stride=1, stride_axis=0)
            o_ref[:, dr * GRID_W:(dr + 1) * GRID_W] = jnp.where(valid, skew[:, :GRID_W], NEG_INF)

    tbl = pl.pallas_call(
        body,
        grid=(depth * heads,),
        in_specs=[pl.BlockSpec((None, 16, LANES), lambda i: (i, 0, 0))],
        out_specs=pl.BlockSpec((None, GRID_W, n_dr * GRID_W), lambda i: (i, 0, 0)),
        out_shape=jax.ShapeDtypeStruct((depth * heads, GRID_W, n_dr * GRID_W), F32),
        compiler_params=pltpu.CompilerParams(dimension_semantics=("arbitrary",)),
        name="bias_table",
    )(rpb_pad)
    return tbl.reshape(depth, heads, GRID_W, n_dr * GRID_W)


def _route(logits):
    lane = lax.broadcasted_iota(jnp.int32, logits.shape, 1)
    lane_f = lane.astype(F32)
    big = jnp.float32(LANES)

    def first_lane(cond):
        return jnp.min(jnp.where(cond, lane_f, big), axis=-1, keepdims=True)

    gmask = (lane >= N_EXPERTS) & (lane < N_EXPERTS + N_GROUPS)
    gl = jnp.where(gmask, logits, NEG_INF)
    gexp = jnp.exp(gl - jnp.max(gl, axis=-1, keepdims=True))
    gprob = gexp / jnp.sum(gexp, axis=-1, keepdims=True)
    g_p = jnp.max(gprob, axis=-1, keepdims=True)
    g_idx = first_lane(gmask & (gprob == g_p)) - N_EXPERTS

    lane_group = jnp.floor(lane_f * (1.0 / EXPERTS_PER_GROUP))
    emask = (lane < N_EXPERTS) & (lane_group == g_idx)
    el = jnp.where(emask, logits, NEG_INF)
    eexp = jnp.exp(el - jnp.max(el, axis=-1, keepdims=True))
    eprob = eexp / jnp.sum(eexp, axis=-1, keepdims=True)
    p1 = jnp.max(eprob, axis=-1, keepdims=True)
    i1 = first_lane(emask & (eprob == p1))
    rest = emask & (lane_f != i1)
    p2 = jnp.max(jnp.where(rest, eprob, -1.0), axis=-1, keepdims=True)
    i2 = first_lane(rest & (eprob == p2))
    denom = p1 + p2
    return (jnp.where(lane_f == i1, g_p * p1 / denom, 0.0)
            + jnp.where(lane_f == i2, g_p * p2 / denom, 0.0))


def _moe_kernel(x_ref, mod_ref, wr_ref, br_ref, win_ref, wout_ref, g_ref, b_ref, o_ref,
                h_scr, gate_scr, acc_scr):
    e = pl.program_id(1)

    @pl.when(e == 0)
    def _():
        sf = mod_ref[:, 3 * D_MODEL:4 * D_MODEL]
        cf = mod_ref[:, 4 * D_MODEL:5 * D_MODEL]
        h = x_ref[...] * (1.0 + cf) + sf
        h_hi = h.astype(BF16)
        h_scr[...] = h_hi
        h_lo = (h - h_hi.astype(F32)).astype(BF16)
        wr = wr_ref[...]
        wr_hi = wr.astype(BF16)
        wr_lo = (wr - wr_hi.astype(F32)).astype(BF16)
        logits = (_dot(h_hi, wr_hi) + _dot(h_lo, wr_hi) + _dot(h_hi, wr_lo)) + br_ref[...]
        gate_scr[...] = _route(logits)
        acc_scr[...] = jnp.zeros_like(acc_scr)

    hid = _dot(h_scr[...], win_ref[...].astype(BF16))
    a = hid[:, :D_EXPERT]
    up = hid[:, D_EXPERT:]
    lane = lax.broadcasted_iota(jnp.int32, gate_scr.shape, 1)
    gate_e = jnp.sum(jnp.where(lane == e, gate_scr[...], 0.0), axis=-1, keepdims=True)
    act = _silu(a) * up * gate_e
    acc_scr[...] += _dot(act.astype(BF16), wout_ref[...].astype(BF16))

    @pl.when(e == N_EXPERTS - 1)
    def _():
        gf = mod_ref[:, 5 * D_MODEL:6 * D_MODEL]
        o_ref[...] = _layer_norm(ALPHA * x_ref[...] + gf * acc_scr[...], g_ref[...], b_ref[...])


def _moe(l, x, mod_rows, w_router, b_router, w_exp_in, w_exp_out, ln_g, ln_b):
    n = x.shape[0]
    tm = MOE_TOKENS
    return pl.pallas_call(
        _moe_kernel,
        grid=(n // tm, N_EXPERTS),
        in_specs=[
            pl.BlockSpec((tm, D_MODEL), lambda i, e: (i, 0)),
            pl.BlockSpec((None, None, 1, 6 * D_MODEL), lambda i, e: (l, i, 0, 0)),
            pl.BlockSpec((None, D_MODEL, LANES), lambda i, e: (l, 0, 0)),
            pl.BlockSpec((None, 1, LANES), lambda i, e: (l, 0, 0)),
            pl.BlockSpec((None, None, D_MODEL, 2 * D_EXPERT), lambda i, e: (l, e, 0, 0)),
            pl.BlockSpec((None, None, D_EXPERT, D_MODEL), lambda i, e: (l, e, 0, 0)),
            pl.BlockSpec((None, 1, D_MODEL), lambda i, e: (l, 0, 0)),
            pl.BlockSpec((None, 1, D_MODEL), lambda i, e: (l, 0, 0)),
        ],
        out_specs=pl.BlockSpec((tm, D_MODEL), lambda i, e: (i, 0)),
        out_shape=jax.ShapeDtypeStruct(x.shape, F32),
        scratch_shapes=[
            pltpu.VMEM((tm, D_MODEL), BF16),
            pltpu.VMEM((tm, LANES), F32),
            pltpu.VMEM((tm, D_MODEL), F32),
        ],
        compiler_params=pltpu.CompilerParams(
            dimension_semantics=("arbitrary", "arbitrary"), vmem_limit_bytes=VMEM_LIMIT_BYTES),
        name="hier_moe",
    )(x, mod_rows, w_router, b_router, w_exp_in, w_exp_out, ln_g, ln_b)


def kernel(x_prompt, x_sample, cache_k, cache_v, c, c_ctx, w_ada, b_ada, w_in, conv_w, rpb, w_out,
           ln1_g, ln1_b, w_router_group, b_router_group, w_router_expert, b_router_expert,
           w_expert_in, w_expert_out, ln2_g, ln2_b):
    batch, seq_len, _ = x_prompt.shape
    dec_batch, dec_seq, _ = x_sample.shape
    assert dec_batch + 1 <= 8 and dec_seq == MOE_TOKENS and (batch * seq_len) % MOE_TOKENS == 0

    cond = jnp.concatenate([c_ctx[None, :], c, jnp.zeros((8 - 1 - dec_batch, D_MODEL), F32)], axis=0)
    mod = _modulation(cond, w_ada, b_ada)
    n_ctx_tiles = batch * seq_len // MOE_TOKENS
    mod_rows_ctx = jnp.broadcast_to(mod[:, 0:1, None, :], (DEPTH, n_ctx_tiles, 1, 6 * D_MODEL))
    mod_rows_lat = mod[:, 1:1 + dec_batch, None, :]

    w_in_bf = w_in.astype(BF16)
    w_out_bf = w_out.astype(BF16)
    pad = jnp.zeros((DEPTH, D_MODEL, LANES - N_EXPERTS - N_GROUPS), F32)
    w_router = jnp.concatenate([w_router_expert, w_router_group, pad], axis=-1)
    b_router = jnp.concatenate([b_router_expert, b_router_group, pad[:, 0, :]], axis=-1)[:, None, :]
    tbl = _bias_table(rpb)
    cache_kT = jnp.swapaxes(cache_k, -1, -2)
    cache_vT = jnp.swapaxes(cache_v, -1, -2)
    ln1_g3, ln1_b3 = ln1_g[:, None, :], ln1_b[:, None, :]
    ln2_g3, ln2_b3 = ln2_g[:, None, :], ln2_b[:, None, :]

    xp, xs = x_prompt, x_sample
    kv_bufs = None
    for l in range(DEPTH):
        moe = functools.partial(_moe, l, w_router=w_router, b_router=b_router, w_exp_in=w_expert_in,
                                w_exp_out=w_expert_out, ln_g=ln2_g3, ln_b=ln2_b3)
        xp, *kv_bufs = _ctx_mixer(l, xp, mod, w_in_bf, w_out_bf, conv_w, ln1_g3, ln1_b3, kv_bufs)
        xp = moe(xp.reshape(-1, D_MODEL), mod_rows_ctx).reshape(xp.shape)
        xs = _lat_mixer(l, xs, mod, w_in_bf, w_out_bf, conv_w, ln1_g3, ln1_b3, cache_kT, cache_vT, tbl)
        xs = moe(xs.reshape(-1, D_MODEL), mod_rows_lat).reshape(xs.shape)
    new_kT, new_vT = kv_bufs
    return (xp, xs, jnp.swapaxes(new_kT, -1, -2), jnp.swapaxes(new_vT, -1, -2))
```

```python
import functools

import jax
import jax.numpy as jnp
from jax import lax
from jax.experimental import pallas as pl
from jax.experimental.pallas import tpu as pltpu

D_MODEL = 1024
DEPTH = 4
GRID_W = 64
ATT_WIDTH = D_MODEL // 2
CONV_WIDTH = D_MODEL - ATT_WIDTH
HEAD_DIM = 64
N_HEADS = ATT_WIDTH // HEAD_DIM
WIN_ROWS = 8
WIN_COLS = 16
N_GROUPS = 4
EXPERTS_PER_GROUP = 4
N_EXPERTS = N_GROUPS * EXPERTS_PER_GROUP
D_EXPERT = D_MODEL // 4
ALPHA = (2 * DEPTH) ** 0.25
LN_EPS = 1e-5
NEG_INF = -1e30
QK_SCALE = HEAD_DIM ** -0.5

F32 = jnp.float32
BF16 = jnp.bfloat16

LANES = 128
VMEM_LIMIT_BYTES = 56 * 1024 * 1024

CTX_SEQ_PER_STEP = 2
LAT_Q_CHUNK = 256
MOE_TOKENS = 1024
MOD_COLS = 1024


def _dot(a, b):
    return jnp.dot(a, b, preferred_element_type=F32)


def _dot_nt(a, b):
    return lax.dot_general(a, b, (((1,), (1,)), ((), ())), preferred_element_type=F32)


def _silu(x):
    return x * (1.0 / (1.0 + jnp.exp(-x)))


def _layer_norm(r, g, b):
    mu = jnp.mean(r, axis=-1, keepdims=True)
    d = r - mu
    var = jnp.mean(d * d, axis=-1, keepdims=True)
    return d * lax.rsqrt(var + LN_EPS) * g + b


def _mod_kernel(cond_ref, w_ref, b_ref, o_ref):
    s = _silu(cond_ref[...]).astype(BF16)
    o_ref[...] = _dot(s, w_ref[...].astype(BF16)) + b_ref[...]


def _modulation(cond, w_ada, b_ada):
    n_out = w_ada.shape[-1]
    return pl.pallas_call(
        _mod_kernel,
        grid=(DEPTH, n_out // MOD_COLS),
        in_specs=[
            pl.BlockSpec((8, D_MODEL), lambda l, j: (0, 0)),
            pl.BlockSpec((None, D_MODEL, MOD_COLS), lambda l, j: (l, 0, j)),
            pl.BlockSpec((None, 1, MOD_COLS), lambda l, j: (l, 0, j)),
        ],
        out_specs=pl.BlockSpec((None, 8, MOD_COLS), lambda l, j: (l, 0, j)),
        out_shape=jax.ShapeDtypeStruct((DEPTH, 8, n_out), F32),
        compiler_params=pltpu.CompilerParams(
            dimension_semantics=("arbitrary", "arbitrary"),
            vmem_limit_bytes=VMEM_LIMIT_BYTES),
        name="adaln_modulation",
    )(cond, w_ada, b_ada.reshape(DEPTH, 1, n_out))


def _project(x, mod, win_ref, convw_ref, qT_scr, k_scr, vT_scr, conv_scr, seq_len, kT_out=None, vT_out=None):
    m = x.shape[0]
    sa = mod[:, 0:D_MODEL]
    ca = mod[:, D_MODEL:2 * D_MODEL]
    h = (x * (1.0 + ca) + sa).astype(BF16)

    qT_scr[...] = (_dot(h, win_ref[:, 0:ATT_WIDTH]) * QK_SCALE).T.astype(BF16)
    zk = _dot(h, win_ref[:, ATT_WIDTH:2 * ATT_WIDTH])
    for hd in range(N_HEADS):
        k_scr[hd] = zk[:, hd * HEAD_DIM:(hd + 1) * HEAD_DIM].astype(BF16)
    zvT = _dot(h, win_ref[:, 2 * ATT_WIDTH:3 * ATT_WIDTH]).T
    vT_scr[...] = zvT.astype(BF16)
    if kT_out is not None:
        zkT = zk.T
        for out, zT in ((kT_out, zkT), (vT_out, zvT)):
            for s in range(m // seq_len):
                for hd in range(N_HEADS):
                    out[s, hd] = zT[hd * HEAD_DIM:(hd + 1) * HEAD_DIM, s * seq_len:(s + 1) * seq_len]

    c0 = 3 * ATT_WIDTH
    bg = _dot(h, win_ref[:, c0:c0 + CONV_WIDTH])
    cg = _dot(h, win_ref[:, c0 + CONV_WIDTH:c0 + 2 * CONV_WIDTH])
    u = _dot(h, win_ref[:, c0 + 2 * CONV_WIDTH:c0 + 3 * CONV_WIDTH])
    y = cg * u
    t = lax.broadcasted_iota(jnp.int32, (m, 1), 0) % seq_len
    y_prev = jnp.where(t == 0, 0.0, pltpu.roll(y, 1, 0))
    y_next = jnp.where(t == seq_len - 1, 0.0, pltpu.roll(y, m - 1, 0))
    cw = convw_ref[...]
    conv = cw[0:1, :] * y_prev + cw[1:2, :] * y + cw[2:3, :] * y_next
    conv_scr[...] = (bg * conv).astype(BF16)


def _merge_and_norm(x, mod, attT_scr, conv_scr, wout_ref, g_ref, b_ref, o_ref):
    att = attT_scr[...].T.astype(BF16)
    mix = _dot(att, wout_ref[0:ATT_WIDTH, :]) + _dot(conv_scr[...], wout_ref[ATT_WIDTH:, :])
    ga = mod[:, 2 * D_MODEL:3 * D_MODEL]
    o_ref[...] = _layer_norm(ALPHA * x + ga * mix, g_ref[...], b_ref[...]).reshape(o_ref.shape)


def _softmax_keys(parts):
    mx = functools.reduce(jnp.maximum, [jnp.max(p, axis=0, keepdims=True) for p in parts])
    es = [jnp.exp(p - mx) for p in parts]
    inv = 1.0 / functools.reduce(jnp.add, [jnp.sum(e, axis=0, keepdims=True) for e in es])
    return [(e * inv).astype(BF16) for e in es]


def _head_rows(hd):
    return pl.ds(pl.multiple_of(hd * HEAD_DIM, HEAD_DIM), HEAD_DIM)


def _weight_spec(shape, index_map):
    return pl.BlockSpec(shape, index_map, pipeline_mode=pl.Buffered(1))


def _ctx_mixer_kernel(n_alias, x_ref, mod_ref, win_ref, wout_ref, convw_ref, g_ref, b_ref, *rest):
    o_ref, kT_ref, vT_ref, qT_scr, k_scr, vT_scr, attT_scr, conv_scr, sT_scr, pT_scr = rest[n_alias:]
    sb, seq_len, _ = x_ref.shape
    x = x_ref[...].reshape(sb * seq_len, D_MODEL)
    mod = mod_ref[0:1, :]
    _project(x, mod, win_ref, convw_ref, qT_scr, k_scr, vT_scr, conv_scr, seq_len, kT_ref, vT_ref)

    pairs = [(hd, s) for hd in range(N_HEADS) for s in range(sb)]
    chan = lambda hd: slice(hd * HEAD_DIM, (hd + 1) * HEAD_DIM)
    tok = lambda s: slice(s * seq_len, (s + 1) * seq_len)
    for i, (hd, s) in enumerate(pairs):
        sT_scr[i] = _dot(k_scr[hd, tok(s), :], qT_scr[chan(hd), tok(s)])
    for i in range(len(pairs)):
        (pT_scr[i],) = _softmax_keys([sT_scr[i]])
    for i, (hd, s) in enumerate(pairs):
        attT_scr[chan(hd), tok(s)] = _dot(vT_scr[chan(hd), tok(s)], pT_scr[i])

    _merge_and_norm(x, mod, attT_scr, conv_scr, wout_ref, g_ref, b_ref, o_ref)


def _ctx_mixer(l, xp, mod, w_in, w_out, conv_w, ln_g, ln_b, kv_bufs):
    batch, seq_len, _ = xp.shape
    sb = CTX_SEQ_PER_STEP
    m = sb * seq_len
    kv_shape = jax.ShapeDtypeStruct((batch, DEPTH, N_HEADS, HEAD_DIM, seq_len), F32)
    kv_spec = pl.BlockSpec((sb, None, N_HEADS, HEAD_DIM, seq_len), lambda i: (i, l, 0, 0, 0))
    in_specs = [
        pl.BlockSpec((sb, seq_len, D_MODEL), lambda i: (i, 0, 0)),
        pl.BlockSpec((None, 8, 6 * D_MODEL), lambda i: (l, 0, 0)),
        _weight_spec((None, D_MODEL, 3 * ATT_WIDTH + 3 * CONV_WIDTH), lambda i: (l, 0, 0)),
        _weight_spec((None, D_MODEL, D_MODEL), lambda i: (l, 0, 0)),
        pl.BlockSpec((None, 3, CONV_WIDTH), lambda i: (l, 0, 0)),
        pl.BlockSpec((None, 1, D_MODEL), lambda i: (l, 0, 0)),
        pl.BlockSpec((None, 1, D_MODEL), lambda i: (l, 0, 0)),
    ]
    args = [xp, mod, w_in, w_out, conv_w, ln_g, ln_b]
    aliases = {}
    if kv_bufs is not None:
        aliases = {len(args): 1, len(args) + 1: 2}
        in_specs += [pl.BlockSpec(memory_space=pl.ANY)] * 2
        args += list(kv_bufs)
    return pl.pallas_call(
        functools.partial(_ctx_mixer_kernel, len(aliases)),
        grid=(batch // sb,),
        in_specs=in_specs,
        out_specs=[pl.BlockSpec((sb, seq_len, D_MODEL), lambda i: (i, 0, 0)), kv_spec, kv_spec],
        out_shape=[jax.ShapeDtypeStruct(xp.shape, F32), kv_shape, kv_shape],
        input_output_aliases=aliases,
        scratch_shapes=[
            pltpu.VMEM((ATT_WIDTH, m), BF16),
            pltpu.VMEM((N_HEADS, m, HEAD_DIM), BF16),
            pltpu.VMEM((ATT_WIDTH, m), BF16),
            pltpu.VMEM((ATT_WIDTH, m), F32),
            pltpu.VMEM((m, CONV_WIDTH), BF16),
            pltpu.VMEM((N_HEADS * sb, seq_len, seq_len), F32),
            pltpu.VMEM((N_HEADS * sb, seq_len, seq_len), BF16),
        ],
        compiler_params=pltpu.CompilerParams(
            dimension_semantics=("arbitrary",), vmem_limit_bytes=VMEM_LIMIT_BYTES),
        name="ctx_mixer",
    )(*args)


def _window_start(r, rows):
    return min(max(r - WIN_ROWS // 2, 0), rows - WIN_ROWS)


def _chunk_key_rows(r0, chunk_rows, rows):
    lo = _window_start(r0, rows) // 2 * 2
    hi = -(-(_window_start(r0 + chunk_rows - 1, rows) + WIN_ROWS) // 2) * 2
    return lo, hi


def _lat_mixer_kernel(x_ref, mod_ref, win_ref, wout_ref, convw_ref, g_ref, b_ref,
                      ckT_ref, cvT_ref, tbl_ref, o_ref,
                      qT_scr, k_scr, vT_scr, attT_scr, conv_scr, sT_scr, pT_scr):
    b = pl.program_id(0)
    hd = pl.program_id(1)
    seq_len = x_ref.shape[0]
    rows = seq_len // GRID_W
    mod = mod_ref[pl.ds(1 + b, 1), :]

    @pl.when(hd == 0)
    def _():
        _project(x_ref[...], mod, win_ref, convw_ref, qT_scr, k_scr, vT_scr, conv_scr, seq_len)

    chan = _head_rows(hd)
    ckT = ckT_ref[...]
    ck = jnp.concatenate([ckT, jnp.zeros_like(ckT)], axis=0).T.astype(BF16)
    cvT = cvT_ref[...].astype(BF16)
    low_half = lax.broadcasted_iota(jnp.int32, (GRID_W, 2 * GRID_W), 1) < GRID_W

    def bias_block(r_pair, rk):
        halves = []
        for r in (r_pair, r_pair + 1):
            rs = _window_start(r, rows)
            if rs <= rk < rs + WIN_ROWS:
                dr = rk - r + WIN_ROWS - 1
                halves.append(tbl_ref[dr * GRID_W:(dr + 1) * GRID_W, :])
            else:
                halves.append(jnp.full((GRID_W, 2 * GRID_W), NEG_INF, F32))
        return jnp.where(low_half, halves[0], halves[1])

    chunk_rows = LAT_Q_CHUNK // GRID_W
    past = ck.shape[0]
    max_loc = sT_scr.shape[1] - past
    chunks = []
    for qc in range(seq_len // LAT_Q_CHUNK):
        r0 = qc * chunk_rows
        k_lo, k_hi = _chunk_key_rows(r0, chunk_rows, rows)
        chunks.append((qc, r0, k_lo, k_hi, slice(qc * LAT_Q_CHUNK, (qc + 1) * LAT_Q_CHUNK)))
    for qc, r0, k_lo, k_hi, qcols in chunks:
        qT = qT_scr[chan, qcols]
        bias = jnp.concatenate(
            [jnp.concatenate([bias_block(r0 + j, rk) for j in range(0, chunk_rows, 2)], axis=1)
             for rk in range(k_lo, k_hi)], axis=0)
        sT_scr[qc, 0:(k_hi - k_lo) * GRID_W] = _dot(k_scr[hd, k_lo * GRID_W:k_hi * GRID_W, :], qT) + bias
        sT_scr[qc, max_loc:] = _dot(ck, jnp.concatenate([qT, jnp.zeros_like(qT)], axis=0))
    for qc, r0, k_lo, k_hi, qcols in chunks:
        n_loc = (k_hi - k_lo) * GRID_W
        pT_scr[qc, 0:n_loc], pT_scr[qc, max_loc:] = _softmax_keys([sT_scr[qc, 0:n_loc], sT_scr[qc, max_loc:]])
    for qc, r0, k_lo, k_hi, qcols in chunks:
        n_loc = (k_hi - k_lo) * GRID_W
        attT_scr[chan, qcols] = (_dot(vT_scr[chan, k_lo * GRID_W:k_hi * GRID_W], pT_scr[qc, 0:n_loc])
                                 + _dot(cvT, pT_scr[qc, max_loc:]))

    @pl.when(hd == N_HEADS - 1)
    def _():
        _merge_and_norm(x_ref[...], mod, attT_scr, conv_scr, wout_ref, g_ref, b_ref, o_ref)


def _lat_mixer(l, xs, mod, w_in, w_out, conv_w, ln_g, ln_b, cache_kT, cache_vT, tbl):
    batch, seq_len, _ = xs.shape
    past = cache_kT.shape[-1]
    rows, chunk_rows, n_chunks = seq_len // GRID_W, LAT_Q_CHUNK // GRID_W, seq_len // LAT_Q_CHUNK
    max_loc = GRID_W * max(hi - lo for lo, hi in
                           (_chunk_key_rows(qc * chunk_rows, chunk_rows, rows) for qc in range(n_chunks)))
    cache_spec = pl.BlockSpec((None, None, None, HEAD_DIM, past), lambda b, h: (b, l, h, 0, 0))
    return pl.pallas_call(
        _lat_mixer_kernel,
        grid=(batch, N_HEADS),
        in_specs=[
            pl.BlockSpec((None, seq_len, D_MODEL), lambda b, h: (b, 0, 0)),
            pl.BlockSpec((None, 8, 6 * D_MODEL), lambda b, h: (l, 0, 0)),
            _weight_spec((None, D_MODEL, 3 * ATT_WIDTH + 3 * CONV_WIDTH), lambda b, h: (l, 0, 0)),
            _weight_spec((None, D_MODEL, D_MODEL), lambda b, h: (l, 0, 0)),
            pl.BlockSpec((None, 3, CONV_WIDTH), lambda b, h: (l, 0, 0)),
            pl.BlockSpec((None, 1, D_MODEL), lambda b, h: (l, 0, 0)),
            pl.BlockSpec((None, 1, D_MODEL), lambda b, h: (l, 0, 0)),
            cache_spec, cache_spec,
            pl.BlockSpec((None, None) + tbl.shape[2:], lambda b, h: (l, h, 0, 0)),
        ],
        out_specs=pl.BlockSpec((None, seq_len, D_MODEL), lambda b, h: (b, 0, 0)),
        out_shape=jax.ShapeDtypeStruct(xs.shape, F32),
        scratch_shapes=[
            pltpu.VMEM((ATT_WIDTH, seq_len), BF16),
            pltpu.VMEM((N_HEADS, seq_len, HEAD_DIM), BF16),
            pltpu.VMEM((ATT_WIDTH, seq_len), BF16),
            pltpu.VMEM((ATT_WIDTH, seq_len), F32),
            pltpu.VMEM((seq_len, CONV_WIDTH), BF16),
            pltpu.VMEM((n_chunks, max_loc + past, LAT_Q_CHUNK), F32),
            pltpu.VMEM((n_chunks, max_loc + past, LAT_Q_CHUNK), BF16),
        ],
        compiler_params=pltpu.CompilerParams(
            dimension_semantics=("arbitrary", "arbitrary"), vmem_limit_bytes=VMEM_LIMIT_BYTES),
        name="lat_mixer",
    )(xs, mod, w_in, w_out, conv_w, ln_g, ln_b, cache_kT, cache_vT, tbl)


def _bias_table(rpb):
    depth, heads, n_dr, n_dc = rpb.shape
    lane0 = GRID_W - WIN_COLS
    rpb_pad = jnp.pad(rpb[..., ::-1], ((0, 0), (0, 0), (0, 16 - n_dr), (lane0, LANES - lane0 - n_dc)))
    rpb_pad = rpb_pad.reshape(depth * heads, 16, LANES)

    def body(r_ref, o_ref):
        cp = lax.broadcasted_iota(jnp.int32, (GRID_W, LANES), 0)
        lane = lax.broadcasted_iota(jnp.int32, (GRID_W, LANES), 1)
        low_half = lane < GRID_W
        c = jnp.where(low_half, lane, lane - GRID_W)
        col_start = jnp.clip(c - WIN_COLS // 2, 0, GRID_W - WIN_COLS)
        valid = (cp >= col_start) & (cp < col_start + WIN_COLS)
        for dr in range(n_dr):
            row = jnp.broadcast_to(r_ref[dr:dr + 1, :], (GRID_W, LANES))
            lo = pltpu.roll(row, LANES - GRID_W + 1, 1, stride=1, stride_axis=0)
            hi = pltpu.roll(row, 1, 1, stride=1, stride_axis=0)
            o_ref[dr * GRID_W:(dr + 1) * GRID_W, :] = jnp.where(valid, jnp.where(low_half, lo, hi), NEG_INF)

    tbl = pl.pallas_call(
        body,
        grid=(depth * heads,),
        in_specs=[pl.BlockSpec((None, 16, LANES), lambda i: (i, 0, 0))],
        out_specs=pl.BlockSpec((None, n_dr * GRID_W, LANES), lambda i: (i, 0, 0)),
        out_shape=jax.ShapeDtypeStruct((depth * heads, n_dr * GRID_W, LANES), F32),
        compiler_params=pltpu.CompilerParams(dimension_semantics=("arbitrary",)),
        name="bias_table",
    )(rpb_pad)
    return tbl.reshape(depth, heads, n_dr * GRID_W, LANES)


def _route(logits):
    lane = lax.broadcasted_iota(jnp.int32, logits.shape, 1)
    lane_f = lane.astype(F32)
    big = jnp.float32(LANES)

    def first_lane(cond):
        return jnp.min(jnp.where(cond, lane_f, big), axis=-1, keepdims=True)

    gmask = (lane >= N_EXPERTS) & (lane < N_EXPERTS + N_GROUPS)
    gl = jnp.where(gmask, logits, NEG_INF)
    gexp = jnp.exp(gl - jnp.max(gl, axis=-1, keepdims=True))
    gprob = gexp / jnp.sum(gexp, axis=-1, keepdims=True)
    g_p = jnp.max(gprob, axis=-1, keepdims=True)
    g_idx = first_lane(gmask & (gprob == g_p)) - N_EXPERTS

    lane_group = jnp.floor(lane_f * (1.0 / EXPERTS_PER_GROUP))
    emask = (lane < N_EXPERTS) & (lane_group == g_idx)
    el = jnp.where(emask, logits, NEG_INF)
    eexp = jnp.exp(el - jnp.max(el, axis=-1, keepdims=True))
    eprob = eexp / jnp.sum(eexp, axis=-1, keepdims=True)
    p1 = jnp.max(eprob, axis=-1, keepdims=True)
    i1 = first_lane(emask & (eprob == p1))
    rest = emask & (lane_f != i1)
    p2 = jnp.max(jnp.where(rest, eprob, -1.0), axis=-1, keepdims=True)
    i2 = first_lane(rest & (eprob == p2))
    denom = p1 + p2
    return (jnp.where(lane_f == i1, g_p * p1 / denom, 0.0)
            + jnp.where(lane_f == i2, g_p * p2 / denom, 0.0))


def _moe_kernel(x_ref, mod_ref, wr_ref, br_ref, win_ref, wout_ref, g_ref, b_ref, o_ref,
                h_scr, gate_scr, acc_scr):
    e = pl.program_id(1)

    @pl.when(e == 0)
    def _():
        sf = mod_ref[:, 3 * D_MODEL:4 * D_MODEL]
        cf = mod_ref[:, 4 * D_MODEL:5 * D_MODEL]
        h = x_ref[...] * (1.0 + cf) + sf
        h_hi = h.astype(BF16)
        h_scr[...] = h_hi
        h_lo = (h - h_hi.astype(F32)).astype(BF16)
        wr = wr_ref[...]
        wr_hi = wr.astype(BF16)
        wr_lo = (wr - wr_hi.astype(F32)).astype(BF16)
        logits = (_dot(h_hi, wr_hi) + _dot(h_lo, wr_hi) + _dot(h_hi, wr_lo)) + br_ref[...]
        gate_scr[...] = _route(logits)
        acc_scr[...] = jnp.zeros_like(acc_scr)

    hid = _dot(h_scr[...], win_ref[...].astype(BF16))
    a = hid[:, :D_EXPERT]
    up = hid[:, D_EXPERT:]
    lane = lax.broadcasted_iota(jnp.int32, gate_scr.shape, 1)
    gate_e = jnp.sum(jnp.where(lane == e, gate_scr[...], 0.0), axis=-1, keepdims=True)
    act = _silu(a) * up * gate_e
    acc_scr[...] += _dot(act.astype(BF16), wout_ref[...].astype(BF16))

    @pl.when(e == N_EXPERTS - 1)
    def _():
        gf = mod_ref[:, 5 * D_MODEL:6 * D_MODEL]
        o_ref[...] = _layer_norm(ALPHA * x_ref[...] + gf * acc_scr[...], g_ref[...], b_ref[...])


def _moe(l, x, mod_rows, w_router, b_router, w_exp_in, w_exp_out, ln_g, ln_b):
    n = x.shape[0]
    tm = MOE_TOKENS
    return pl.pallas_call(
        _moe_kernel,
        grid=(n // tm, N_EXPERTS),
        in_specs=[
            pl.BlockSpec((tm, D_MODEL), lambda i, e: (i, 0)),
            pl.BlockSpec((None, None, 1, 6 * D_MODEL), lambda i, e: (l, i, 0, 0)),
            pl.BlockSpec((None, D_MODEL, LANES), lambda i, e: (l, 0, 0)),
            pl.BlockSpec((None, 1, LANES), lambda i, e: (l, 0, 0)),
            pl.BlockSpec((None, None, D_MODEL, 2 * D_EXPERT), lambda i, e: (l, e, 0, 0)),
            pl.BlockSpec((None, None, D_EXPERT, D_MODEL), lambda i, e: (l, e, 0, 0)),
            pl.BlockSpec((None, 1, D_MODEL), lambda i, e: (l, 0, 0)),
            pl.BlockSpec((None, 1, D_MODEL), lambda i, e: (l, 0, 0)),
        ],
        out_specs=pl.BlockSpec((tm, D_MODEL), lambda i, e: (i, 0)),
        out_shape=jax.ShapeDtypeStruct(x.shape, F32),
        scratch_shapes=[
            pltpu.VMEM((tm, D_MODEL), BF16),
            pltpu.VMEM((tm, LANES), F32),
            pltpu.VMEM((tm, D_MODEL), F32),
        ],
        compiler_params=pltpu.CompilerParams(
            dimension_semantics=("arbitrary", "arbitrary"), vmem_limit_bytes=VMEM_LIMIT_BYTES),
        name="hier_moe",
    )(x, mod_rows, w_router, b_router, w_exp_in, w_exp_out, ln_g, ln_b)


def kernel(x_prompt, x_sample, cache_k, cache_v, c, c_ctx, w_ada, b_ada, w_in, conv_w, rpb, w_out,
           ln1_g, ln1_b, w_router_group, b_router_group, w_router_expert, b_router_expert,
           w_expert_in, w_expert_out, ln2_g, ln2_b):
    batch, seq_len, _ = x_prompt.shape
    dec_batch, dec_seq, _ = x_sample.shape
    assert dec_batch + 1 <= 8 and dec_seq == MOE_TOKENS and (batch * seq_len) % MOE_TOKENS == 0

    cond = jnp.concatenate([c_ctx[None, :], c, jnp.zeros((8 - 1 - dec_batch, D_MODEL), F32)], axis=0)
    mod = _modulation(cond, w_ada, b_ada)
    n_ctx_tiles = batch * seq_len // MOE_TOKENS
    mod_rows_ctx = jnp.broadcast_to(mod[:, 0:1, None, :], (DEPTH, n_ctx_tiles, 1, 6 * D_MODEL))
    mod_rows_lat = mod[:, 1:1 + dec_batch, None, :]

    w_in_bf = w_in.astype(BF16)
    w_out_bf = w_out.astype(BF16)
    pad = jnp.zeros((DEPTH, D_MODEL, LANES - N_EXPERTS - N_GROUPS), F32)
    w_router = jnp.concatenate([w_router_expert, w_router_group, pad], axis=-1)
    b_router = jnp.concatenate([b_router_expert, b_router_group, pad[:, 0, :]], axis=-1)[:, None, :]
    tbl = _bias_table(rpb)
    cache_kT = jnp.swapaxes(cache_k, -1, -2)
    cache_vT = jnp.swapaxes(cache_v, -1, -2)
    ln1_g3, ln1_b3 = ln1_g[:, None, :], ln1_b[:, None, :]
    ln2_g3, ln2_b3 = ln2_g[:, None, :], ln2_b[:, None, :]

    xp, xs = x_prompt, x_sample
    kv_bufs = None
    for l in range(DEPTH):
        moe = functools.partial(_moe, l, w_router=w_router, b_router=b_router, w_exp_in=w_expert_in,
                                w_exp_out=w_expert_out, ln_g=ln2_g3, ln_b=ln2_b3)
        xp, *kv_bufs = _ctx_mixer(l, xp, mod, w_in_bf, w_out_bf, conv_w, ln1_g3, ln1_b3, kv_bufs)
        xp = moe(xp.reshape(-1, D_MODEL), mod_rows_ctx).reshape(xp.shape)
        xs = _lat_mixer(l, xs, mod, w_in_bf, w_out_bf, conv_w, ln1_g3, ln1_b3, cache_kT, cache_vT, tbl)
        xs = moe(xs.reshape(-1, D_MODEL), mod_rows_lat).reshape(xs.shape)
    new_kT, new_vT = kv_bufs
    return (xp, xs, jnp.swapaxes(new_kT, -1, -2), jnp.swapaxes(new_vT, -1, -2))
```

```python
import functools

import jax
import jax.numpy as jnp
from jax import lax
from jax.experimental import pallas as pl
from jax.experimental.pallas import tpu as pltpu

D_MODEL = 1024
DEPTH = 4
GRID_W = 64
ATT_WIDTH = D_MODEL // 2
CONV_WIDTH = D_MODEL - ATT_WIDTH
HEAD_DIM = 64
N_HEADS = ATT_WIDTH // HEAD_DIM
WIN_ROWS = 8
WIN_COLS = 16
N_GROUPS = 4
EXPERTS_PER_GROUP = 4
N_EXPERTS = N_GROUPS * EXPERTS_PER_GROUP
D_EXPERT = D_MODEL // 4
ALPHA = (2 * DEPTH) ** 0.25
LN_EPS = 1e-5
NEG_INF = -1e30
QK_SCALE = HEAD_DIM ** -0.5

F32 = jnp.float32
BF16 = jnp.bfloat16

LANES = 128
VMEM_LIMIT_BYTES = 56 * 1024 * 1024

CTX_SEQ_PER_STEP = 2
LAT_Q_CHUNK = 256
MOE_TOKENS = 1024
MOE_BLOCK = 512
MOE_SLOT = 160
SCATTER_K = 256
GATE_COLS = 128
MOD_COLS = 1024


def _dot(a, b):
    return jnp.dot(a, b, preferred_element_type=F32)


def _dot_nt(a, b):
    return lax.dot_general(a, b, (((1,), (1,)), ((), ())), preferred_element_type=F32)


def _silu(x):
    return x * (1.0 / (1.0 + jnp.exp(-x)))


def _layer_norm(r, g, b):
    mu = jnp.mean(r, axis=-1, keepdims=True)
    d = r - mu
    var = jnp.mean(d * d, axis=-1, keepdims=True)
    return d * lax.rsqrt(var + LN_EPS) * g + b


def _mod_kernel(cond_ref, w_ref, b_ref, o_ref):
    s = _silu(cond_ref[...]).astype(BF16)
    o_ref[...] = _dot(s, w_ref[...].astype(BF16)) + b_ref[...]


def _modulation(cond, w_ada, b_ada):
    n_out = w_ada.shape[-1]
    return pl.pallas_call(
        _mod_kernel,
        grid=(DEPTH, n_out // MOD_COLS),
        in_specs=[
            pl.BlockSpec((8, D_MODEL), lambda l, j: (0, 0)),
            pl.BlockSpec((None, D_MODEL, MOD_COLS), lambda l, j: (l, 0, j)),
            pl.BlockSpec((None, 1, MOD_COLS), lambda l, j: (l, 0, j)),
        ],
        out_specs=pl.BlockSpec((None, 8, MOD_COLS), lambda l, j: (l, 0, j)),
        out_shape=jax.ShapeDtypeStruct((DEPTH, 8, n_out), F32),
        compiler_params=pltpu.CompilerParams(
            dimension_semantics=("arbitrary", "arbitrary"),
            vmem_limit_bytes=VMEM_LIMIT_BYTES),
        name="adaln_modulation",
    )(cond, w_ada, b_ada.reshape(DEPTH, 1, n_out))


def _project(x, mod, win_ref, convw_ref, qT_scr, k_scr, vT_scr, conv_scr, seq_len, kT_out=None, vT_out=None):
    m = x.shape[0]
    sa = mod[:, 0:D_MODEL]
    ca = mod[:, D_MODEL:2 * D_MODEL]
    h = (x * (1.0 + ca) + sa).astype(BF16)

    qT_scr[...] = (_dot(h, win_ref[:, 0:ATT_WIDTH]) * QK_SCALE).T.astype(BF16)
    zk = _dot(h, win_ref[:, ATT_WIDTH:2 * ATT_WIDTH])
    for hd in range(N_HEADS):
        k_scr[hd] = zk[:, hd * HEAD_DIM:(hd + 1) * HEAD_DIM].astype(BF16)
    zvT = _dot(h, win_ref[:, 2 * ATT_WIDTH:3 * ATT_WIDTH]).T
    vT_scr[...] = zvT.astype(BF16)
    if kT_out is not None:
        zkT = zk.T
        for out, zT in ((kT_out, zkT), (vT_out, zvT)):
            for s in range(m // seq_len):
                for hd in range(N_HEADS):
                    out[s, hd] = zT[hd * HEAD_DIM:(hd + 1) * HEAD_DIM, s * seq_len:(s + 1) * seq_len]

    c0 = 3 * ATT_WIDTH
    bg = _dot(h, win_ref[:, c0:c0 + CONV_WIDTH])
    cg = _dot(h, win_ref[:, c0 + CONV_WIDTH:c0 + 2 * CONV_WIDTH])
    u = _dot(h, win_ref[:, c0 + 2 * CONV_WIDTH:c0 + 3 * CONV_WIDTH])
    y = cg * u
    t = lax.broadcasted_iota(jnp.int32, (m, 1), 0) % seq_len
    y_prev = jnp.where(t == 0, 0.0, pltpu.roll(y, 1, 0))
    y_next = jnp.where(t == seq_len - 1, 0.0, pltpu.roll(y, m - 1, 0))
    cw = convw_ref[...]
    conv = cw[0:1, :] * y_prev + cw[1:2, :] * y + cw[2:3, :] * y_next
    conv_scr[...] = (bg * conv).astype(BF16)


def _merge_and_norm(x, mod, attT_scr, conv_scr, wout_ref, g_ref, b_ref, o_ref):
    att = attT_scr[...].T.astype(BF16)
    mix = _dot(att, wout_ref[0:ATT_WIDTH, :]) + _dot(conv_scr[...], wout_ref[ATT_WIDTH:, :])
    ga = mod[:, 2 * D_MODEL:3 * D_MODEL]
    o_ref[...] = _layer_norm(ALPHA * x + ga * mix, g_ref[...], b_ref[...]).reshape(o_ref.shape)


def _softmax_keys(parts):
    mx = functools.reduce(jnp.maximum, [jnp.max(p, axis=0, keepdims=True) for p in parts])
    es = [jnp.exp(p - mx) for p in parts]
    inv = 1.0 / functools.reduce(jnp.add, [jnp.sum(e, axis=0, keepdims=True) for e in es])
    return [(e * inv).astype(BF16) for e in es]


def _head_rows(hd):
    return pl.ds(pl.multiple_of(hd * HEAD_DIM, HEAD_DIM), HEAD_DIM)


def _weight_spec(shape, index_map):
    return pl.BlockSpec(shape, index_map, pipeline_mode=pl.Buffered(1))


def _ctx_mixer_kernel(x_ref, mod_ref, win_ref, wout_ref, convw_ref, g_ref, b_ref, kT_in, vT_in,
                      o_ref, kT_ref, vT_ref, qT_scr, k_scr, vT_scr, attT_scr, conv_scr, sT_scr, pT_scr):
    del kT_in, vT_in
    sb, seq_len, _ = x_ref.shape
    x = x_ref[...].reshape(sb * seq_len, D_MODEL)
    mod = mod_ref[0:1, :]
    _project(x, mod, win_ref, convw_ref, qT_scr, k_scr, vT_scr, conv_scr, seq_len, kT_ref, vT_ref)

    pairs = [(hd, s) for hd in range(N_HEADS) for s in range(sb)]
    chan = lambda hd: slice(hd * HEAD_DIM, (hd + 1) * HEAD_DIM)
    tok = lambda s: slice(s * seq_len, (s + 1) * seq_len)
    for i, (hd, s) in enumerate(pairs):
        sT_scr[i] = _dot(k_scr[hd, tok(s), :], qT_scr[chan(hd), tok(s)])
    for i in range(len(pairs)):
        (pT_scr[i],) = _softmax_keys([sT_scr[i]])
    for i, (hd, s) in enumerate(pairs):
        attT_scr[chan(hd), tok(s)] = _dot(vT_scr[chan(hd), tok(s)], pT_scr[i])

    _merge_and_norm(x, mod, attT_scr, conv_scr, wout_ref, g_ref, b_ref, o_ref)


def _ctx_mixer(l, xp, mod, w_in, w_out, conv_w, ln_g, ln_b, kv_bufs):
    batch, seq_len, _ = xp.shape
    sb = CTX_SEQ_PER_STEP
    m = sb * seq_len
    kv_shape = jax.ShapeDtypeStruct((batch, DEPTH, N_HEADS, HEAD_DIM, seq_len), F32)
    kv_spec = pl.BlockSpec((sb, None, N_HEADS, HEAD_DIM, seq_len), lambda i: (i, l, 0, 0, 0))
    in_specs = [
        pl.BlockSpec((sb, seq_len, D_MODEL), lambda i: (i, 0, 0)),
        pl.BlockSpec((None, 8, 6 * D_MODEL), lambda i: (l, 0, 0)),
        _weight_spec((None, D_MODEL, 3 * ATT_WIDTH + 3 * CONV_WIDTH), lambda i: (l, 0, 0)),
        _weight_spec((None, D_MODEL, D_MODEL), lambda i: (l, 0, 0)),
        pl.BlockSpec((None, 3, CONV_WIDTH), lambda i: (l, 0, 0)),
        pl.BlockSpec((None, 1, D_MODEL), lambda i: (l, 0, 0)),
        pl.BlockSpec((None, 1, D_MODEL), lambda i: (l, 0, 0)),
        pl.BlockSpec(memory_space=pl.ANY),
        pl.BlockSpec(memory_space=pl.ANY),
    ]
    args = [xp, mod, w_in, w_out, conv_w, ln_g, ln_b, *kv_bufs]
    return pl.pallas_call(
        _ctx_mixer_kernel,
        grid=(batch // sb,),
        in_specs=in_specs,
        out_specs=[pl.BlockSpec((sb, seq_len, D_MODEL), lambda i: (i, 0, 0)), kv_spec, kv_spec],
        out_shape=[jax.ShapeDtypeStruct(xp.shape, F32), kv_shape, kv_shape],
        input_output_aliases={len(args) - 2: 1, len(args) - 1: 2},
        scratch_shapes=[
            pltpu.VMEM((ATT_WIDTH, m), BF16),
            pltpu.VMEM((N_HEADS, m, HEAD_DIM), BF16),
            pltpu.VMEM((ATT_WIDTH, m), BF16),
            pltpu.VMEM((ATT_WIDTH, m), F32),
            pltpu.VMEM((m, CONV_WIDTH), BF16),
            pltpu.VMEM((N_HEADS * sb, seq_len, seq_len), F32),
            pltpu.VMEM((N_HEADS * sb, seq_len, seq_len), BF16),
        ],
        compiler_params=pltpu.CompilerParams(
            dimension_semantics=("arbitrary",), vmem_limit_bytes=VMEM_LIMIT_BYTES),
        name="ctx_mixer",
    )(*args)


def _window_start(r, rows):
    return min(max(r - WIN_ROWS // 2, 0), rows - WIN_ROWS)


def _chunk_key_rows(r0, chunk_rows, rows):
    lo = _window_start(r0, rows) // 2 * 2
    hi = -(-(_window_start(r0 + chunk_rows - 1, rows) + WIN_ROWS) // 2) * 2
    return lo, hi


def _lat_mixer_kernel(x_ref, mod_ref, win_ref, wout_ref, convw_ref, g_ref, b_ref,
                      ckT_ref, cvT_ref, tbl_ref, o_ref,
                      qT_scr, k_scr, vT_scr, attT_scr, conv_scr, sT_scr, pT_scr):
    b = pl.program_id(0)
    hd = pl.program_id(1)
    seq_len = x_ref.shape[0]
    rows = seq_len // GRID_W
    mod = mod_ref[pl.ds(1 + b, 1), :]

    @pl.when(hd == 0)
    def _():
        _project(x_ref[...], mod, win_ref, convw_ref, qT_scr, k_scr, vT_scr, conv_scr, seq_len)

    chan = _head_rows(hd)
    ckT = ckT_ref[...]
    ck = jnp.concatenate([ckT, jnp.zeros_like(ckT)], axis=0).T.astype(BF16)
    cvT = cvT_ref[...].astype(BF16)
    low_half = lax.broadcasted_iota(jnp.int32, (GRID_W, 2 * GRID_W), 1) < GRID_W

    def bias_block(r_pair, rk):
        halves = []
        for r in (r_pair, r_pair + 1):
            rs = _window_start(r, rows)
            if rs <= rk < rs + WIN_ROWS:
                dr = rk - r + WIN_ROWS - 1
                halves.append(tbl_ref[dr * GRID_W:(dr + 1) * GRID_W, :])
            else:
                halves.append(jnp.full((GRID_W, 2 * GRID_W), NEG_INF, F32))
        return jnp.where(low_half, halves[0], halves[1])

    chunk_rows = LAT_Q_CHUNK // GRID_W
    past = ck.shape[0]
    max_loc = sT_scr.shape[1] - past
    chunks = []
    for qc in range(seq_len // LAT_Q_CHUNK):
        r0 = qc * chunk_rows
        k_lo, k_hi = _chunk_key_rows(r0, chunk_rows, rows)
        chunks.append((qc, r0, k_lo, k_hi, slice(qc * LAT_Q_CHUNK, (qc + 1) * LAT_Q_CHUNK)))
    for qc, r0, k_lo, k_hi, qcols in chunks:
        qT = qT_scr[chan, qcols]
        bias = jnp.concatenate(
            [jnp.concatenate([bias_block(r0 + j, rk) for j in range(0, chunk_rows, 2)], axis=1)
             for rk in range(k_lo, k_hi)], axis=0)
        sT_scr[qc, 0:(k_hi - k_lo) * GRID_W] = _dot(k_scr[hd, k_lo * GRID_W:k_hi * GRID_W, :], qT) + bias
        sT_scr[qc, max_loc:] = _dot(ck, jnp.concatenate([qT, jnp.zeros_like(qT)], axis=0))
    for qc, r0, k_lo, k_hi, qcols in chunks:
        n_loc = (k_hi - k_lo) * GRID_W
        pT_scr[qc, 0:n_loc], pT_scr[qc, max_loc:] = _softmax_keys([sT_scr[qc, 0:n_loc], sT_scr[qc, max_loc:]])
    for qc, r0, k_lo, k_hi, qcols in chunks:
        n_loc = (k_hi - k_lo) * GRID_W
        attT_scr[chan, qcols] = (_dot(vT_scr[chan, k_lo * GRID_W:k_hi * GRID_W], pT_scr[qc, 0:n_loc])
                                 + _dot(cvT, pT_scr[qc, max_loc:]))

    @pl.when(hd == N_HEADS - 1)
    def _():
        _merge_and_norm(x_ref[...], mod, attT_scr, conv_scr, wout_ref, g_ref, b_ref, o_ref)


def _lat_mixer(l, xs, mod, w_in, w_out, conv_w, ln_g, ln_b, cache_kT, cache_vT, tbl):
    batch, seq_len, _ = xs.shape
    past = cache_kT.shape[-1]
    rows, chunk_rows, n_chunks = seq_len // GRID_W, LAT_Q_CHUNK // GRID_W, seq_len // LAT_Q_CHUNK
    max_loc = GRID_W * max(hi - lo for lo, hi in
                           (_chunk_key_rows(qc * chunk_rows, chunk_rows, rows) for qc in range(n_chunks)))
    cache_spec = pl.BlockSpec((None, None, None, HEAD_DIM, past), lambda b, h: (b, l, h, 0, 0))
    return pl.pallas_call(
        _lat_mixer_kernel,
        grid=(batch, N_HEADS),
        in_specs=[
            pl.BlockSpec((None, seq_len, D_MODEL), lambda b, h: (b, 0, 0)),
            pl.BlockSpec((None, 8, 6 * D_MODEL), lambda b, h: (l, 0, 0)),
            _weight_spec((None, D_MODEL, 3 * ATT_WIDTH + 3 * CONV_WIDTH), lambda b, h: (l, 0, 0)),
            _weight_spec((None, D_MODEL, D_MODEL), lambda b, h: (l, 0, 0)),
            pl.BlockSpec((None, 3, CONV_WIDTH), lambda b, h: (l, 0, 0)),
            pl.BlockSpec((None, 1, D_MODEL), lambda b, h: (l, 0, 0)),
            pl.BlockSpec((None, 1, D_MODEL), lambda b, h: (l, 0, 0)),
            cache_spec, cache_spec,
            pl.BlockSpec((None, None) + tbl.shape[2:], lambda b, h: (l, h, 0, 0)),
        ],
        out_specs=pl.BlockSpec((None, seq_len, D_MODEL), lambda b, h: (b, 0, 0)),
        out_shape=jax.ShapeDtypeStruct(xs.shape, F32),
        scratch_shapes=[
            pltpu.VMEM((ATT_WIDTH, seq_len), BF16),
            pltpu.VMEM((N_HEADS, seq_len, HEAD_DIM), BF16),
            pltpu.VMEM((ATT_WIDTH, seq_len), BF16),
            pltpu.VMEM((ATT_WIDTH, seq_len), F32),
            pltpu.VMEM((seq_len, CONV_WIDTH), BF16),
            pltpu.VMEM((n_chunks, max_loc + past, LAT_Q_CHUNK), F32),
            pltpu.VMEM((n_chunks, max_loc + past, LAT_Q_CHUNK), BF16),
        ],
        compiler_params=pltpu.CompilerParams(
            dimension_semantics=("arbitrary", "arbitrary"), vmem_limit_bytes=VMEM_LIMIT_BYTES),
        name="lat_mixer",
    )(xs, mod, w_in, w_out, conv_w, ln_g, ln_b, cache_kT, cache_vT, tbl)


def _bias_table(rpb):
    depth, heads, n_dr, n_dc = rpb.shape
    lane0 = GRID_W - WIN_COLS
    rpb_pad = jnp.pad(rpb[..., ::-1], ((0, 0), (0, 0), (0, 16 - n_dr), (lane0, LANES - lane0 - n_dc)))
    rpb_pad = rpb_pad.reshape(depth * heads, 16, LANES)

    def body(r_ref, o_ref):
        cp = lax.broadcasted_iota(jnp.int32, (GRID_W, LANES), 0)
        lane = lax.broadcasted_iota(jnp.int32, (GRID_W, LANES), 1)
        low_half = lane < GRID_W
        c = jnp.where(low_half, lane, lane - GRID_W)
        col_start = jnp.clip(c - WIN_COLS // 2, 0, GRID_W - WIN_COLS)
        valid = (cp >= col_start) & (cp < col_start + WIN_COLS)
        for dr in range(n_dr):
            row = jnp.broadcast_to(r_ref[dr:dr + 1, :], (GRID_W, LANES))
            lo = pltpu.roll(row, LANES - GRID_W + 1, 1, stride=1, stride_axis=0)
            hi = pltpu.roll(row, 1, 1, stride=1, stride_axis=0)
            o_ref[dr * GRID_W:(dr + 1) * GRID_W, :] = jnp.where(valid, jnp.where(low_half, lo, hi), NEG_INF)

    tbl = pl.pallas_call(
        body,
        grid=(depth * heads,),
        in_specs=[pl.BlockSpec((None, 16, LANES), lambda i: (i, 0, 0))],
        out_specs=pl.BlockSpec((None, n_dr * GRID_W, LANES), lambda i: (i, 0, 0)),
        out_shape=jax.ShapeDtypeStruct((depth * heads, n_dr * GRID_W, LANES), F32),
        compiler_params=pltpu.CompilerParams(dimension_semantics=("arbitrary",)),
        name="bias_table",
    )(rpb_pad)
    return tbl.reshape(depth, heads, n_dr * GRID_W, LANES)


def _route(logits):
    lane = lax.broadcasted_iota(jnp.int32, logits.shape, 1)
    lane_f = lane.astype(F32)
    big = jnp.float32(LANES)

    def first_lane(cond):
        return jnp.min(jnp.where(cond, lane_f, big), axis=-1, keepdims=True)

    gmask = (lane >= N_EXPERTS) & (lane < N_EXPERTS + N_GROUPS)
    gl = jnp.where(gmask, logits, NEG_INF)
    gexp = jnp.exp(gl - jnp.max(gl, axis=-1, keepdims=True))
    gprob = gexp / jnp.sum(gexp, axis=-1, keepdims=True)
    g_p = jnp.max(gprob, axis=-1, keepdims=True)
    g_idx = first_lane(gmask & (gprob == g_p)) - N_EXPERTS

    lane_group = jnp.floor(lane_f * (1.0 / EXPERTS_PER_GROUP))
    emask = (lane < N_EXPERTS) & (lane_group == g_idx)
    el = jnp.where(emask, logits, NEG_INF)
    eexp = jnp.exp(el - jnp.max(el, axis=-1, keepdims=True))
    eprob = eexp / jnp.sum(eexp, axis=-1, keepdims=True)
    p1 = jnp.max(eprob, axis=-1, keepdims=True)
    i1 = first_lane(emask & (eprob == p1))
    rest = emask & (lane_f != i1)
    p2 = jnp.max(jnp.where(rest, eprob, -1.0), axis=-1, keepdims=True)
    i2 = first_lane(rest & (eprob == p2))
    denom = p1 + p2
    gate = (jnp.where(lane_f == i1, g_p * p1 / denom, 0.0)
            + jnp.where(lane_f == i2, g_p * p2 / denom, 0.0))
    return gate, g_idx


def _split3(v):
    hi = v.astype(BF16).astype(F32)
    mid = (v - hi).astype(BF16).astype(F32)
    return hi, mid, v - hi - mid


def _moe_route_tile(x_ref, mod_ref, wr_ref, br_ref, h_scr, gate4_scr, meta_scr, metaT_scr, acc_scr, ovf_smem):
    tm = x_ref.shape[0]
    lane = lax.broadcasted_iota(jnp.int32, (tm, LANES), 1)
    sf = mod_ref[:, 3 * D_MODEL:4 * D_MODEL]
    cf = mod_ref[:, 4 * D_MODEL:5 * D_MODEL]
    h = x_ref[...] * (1.0 + cf) + sf
    h_hi = h.astype(BF16)
    h_lo = (h - h_hi.astype(F32)).astype(BF16)
    wr = wr_ref[...]
    wr_hi = wr.astype(BF16)
    wr_lo = (wr - wr_hi.astype(F32)).astype(BF16)
    logits = (_dot(h_hi, wr_hi) + _dot(h_lo, wr_hi) + _dot(h_hi, wr_lo)) + br_ref[...]
    gate, g_idx = _route(logits)

    gate4 = jnp.zeros_like(gate)
    for gp in range(N_GROUPS):
        shifted = gate if gp == 0 else pltpu.roll(gate, LANES - gp * EXPERTS_PER_GROUP, 1)
        gate4 = jnp.where((g_idx == gp) & (lane < EXPERTS_PER_GROUP), shifted, gate4)
    gate4_scr[...] = gate4
    hi, mid, lo = _split3(gate4)
    ext = hi + pltpu.roll(mid, EXPERTS_PER_GROUP, 1) + pltpu.roll(lo, 2 * EXPERTS_PER_GROUP, 1)
    h_scr[:, 0:D_MODEL] = h_hi
    h_scr[:, D_MODEL:] = ext.astype(BF16)

    onehot = jnp.where((lane.astype(F32) == g_idx) & (lane < N_GROUPS), 1.0, 0.0)
    ri = lax.broadcasted_iota(jnp.int32, (MOE_BLOCK, MOE_BLOCK), 0)
    ci = lax.broadcasted_iota(jnp.int32, (MOE_BLOCK, MOE_BLOCK), 1)
    tri = jnp.where(ci < ri, 1.0, 0.0).astype(BF16)
    ranks = []
    most = jnp.zeros((1, LANES), F32)
    for b in range(tm // MOE_BLOCK):
        oh_b = onehot[b * MOE_BLOCK:(b + 1) * MOE_BLOCK]
        ranks.append(_dot(tri, oh_b.astype(BF16)))
        most = jnp.maximum(most, jnp.sum(oh_b, axis=0, keepdims=True))
    r_own = jnp.sum(onehot * jnp.concatenate(ranks, axis=0), axis=-1, keepdims=True)
    meta = jnp.where(lane == 0, g_idx, jnp.where(lane == 1, r_own, 0.0))
    meta_scr[...] = meta
    metaT_scr[...] = meta.T[0:8, :]
    lane1 = lax.broadcasted_iota(jnp.int32, (1, LANES), 1)
    for gp in range(N_GROUPS):
        ovf_smem[gp] = (jnp.max(jnp.where(lane1 == gp, most, 0.0)) > MOE_SLOT).astype(jnp.int32)
    acc_scr[...] = jnp.zeros_like(acc_scr)


def _moe_kernel(x_ref, mod_ref, wr_ref, br_ref, win_ref, wout_ref, g_ref, b_ref, o_ref,
                h_scr, gate4_scr, meta_scr, metaT_scr, acc_scr, act_scr, ovf_smem):
    g = pl.program_id(1)
    g_f = g.astype(F32)
    tm = x_ref.shape[0]
    n_blocks = tm // MOE_BLOCK
    n_rows = n_blocks * MOE_SLOT

    @pl.when(g == 0)
    def _():
        _moe_route_tile(x_ref, mod_ref, wr_ref, br_ref, h_scr, gate4_scr, meta_scr, metaT_scr, acc_scr,
                        ovf_smem)

    def experts(xb, gates, rows):
        for e in range(EXPERTS_PER_GROUP):
            hid = _dot(xb, win_ref[e].astype(BF16))
            act = _silu(hid[:, :D_EXPERT]) * hid[:, D_EXPERT:] * gates[:, e:e + 1]
            act_scr[rows, e * D_EXPERT:(e + 1) * D_EXPERT] = act.astype(BF16)
        return _dot(act_scr[rows, :], wout_ref[...].astype(BF16))

    @pl.when(ovf_smem[g] == 0)
    def _():
        slot_row = lax.broadcasted_iota(jnp.int32, (MOE_SLOT, MOE_BLOCK), 0).astype(F32)
        gathered = []
        for b in range(n_blocks):
            tok = slice(b * MOE_BLOCK, (b + 1) * MOE_BLOCK)
            pick = (metaT_scr[0:1, tok] == g_f) & (metaT_scr[1:2, tok] == slot_row)
            gathered.append(_dot(jnp.where(pick, 1.0, 0.0).astype(BF16), h_scr[tok, :]))
        xg = jnp.concatenate(gathered, axis=0)
        ge = xg[:, D_MODEL:]
        gates = (ge + pltpu.roll(ge, LANES - EXPERTS_PER_GROUP, 1)) + pltpu.roll(ge, LANES - 2 * EXPERTS_PER_GROUP, 1)
        y = experts(xg[:, :D_MODEL].astype(BF16), gates, slice(0, n_rows)).astype(BF16)
        slot_col = lax.broadcasted_iota(jnp.int32, (MOE_BLOCK, SCATTER_K), 1).astype(F32)
        for b in range(n_blocks):
            tok = slice(b * MOE_BLOCK, (b + 1) * MOE_BLOCK)
            first = min(b * MOE_SLOT, n_rows - SCATTER_K)
            place = (meta_scr[tok, 0:1] == g_f) & (meta_scr[tok, 1:2] + (b * MOE_SLOT - first) == slot_col)
            acc_scr[tok, :] += _dot(jnp.where(place, 1.0, 0.0).astype(BF16), y[first:first + SCATTER_K, :])

    @pl.when(ovf_smem[g] != 0)
    def _():
        gates = jnp.where(meta_scr[:, 0:1] == g_f, gate4_scr[...], 0.0)
        acc_scr[...] += experts(h_scr[:, 0:D_MODEL], gates, slice(0, tm))

    @pl.when(g == N_GROUPS - 1)
    def _():
        gf = mod_ref[:, 5 * D_MODEL:6 * D_MODEL]
        o_ref[...] = _layer_norm(ALPHA * x_ref[...] + gf * acc_scr[...], g_ref[...], b_ref[...])


def _moe(l, x, mod_rows, w_router, b_router, w_exp_in, w_exp_out, ln_g, ln_b):
    n = x.shape[0]
    tm = MOE_TOKENS
    assert tm % MOE_BLOCK == 0 and MOE_SLOT % 16 == 0 and SCATTER_K <= tm // MOE_BLOCK * MOE_SLOT
    return pl.pallas_call(
        _moe_kernel,
        grid=(n // tm, N_GROUPS),
        in_specs=[
            pl.BlockSpec((tm, D_MODEL), lambda i, g: (i, 0)),
            pl.BlockSpec((None, None, 1, 6 * D_MODEL), lambda i, g: (l, i, 0, 0)),
            pl.BlockSpec((None, D_MODEL, LANES), lambda i, g: (l, 0, 0)),
            pl.BlockSpec((None, 1, LANES), lambda i, g: (l, 0, 0)),
            pl.BlockSpec((None, EXPERTS_PER_GROUP, D_MODEL, 2 * D_EXPERT), lambda i, g: (l, g, 0, 0)),
            pl.BlockSpec((None, None, EXPERTS_PER_GROUP * D_EXPERT, D_MODEL), lambda i, g: (l, g, 0, 0)),
            pl.BlockSpec((None, 1, D_MODEL), lambda i, g: (l, 0, 0)),
            pl.BlockSpec((None, 1, D_MODEL), lambda i, g: (l, 0, 0)),
        ],
        out_specs=pl.BlockSpec((tm, D_MODEL), lambda i, g: (i, 0)),
        out_shape=jax.ShapeDtypeStruct(x.shape, F32),
        scratch_shapes=[
            pltpu.VMEM((tm, D_MODEL + GATE_COLS), BF16),
            pltpu.VMEM((tm, LANES), F32),
            pltpu.VMEM((tm, LANES), F32),
            pltpu.VMEM((8, tm), F32),
            pltpu.VMEM((tm, D_MODEL), F32),
            pltpu.VMEM((tm, EXPERTS_PER_GROUP * D_EXPERT), BF16),
            pltpu.SMEM((N_GROUPS,), jnp.int32),
        ],
        compiler_params=pltpu.CompilerParams(
            dimension_semantics=("arbitrary", "arbitrary"), vmem_limit_bytes=VMEM_LIMIT_BYTES),
        name="hier_moe",
    )(x, mod_rows, w_router, b_router, w_exp_in, w_exp_out, ln_g, ln_b)


def kernel(x_prompt, x_sample, cache_k, cache_v, c, c_ctx, w_ada, b_ada, w_in, conv_w, rpb, w_out,
           ln1_g, ln1_b, w_router_group, b_router_group, w_router_expert, b_router_expert,
           w_expert_in, w_expert_out, ln2_g, ln2_b):
    batch, seq_len, _ = x_prompt.shape
    dec_batch, dec_seq, _ = x_sample.shape
    assert dec_batch + 1 <= 8 and dec_seq == MOE_TOKENS and (batch * seq_len) % MOE_TOKENS == 0

    cond = jnp.concatenate([c_ctx[None, :], c, jnp.zeros((8 - 1 - dec_batch, D_MODEL), F32)], axis=0)
    mod = _modulation(cond, w_ada, b_ada)
    n_ctx_tiles = batch * seq_len // MOE_TOKENS
    mod_rows_ctx = jnp.broadcast_to(mod[:, 0:1, None, :], (DEPTH, n_ctx_tiles, 1, 6 * D_MODEL))
    mod_rows_lat = mod[:, 1:1 + dec_batch, None, :]

    w_in_bf = w_in.astype(BF16)
    w_out_bf = w_out.astype(BF16)
    pad = jnp.zeros((DEPTH, D_MODEL, LANES - N_EXPERTS - N_GROUPS), F32)
    w_router = jnp.concatenate([w_router_expert, w_router_group, pad], axis=-1)
    b_router = jnp.concatenate([b_router_expert, b_router_group, pad[:, 0, :]], axis=-1)[:, None, :]
    tbl = _bias_table(rpb)
    cache_kT = jnp.swapaxes(cache_k, -1, -2)
    cache_vT = jnp.swapaxes(cache_v, -1, -2)
    ln1_g3, ln1_b3 = ln1_g[:, None, :], ln1_b[:, None, :]
    ln2_g3, ln2_b3 = ln2_g[:, None, :], ln2_b[:, None, :]

    w_exp_out = w_expert_out.reshape(DEPTH, N_GROUPS, EXPERTS_PER_GROUP * D_EXPERT, D_MODEL)

    xp, xs = x_prompt, x_sample
    kv_bufs = [jnp.zeros((batch, DEPTH, N_HEADS, HEAD_DIM, seq_len), F32)] * 2
    for l in range(DEPTH):
        moe = functools.partial(_moe, l, w_router=w_router, b_router=b_router, w_exp_in=w_expert_in,
                                w_exp_out=w_exp_out, ln_g=ln2_g3, ln_b=ln2_b3)
        xp, *kv_bufs = _ctx_mixer(l, xp, mod, w_in_bf, w_out_bf, conv_w, ln1_g3, ln1_b3, kv_bufs)
        xp = moe(xp.reshape(-1, D_MODEL), mod_rows_ctx).reshape(xp.shape)
        xs = _lat_mixer(l, xs, mod, w_in_bf, w_out_bf, conv_w, ln1_g3, ln1_b3, cache_kT, cache_vT, tbl)
        xs = moe(xs.reshape(-1, D_MODEL), mod_rows_lat).reshape(xs.shape)
    new_kT, new_vT = kv_bufs
    return (xp, xs, jnp.swapaxes(new_kT, -1, -2), jnp.swapaxes(new_vT, -1, -2))
```

```python
import functools

import jax
import jax.numpy as jnp
from jax import lax
from jax.experimental import pallas as pl
from jax.experimental.pallas import tpu as pltpu

D_MODEL = 1024
DEPTH = 4
GRID_W = 64
ATT_WIDTH = D_MODEL // 2
CONV_WIDTH = D_MODEL - ATT_WIDTH
HEAD_DIM = 64
N_HEADS = ATT_WIDTH // HEAD_DIM
WIN_ROWS = 8
WIN_COLS = 16
N_GROUPS = 4
EXPERTS_PER_GROUP = 4
N_EXPERTS = N_GROUPS * EXPERTS_PER_GROUP
D_EXPERT = D_MODEL // 4
ALPHA = (2 * DEPTH) ** 0.25
LN_EPS = 1e-5
NEG_INF = -1e30
QK_SCALE = HEAD_DIM ** -0.5

F32 = jnp.float32
BF16 = jnp.bfloat16

LANES = 128
VMEM_LIMIT_BYTES = 58 * 1024 * 1024

CTX_SEQ_PER_STEP = 2
LAT_Q_CHUNK = 256
MOE_TOKENS = 2048
MOE_BLOCK = 512
MOE_SLOT = 160
SCATTER_K = 256
GATE_COLS = 128
META_GROUP, META_RANK = 4, 5
MOD_COLS = 1024


def _dot(a, b):
    return jnp.dot(a, b, preferred_element_type=F32)


def _dot_nt(a, b):
    return lax.dot_general(a, b, (((1,), (1,)), ((), ())), preferred_element_type=F32)


def _silu(x):
    return x * (1.0 / (1.0 + jnp.exp(-x)))


def _layer_norm(r, g, b):
    mu = jnp.mean(r, axis=-1, keepdims=True)
    d = r - mu
    var = jnp.mean(d * d, axis=-1, keepdims=True)
    return d * lax.rsqrt(var + LN_EPS) * g + b


def _mod_kernel(cond_ref, w_ref, b_ref, o_ref):
    s = _silu(cond_ref[...]).astype(BF16)
    o_ref[...] = _dot(s, w_ref[...].astype(BF16)) + b_ref[...]


def _modulation(cond, w_ada, b_ada):
    n_out = w_ada.shape[-1]
    return pl.pallas_call(
        _mod_kernel,
        grid=(DEPTH, n_out // MOD_COLS),
        in_specs=[
            pl.BlockSpec((8, D_MODEL), lambda l, j: (0, 0)),
            pl.BlockSpec((None, D_MODEL, MOD_COLS), lambda l, j: (l, 0, j)),
            pl.BlockSpec((None, 1, MOD_COLS), lambda l, j: (l, 0, j)),
        ],
        out_specs=pl.BlockSpec((None, 8, MOD_COLS), lambda l, j: (l, 0, j)),
        out_shape=jax.ShapeDtypeStruct((DEPTH, 8, n_out), F32),
        compiler_params=pltpu.CompilerParams(
            dimension_semantics=("arbitrary", "arbitrary"),
            vmem_limit_bytes=VMEM_LIMIT_BYTES),
        name="adaln_modulation",
    )(cond, w_ada, b_ada.reshape(DEPTH, 1, n_out))


def _project(x, mod, win_ref, convw_ref, qT_scr, k_scr, vT_scr, conv_scr, seq_len, kT_out=None, vT_out=None):
    m = x.shape[0]
    sa = mod[:, 0:D_MODEL]
    ca = mod[:, D_MODEL:2 * D_MODEL]
    h = (x * (1.0 + ca) + sa).astype(BF16)

    qT_scr[...] = (_dot(h, win_ref[:, 0:ATT_WIDTH]) * QK_SCALE).T.astype(BF16)
    zk = _dot(h, win_ref[:, ATT_WIDTH:2 * ATT_WIDTH])
    for hd in range(N_HEADS):
        k_scr[hd] = zk[:, hd * HEAD_DIM:(hd + 1) * HEAD_DIM].astype(BF16)
    zvT = _dot(h, win_ref[:, 2 * ATT_WIDTH:3 * ATT_WIDTH]).T
    vT_scr[...] = zvT.astype(BF16)
    if kT_out is not None:
        zkT = zk.T
        for out, zT in ((kT_out, zkT), (vT_out, zvT)):
            for s in range(m // seq_len):
                for hd in range(N_HEADS):
                    out[s, hd] = zT[hd * HEAD_DIM:(hd + 1) * HEAD_DIM, s * seq_len:(s + 1) * seq_len]

    c0 = 3 * ATT_WIDTH
    bg = _dot(h, win_ref[:, c0:c0 + CONV_WIDTH])
    cg = _dot(h, win_ref[:, c0 + CONV_WIDTH:c0 + 2 * CONV_WIDTH])
    u = _dot(h, win_ref[:, c0 + 2 * CONV_WIDTH:c0 + 3 * CONV_WIDTH])
    y = cg * u
    t = lax.broadcasted_iota(jnp.int32, (m, 1), 0) % seq_len
    y_prev = jnp.where(t == 0, 0.0, pltpu.roll(y, 1, 0))
    y_next = jnp.where(t == seq_len - 1, 0.0, pltpu.roll(y, m - 1, 0))
    cw = convw_ref[...]
    conv = cw[0:1, :] * y_prev + cw[1:2, :] * y + cw[2:3, :] * y_next
    conv_scr[...] = (bg * conv).astype(BF16)


def _merge_and_norm(x, mod, attT_scr, conv_scr, wout_ref, g_ref, b_ref, o_ref):
    att = attT_scr[...].T.astype(BF16)
    mix = _dot(att, wout_ref[0:ATT_WIDTH, :]) + _dot(conv_scr[...], wout_ref[ATT_WIDTH:, :])
    ga = mod[:, 2 * D_MODEL:3 * D_MODEL]
    o_ref[...] = _layer_norm(ALPHA * x + ga * mix, g_ref[...], b_ref[...]).reshape(o_ref.shape)


def _softmax_keys(parts):
    mx = functools.reduce(jnp.maximum, [jnp.max(p, axis=0, keepdims=True) for p in parts])
    es = [jnp.exp(p - mx) for p in parts]
    inv = 1.0 / functools.reduce(jnp.add, [jnp.sum(e, axis=0, keepdims=True) for e in es])
    return [(e * inv).astype(BF16) for e in es]


def _head_rows(hd):
    return pl.ds(pl.multiple_of(hd * HEAD_DIM, HEAD_DIM), HEAD_DIM)


def _weight_spec(shape, index_map):
    return pl.BlockSpec(shape, index_map, pipeline_mode=pl.Buffered(1))


def _ctx_mixer_kernel(x_ref, mod_ref, win_ref, wout_ref, convw_ref, g_ref, b_ref, kT_in, vT_in,
                      o_ref, kT_ref, vT_ref, qT_scr, k_scr, vT_scr, attT_scr, conv_scr, sT_scr, pT_scr):
    del kT_in, vT_in
    sb, seq_len, _ = x_ref.shape
    x = x_ref[...].reshape(sb * seq_len, D_MODEL)
    mod = mod_ref[0:1, :]
    _project(x, mod, win_ref, convw_ref, qT_scr, k_scr, vT_scr, conv_scr, seq_len, kT_ref, vT_ref)

    pairs = [(hd, s) for hd in range(N_HEADS) for s in range(sb)]
    chan = lambda hd: slice(hd * HEAD_DIM, (hd + 1) * HEAD_DIM)
    tok = lambda s: slice(s * seq_len, (s + 1) * seq_len)
    for i, (hd, s) in enumerate(pairs):
        sT_scr[i] = _dot(k_scr[hd, tok(s), :], qT_scr[chan(hd), tok(s)])
    for i in range(len(pairs)):
        (pT_scr[i],) = _softmax_keys([sT_scr[i]])
    for i, (hd, s) in enumerate(pairs):
        attT_scr[chan(hd), tok(s)] = _dot(vT_scr[chan(hd), tok(s)], pT_scr[i])

    _merge_and_norm(x, mod, attT_scr, conv_scr, wout_ref, g_ref, b_ref, o_ref)


def _ctx_mixer(l, xp, mod, w_in, w_out, conv_w, ln_g, ln_b, kv_bufs):
    batch, seq_len, _ = xp.shape
    sb = CTX_SEQ_PER_STEP
    m = sb * seq_len
    kv_shape = jax.ShapeDtypeStruct((batch, DEPTH, N_HEADS, HEAD_DIM, seq_len), F32)
    kv_spec = pl.BlockSpec((sb, None, N_HEADS, HEAD_DIM, seq_len), lambda i: (i, l, 0, 0, 0))
    in_specs = [
        pl.BlockSpec((sb, seq_len, D_MODEL), lambda i: (i, 0, 0)),
        pl.BlockSpec((None, 8, 6 * D_MODEL), lambda i: (l, 0, 0)),
        _weight_spec((None, D_MODEL, 3 * ATT_WIDTH + 3 * CONV_WIDTH), lambda i: (l, 0, 0)),
        _weight_spec((None, D_MODEL, D_MODEL), lambda i: (l, 0, 0)),
        pl.BlockSpec((None, 3, CONV_WIDTH), lambda i: (l, 0, 0)),
        pl.BlockSpec((None, 1, D_MODEL), lambda i: (l, 0, 0)),
        pl.BlockSpec((None, 1, D_MODEL), lambda i: (l, 0, 0)),
        pl.BlockSpec(memory_space=pl.ANY),
        pl.BlockSpec(memory_space=pl.ANY),
    ]
    args = [xp, mod, w_in, w_out, conv_w, ln_g, ln_b, *kv_bufs]
    return pl.pallas_call(
        _ctx_mixer_kernel,
        grid=(batch // sb,),
        in_specs=in_specs,
        out_specs=[pl.BlockSpec((sb, seq_len, D_MODEL), lambda i: (i, 0, 0)), kv_spec, kv_spec],
        out_shape=[jax.ShapeDtypeStruct(xp.shape, F32), kv_shape, kv_shape],
        input_output_aliases={len(args) - 2: 1, len(args) - 1: 2},
        scratch_shapes=[
            pltpu.VMEM((ATT_WIDTH, m), BF16),
            pltpu.VMEM((N_HEADS, m, HEAD_DIM), BF16),
            pltpu.VMEM((ATT_WIDTH, m), BF16),
            pltpu.VMEM((ATT_WIDTH, m), F32),
            pltpu.VMEM((m, CONV_WIDTH), BF16),
            pltpu.VMEM((N_HEADS * sb, seq_len, seq_len), F32),
            pltpu.VMEM((N_HEADS * sb, seq_len, seq_len), BF16),
        ],
        compiler_params=pltpu.CompilerParams(
            dimension_semantics=("arbitrary",), vmem_limit_bytes=VMEM_LIMIT_BYTES),
        name="ctx_mixer",
    )(*args)


def _window_start(r, rows):
    return min(max(r - WIN_ROWS // 2, 0), rows - WIN_ROWS)


def _chunk_key_rows(r0, chunk_rows, rows):
    lo = _window_start(r0, rows) // 2 * 2
    hi = -(-(_window_start(r0 + chunk_rows - 1, rows) + WIN_ROWS) // 2) * 2
    return lo, hi


def _lat_mixer_kernel(x_ref, mod_ref, win_ref, wout_ref, convw_ref, g_ref, b_ref,
                      ckT_ref, cvT_ref, tbl_ref, o_ref,
                      qT_scr, k_scr, vT_scr, attT_scr, conv_scr, sT_scr, pT_scr):
    b = pl.program_id(0)
    hd = pl.program_id(1)
    seq_len = x_ref.shape[0]
    rows = seq_len // GRID_W
    mod = mod_ref[pl.ds(1 + b, 1), :]

    @pl.when(hd == 0)
    def _():
        _project(x_ref[...], mod, win_ref, convw_ref, qT_scr, k_scr, vT_scr, conv_scr, seq_len)

    chan = _head_rows(hd)
    ckT = ckT_ref[...]
    ck = jnp.concatenate([ckT, jnp.zeros_like(ckT)], axis=0).T.astype(BF16)
    cvT = cvT_ref[...].astype(BF16)
    low_half = lax.broadcasted_iota(jnp.int32, (GRID_W, 2 * GRID_W), 1) < GRID_W

    def bias_block(r_pair, rk):
        halves = []
        for r in (r_pair, r_pair + 1):
            rs = _window_start(r, rows)
            if rs <= rk < rs + WIN_ROWS:
                dr = rk - r + WIN_ROWS - 1
                halves.append(tbl_ref[dr * GRID_W:(dr + 1) * GRID_W, :])
            else:
                halves.append(jnp.full((GRID_W, 2 * GRID_W), NEG_INF, F32))
        return jnp.where(low_half, halves[0], halves[1])

    chunk_rows = LAT_Q_CHUNK // GRID_W
    past = ck.shape[0]
    max_loc = sT_scr.shape[1] - past
    chunks = []
    for qc in range(seq_len // LAT_Q_CHUNK):
        r0 = qc * chunk_rows
        k_lo, k_hi = _chunk_key_rows(r0, chunk_rows, rows)
        chunks.append((qc, r0, k_lo, k_hi, slice(qc * LAT_Q_CHUNK, (qc + 1) * LAT_Q_CHUNK)))
    for qc, r0, k_lo, k_hi, qcols in chunks:
        qT = qT_scr[chan, qcols]
        bias = jnp.concatenate(
            [jnp.concatenate([bias_block(r0 + j, rk) for j in range(0, chunk_rows, 2)], axis=1)
             for rk in range(k_lo, k_hi)], axis=0)
        sT_scr[qc, 0:(k_hi - k_lo) * GRID_W] = _dot(k_scr[hd, k_lo * GRID_W:k_hi * GRID_W, :], qT) + bias
        sT_scr[qc, max_loc:] = _dot(ck, jnp.concatenate([qT, jnp.zeros_like(qT)], axis=0))
    for qc, r0, k_lo, k_hi, qcols in chunks:
        n_loc = (k_hi - k_lo) * GRID_W
        pT_scr[qc, 0:n_loc], pT_scr[qc, max_loc:] = _softmax_keys([sT_scr[qc, 0:n_loc], sT_scr[qc, max_loc:]])
    for qc, r0, k_lo, k_hi, qcols in chunks:
        n_loc = (k_hi - k_lo) * GRID_W
        attT_scr[chan, qcols] = (_dot(vT_scr[chan, k_lo * GRID_W:k_hi * GRID_W], pT_scr[qc, 0:n_loc])
                                 + _dot(cvT, pT_scr[qc, max_loc:]))

    @pl.when(hd == N_HEADS - 1)
    def _():
        _merge_and_norm(x_ref[...], mod, attT_scr, conv_scr, wout_ref, g_ref, b_ref, o_ref)


def _lat_mixer(l, xs, mod, w_in, w_out, conv_w, ln_g, ln_b, cache_kT, cache_vT, tbl):
    batch, seq_len, _ = xs.shape
    past = cache_kT.shape[-1]
    rows, chunk_rows, n_chunks = seq_len // GRID_W, LAT_Q_CHUNK // GRID_W, seq_len // LAT_Q_CHUNK
    max_loc = GRID_W * max(hi - lo for lo, hi in
                           (_chunk_key_rows(qc * chunk_rows, chunk_rows, rows) for qc in range(n_chunks)))
    cache_spec = pl.BlockSpec((None, None, None, HEAD_DIM, past), lambda b, h: (b, l, h, 0, 0))
    return pl.pallas_call(
        _lat_mixer_kernel,
        grid=(batch, N_HEADS),
        in_specs=[
            pl.BlockSpec((None, seq_len, D_MODEL), lambda b, h: (b, 0, 0)),
            pl.BlockSpec((None, 8, 6 * D_MODEL), lambda b, h: (l, 0, 0)),
            _weight_spec((None, D_MODEL, 3 * ATT_WIDTH + 3 * CONV_WIDTH), lambda b, h: (l, 0, 0)),
            _weight_spec((None, D_MODEL, D_MODEL), lambda b, h: (l, 0, 0)),
            pl.BlockSpec((None, 3, CONV_WIDTH), lambda b, h: (l, 0, 0)),
            pl.BlockSpec((None, 1, D_MODEL), lambda b, h: (l, 0, 0)),
            pl.BlockSpec((None, 1, D_MODEL), lambda b, h: (l, 0, 0)),
            cache_spec, cache_spec,
            pl.BlockSpec((None, None) + tbl.shape[2:], lambda b, h: (l, h, 0, 0)),
        ],
        out_specs=pl.BlockSpec((None, seq_len, D_MODEL), lambda b, h: (b, 0, 0)),
        out_shape=jax.ShapeDtypeStruct(xs.shape, F32),
        scratch_shapes=[
            pltpu.VMEM((ATT_WIDTH, seq_len), BF16),
            pltpu.VMEM((N_HEADS, seq_len, HEAD_DIM), BF16),
            pltpu.VMEM((ATT_WIDTH, seq_len), BF16),
            pltpu.VMEM((ATT_WIDTH, seq_len), F32),
            pltpu.VMEM((seq_len, CONV_WIDTH), BF16),
            pltpu.VMEM((n_chunks, max_loc + past, LAT_Q_CHUNK), F32),
            pltpu.VMEM((n_chunks, max_loc + past, LAT_Q_CHUNK), BF16),
        ],
        compiler_params=pltpu.CompilerParams(
            dimension_semantics=("arbitrary", "arbitrary"), vmem_limit_bytes=VMEM_LIMIT_BYTES),
        name="lat_mixer",
    )(xs, mod, w_in, w_out, conv_w, ln_g, ln_b, cache_kT, cache_vT, tbl)


def _bias_table(rpb):
    depth, heads, n_dr, n_dc = rpb.shape
    lane0 = GRID_W - WIN_COLS
    rpb_pad = jnp.pad(rpb[..., ::-1], ((0, 0), (0, 0), (0, 16 - n_dr), (lane0, LANES - lane0 - n_dc)))
    rpb_pad = rpb_pad.reshape(depth * heads, 16, LANES)

    def body(r_ref, o_ref):
        cp = lax.broadcasted_iota(jnp.int32, (GRID_W, LANES), 0)
        lane = lax.broadcasted_iota(jnp.int32, (GRID_W, LANES), 1)
        low_half = lane < GRID_W
        c = jnp.where(low_half, lane, lane - GRID_W)
        col_start = jnp.clip(c - WIN_COLS // 2, 0, GRID_W - WIN_COLS)
        valid = (cp >= col_start) & (cp < col_start + WIN_COLS)
        for dr in range(n_dr):
            row = jnp.broadcast_to(r_ref[dr:dr + 1, :], (GRID_W, LANES))
            lo = pltpu.roll(row, LANES - GRID_W + 1, 1, stride=1, stride_axis=0)
            hi = pltpu.roll(row, 1, 1, stride=1, stride_axis=0)
            o_ref[dr * GRID_W:(dr + 1) * GRID_W, :] = jnp.where(valid, jnp.where(low_half, lo, hi), NEG_INF)

    tbl = pl.pallas_call(
        body,
        grid=(depth * heads,),
        in_specs=[pl.BlockSpec((None, 16, LANES), lambda i: (i, 0, 0))],
        out_specs=pl.BlockSpec((None, n_dr * GRID_W, LANES), lambda i: (i, 0, 0)),
        out_shape=jax.ShapeDtypeStruct((depth * heads, n_dr * GRID_W, LANES), F32),
        compiler_params=pltpu.CompilerParams(dimension_semantics=("arbitrary",)),
        name="bias_table",
    )(rpb_pad)
    return tbl.reshape(depth, heads, n_dr * GRID_W, LANES)


def _route(logits):
    lane = lax.broadcasted_iota(jnp.int32, logits.shape, 1)
    lane_f = lane.astype(F32)
    big = jnp.float32(LANES)

    def first_lane(cond):
        return jnp.min(jnp.where(cond, lane_f, big), axis=-1, keepdims=True)

    gmask = (lane >= N_EXPERTS) & (lane < N_EXPERTS + N_GROUPS)
    gl = jnp.where(gmask, logits, NEG_INF)
    gexp = jnp.exp(gl - jnp.max(gl, axis=-1, keepdims=True))
    gprob = gexp / jnp.sum(gexp, axis=-1, keepdims=True)
    g_p = jnp.max(gprob, axis=-1, keepdims=True)
    g_idx = first_lane(gmask & (gprob == g_p)) - N_EXPERTS

    lane_group = jnp.floor(lane_f * (1.0 / EXPERTS_PER_GROUP))
    emask = (lane < N_EXPERTS) & (lane_group == g_idx)
    el = jnp.where(emask, logits, NEG_INF)
    eexp = jnp.exp(el - jnp.max(el, axis=-1, keepdims=True))
    eprob = eexp / jnp.sum(eexp, axis=-1, keepdims=True)
    p1 = jnp.max(eprob, axis=-1, keepdims=True)
    i1 = first_lane(emask & (eprob == p1))
    rest = emask & (lane_f != i1)
    p2 = jnp.max(jnp.where(rest, eprob, -1.0), axis=-1, keepdims=True)
    i2 = first_lane(rest & (eprob == p2))
    denom = p1 + p2
    gate = (jnp.where(lane_f == i1, g_p * p1 / denom, 0.0)
            + jnp.where(lane_f == i2, g_p * p2 / denom, 0.0))
    return gate, g_idx


def _split3(v):
    hi = v.astype(BF16).astype(F32)
    mid = (v - hi).astype(BF16).astype(F32)
    return hi, mid, v - hi - mid


def _moe_route_tile(x_ref, mod_ref, wr_ref, br_ref, h_scr, meta_scr, metaT_scr, o_ref, ovf_smem):
    tm = x_ref.shape[0]
    seg = tm // mod_ref.shape[0]
    lane = lax.broadcasted_iota(jnp.int32, (tm, LANES), 1)
    h = jnp.concatenate(
        [x_ref[s * seg:(s + 1) * seg, :] * (1.0 + mod_ref[s, :, 4 * D_MODEL:5 * D_MODEL])
         + mod_ref[s, :, 3 * D_MODEL:4 * D_MODEL] for s in range(tm // seg)], axis=0)
    h_hi = h.astype(BF16)
    h_lo = (h - h_hi.astype(F32)).astype(BF16)
    wr = wr_ref[...]
    wr_hi = wr.astype(BF16)
    wr_lo = (wr - wr_hi.astype(F32)).astype(BF16)
    logits = (_dot(h_hi, wr_hi) + _dot(h_lo, wr_hi) + _dot(h_hi, wr_lo)) + br_ref[...]
    gate, g_idx = _route(logits)

    gate4 = jnp.zeros_like(gate)
    for gp in range(N_GROUPS):
        shifted = gate if gp == 0 else pltpu.roll(gate, LANES - gp * EXPERTS_PER_GROUP, 1)
        gate4 = jnp.where((g_idx == gp) & (lane < EXPERTS_PER_GROUP), shifted, gate4)
    hi, mid, lo = _split3(gate4)
    ext = hi + pltpu.roll(mid, EXPERTS_PER_GROUP, 1) + pltpu.roll(lo, 2 * EXPERTS_PER_GROUP, 1)
    h_scr[:, 0:D_MODEL] = h_hi
    h_scr[:, D_MODEL:] = ext.astype(BF16)

    onehot = jnp.where((lane.astype(F32) == g_idx) & (lane < N_GROUPS), 1.0, 0.0)
    ri = lax.broadcasted_iota(jnp.int32, (MOE_BLOCK, MOE_BLOCK), 0)
    ci = lax.broadcasted_iota(jnp.int32, (MOE_BLOCK, MOE_BLOCK), 1)
    tri = jnp.where(ci < ri, 1.0, 0.0).astype(BF16)
    ranks = []
    most = jnp.zeros((1, LANES), F32)
    for b in range(tm // MOE_BLOCK):
        oh_b = onehot[b * MOE_BLOCK:(b + 1) * MOE_BLOCK]
        ranks.append(_dot(tri, oh_b.astype(BF16)))
        most = jnp.maximum(most, jnp.sum(oh_b, axis=0, keepdims=True))
    r_own = jnp.sum(onehot * jnp.concatenate(ranks, axis=0), axis=-1, keepdims=True)
    meta = jnp.where(lane == META_GROUP, g_idx, jnp.where(lane == META_RANK, r_own, gate4))
    meta_scr[...] = meta
    metaT_scr[...] = meta.T[0:8, :]
    lane1 = lax.broadcasted_iota(jnp.int32, (1, LANES), 1)
    for gp in range(N_GROUPS):
        ovf_smem[gp] = (jnp.max(jnp.where(lane1 == gp, most, 0.0)) > MOE_SLOT).astype(jnp.int32)
    o_ref[...] = jnp.zeros_like(o_ref)


def _moe_kernel(x_ref, mod_ref, wr_ref, br_ref, win_ref, wout_ref, g_ref, b_ref, o_ref,
                h_scr, meta_scr, metaT_scr, act_scr, ovf_smem):
    g = pl.program_id(1)
    g_f = g.astype(F32)
    tm = x_ref.shape[0]
    n_blocks = tm // MOE_BLOCK
    n_rows = n_blocks * MOE_SLOT

    @pl.when(g == 0)
    def _():
        _moe_route_tile(x_ref, mod_ref, wr_ref, br_ref, h_scr, meta_scr, metaT_scr, o_ref, ovf_smem)

    def experts(xb, gates, rows):
        for e in range(EXPERTS_PER_GROUP):
            hid = _dot(xb, win_ref[e].astype(BF16))
            act = _silu(hid[:, :D_EXPERT]) * hid[:, D_EXPERT:] * gates[:, e:e + 1]
            act_scr[rows, e * D_EXPERT:(e + 1) * D_EXPERT] = act.astype(BF16)
        return _dot(act_scr[rows, :], wout_ref[...].astype(BF16))

    @pl.when(ovf_smem[g] == 0)
    def _():
        slot_row = lax.broadcasted_iota(jnp.int32, (MOE_SLOT, MOE_BLOCK), 0).astype(F32)
        gathered = []
        for b in range(n_blocks):
            tok = slice(b * MOE_BLOCK, (b + 1) * MOE_BLOCK)
            pick = ((metaT_scr[META_GROUP:META_GROUP + 1, tok] == g_f)
                    & (metaT_scr[META_RANK:META_RANK + 1, tok] == slot_row))
            gathered.append(_dot(jnp.where(pick, 1.0, 0.0).astype(BF16), h_scr[tok, :]))
        xg = jnp.concatenate(gathered, axis=0)
        ge = xg[:, D_MODEL:]
        gates = (ge + pltpu.roll(ge, LANES - EXPERTS_PER_GROUP, 1)) + pltpu.roll(ge, LANES - 2 * EXPERTS_PER_GROUP, 1)
        y = experts(xg[:, :D_MODEL].astype(BF16), gates, slice(0, n_rows)).astype(BF16)
        slot_col = lax.broadcasted_iota(jnp.int32, (MOE_BLOCK, SCATTER_K), 1).astype(F32)
        for b in range(n_blocks):
            tok = slice(b * MOE_BLOCK, (b + 1) * MOE_BLOCK)
            first = min(b * MOE_SLOT, n_rows - SCATTER_K)
            place = ((meta_scr[tok, META_GROUP:META_GROUP + 1] == g_f)
                     & (meta_scr[tok, META_RANK:META_RANK + 1] + (b * MOE_SLOT - first) == slot_col))
            o_ref[tok, :] += _dot(jnp.where(place, 1.0, 0.0).astype(BF16), y[first:first + SCATTER_K, :])

    @pl.when(ovf_smem[g] != 0)
    def _():
        for b in range(n_blocks):
            tok = slice(b * MOE_BLOCK, (b + 1) * MOE_BLOCK)
            gates = jnp.where(meta_scr[tok, META_GROUP:META_GROUP + 1] == g_f, meta_scr[tok, :], 0.0)
            o_ref[tok, :] += experts(h_scr[tok, 0:D_MODEL], gates, slice(0, MOE_BLOCK))

    @pl.when(g == N_GROUPS - 1)
    def _():
        seg = tm // mod_ref.shape[0]
        for s in range(tm // seg):
            tok = slice(s * seg, (s + 1) * seg)
            gf = mod_ref[s, :, 5 * D_MODEL:6 * D_MODEL]
            o_ref[tok, :] = _layer_norm(ALPHA * x_ref[tok, :] + gf * o_ref[tok, :], g_ref[...], b_ref[...])


def _moe(l, x, mod_rows, w_router, b_router, w_exp_in, w_exp_out, ln_g, ln_b):
    n = x.shape[0]
    tm = MOE_TOKENS
    n_seg = mod_rows.shape[2]
    n_rows = tm // MOE_BLOCK * MOE_SLOT
    assert tm % MOE_BLOCK == 0 and MOE_SLOT % 16 == 0 and SCATTER_K <= n_rows
    return pl.pallas_call(
        _moe_kernel,
        grid=(n // tm, N_GROUPS),
        in_specs=[
            pl.BlockSpec((tm, D_MODEL), lambda i, g: (i, 0), pipeline_mode=pl.Buffered(1)),
            pl.BlockSpec((None, None, n_seg, 1, 6 * D_MODEL), lambda i, g: (l, i, 0, 0, 0)),
            pl.BlockSpec((None, D_MODEL, LANES), lambda i, g: (l, 0, 0)),
            pl.BlockSpec((None, 1, LANES), lambda i, g: (l, 0, 0)),
            pl.BlockSpec((None, EXPERTS_PER_GROUP, D_MODEL, 2 * D_EXPERT), lambda i, g: (l, g, 0, 0)),
            pl.BlockSpec((None, None, EXPERTS_PER_GROUP * D_EXPERT, D_MODEL), lambda i, g: (l, g, 0, 0)),
            pl.BlockSpec((None, 1, D_MODEL), lambda i, g: (l, 0, 0)),
            pl.BlockSpec((None, 1, D_MODEL), lambda i, g: (l, 0, 0)),
        ],
        out_specs=pl.BlockSpec((tm, D_MODEL), lambda i, g: (i, 0), pipeline_mode=pl.Buffered(1)),
        out_shape=jax.ShapeDtypeStruct(x.shape, F32),
        scratch_shapes=[
            pltpu.VMEM((tm, D_MODEL + GATE_COLS), BF16),
            pltpu.VMEM((tm, LANES), F32),
            pltpu.VMEM((8, tm), F32),
            pltpu.VMEM((max(n_rows, MOE_BLOCK), EXPERTS_PER_GROUP * D_EXPERT), BF16),
            pltpu.SMEM((N_GROUPS,), jnp.int32),
        ],
        compiler_params=pltpu.CompilerParams(
            dimension_semantics=("arbitrary", "arbitrary"), vmem_limit_bytes=VMEM_LIMIT_BYTES),
        name="hier_moe",
    )(x, mod_rows, w_router, b_router, w_exp_in, w_exp_out, ln_g, ln_b)


def kernel(x_prompt, x_sample, cache_k, cache_v, c, c_ctx, w_ada, b_ada, w_in, conv_w, rpb, w_out,
           ln1_g, ln1_b, w_router_group, b_router_group, w_router_expert, b_router_expert,
           w_expert_in, w_expert_out, ln2_g, ln2_b):
    batch, seq_len, _ = x_prompt.shape
    dec_batch, dec_seq, _ = x_sample.shape
    assert dec_batch + 1 <= 8 and MOE_TOKENS % dec_seq == 0
    assert (dec_batch * dec_seq) % MOE_TOKENS == 0 and (batch * seq_len) % MOE_TOKENS == 0

    cond = jnp.concatenate([c_ctx[None, :], c, jnp.zeros((8 - 1 - dec_batch, D_MODEL), F32)], axis=0)
    mod = _modulation(cond, w_ada, b_ada)
    n_ctx_tiles = batch * seq_len // MOE_TOKENS
    mod_rows_ctx = jnp.broadcast_to(mod[:, 0:1, None, None, :], (DEPTH, n_ctx_tiles, 1, 1, 6 * D_MODEL))
    mod_rows_lat = mod[:, 1:1 + dec_batch].reshape(DEPTH, -1, MOE_TOKENS // dec_seq, 1, 6 * D_MODEL)

    w_in_bf = w_in.astype(BF16)
    w_out_bf = w_out.astype(BF16)
    pad = jnp.zeros((DEPTH, D_MODEL, LANES - N_EXPERTS - N_GROUPS), F32)
    w_router = jnp.concatenate([w_router_expert, w_router_group, pad], axis=-1)
    b_router = jnp.concatenate([b_router_expert, b_router_group, pad[:, 0, :]], axis=-1)[:, None, :]
    tbl = _bias_table(rpb)
    cache_kT = jnp.swapaxes(cache_k, -1, -2)
    cache_vT = jnp.swapaxes(cache_v, -1, -2)
    ln1_g3, ln1_b3 = ln1_g[:, None, :], ln1_b[:, None, :]
    ln2_g3, ln2_b3 = ln2_g[:, None, :], ln2_b[:, None, :]

    w_exp_out = w_expert_out.reshape(DEPTH, N_GROUPS, EXPERTS_PER_GROUP * D_EXPERT, D_MODEL)

    xp, xs = x_prompt, x_sample
    kv_bufs = [jnp.zeros((batch, DEPTH, N_HEADS, HEAD_DIM, seq_len), F32)] * 2
    for l in range(DEPTH):
        moe = functools.partial(_moe, l, w_router=w_router, b_router=b_router, w_exp_in=w_expert_in,
                                w_exp_out=w_exp_out, ln_g=ln2_g3, ln_b=ln2_b3)
        xp, *kv_bufs = _ctx_mixer(l, xp, mod, w_in_bf, w_out_bf, conv_w, ln1_g3, ln1_b3, kv_bufs)
        xp = moe(xp.reshape(-1, D_MODEL), mod_rows_ctx).reshape(xp.shape)
        xs = _lat_mixer(l, xs, mod, w_in_bf, w_out_bf, conv_w, ln1_g3, ln1_b3, cache_kT, cache_vT, tbl)
        xs = moe(xs.reshape(-1, D_MODEL), mod_rows_lat).reshape(xs.shape)
    new_kT, new_vT = kv_bufs
    return (xp, xs, jnp.swapaxes(new_kT, -1, -2), jnp.swapaxes(new_vT, -1, -2))
```

```python
import functools

import jax
import jax.numpy as jnp
from jax import lax
from jax.experimental import pallas as pl
from jax.experimental.pallas import tpu as pltpu

D_MODEL = 1024
DEPTH = 4
GRID_W = 64
ATT_WIDTH = D_MODEL // 2
CONV_WIDTH = D_MODEL - ATT_WIDTH
HEAD_DIM = 64
N_HEADS = ATT_WIDTH // HEAD_DIM
WIN_ROWS = 8
WIN_COLS = 16
N_GROUPS = 4
EXPERTS_PER_GROUP = 4
N_EXPERTS = N_GROUPS * EXPERTS_PER_GROUP
D_EXPERT = D_MODEL // 4
ALPHA = (2 * DEPTH) ** 0.25
LN_EPS = 1e-5
NEG_INF = -1e30
QK_SCALE = HEAD_DIM ** -0.5

F32 = jnp.float32
BF16 = jnp.bfloat16

LANES = 128
VMEM_LIMIT_BYTES = 56 * 1024 * 1024

CTX_SEQ_PER_STEP = 2
LAT_Q_CHUNK = 256
MOE_TOKENS = 1024
MOE_BLOCK = 512
MOE_SLOT = 192
SCATTER_K = 256
GATE_COLS = 128
ROUTE_ROWS = 24
REC_SPLIT, REC_GATE, REC_GROUP, REC_RANK = 0, 12, 16, 17
MOD_COLS = 1024


def _dot(a, b):
    return jnp.dot(a, b, preferred_element_type=F32)


def _dot_nt(a, b):
    return lax.dot_general(a, b, (((1,), (1,)), ((), ())), preferred_element_type=F32)


def _silu(x):
    return x * (1.0 / (1.0 + jnp.exp(-x)))


def _layer_norm(r, g, b):
    mu = jnp.mean(r, axis=-1, keepdims=True)
    d = r - mu
    var = jnp.mean(d * d, axis=-1, keepdims=True)
    return d * lax.rsqrt(var + LN_EPS) * g + b


def _mod_kernel(cond_ref, w_ref, b_ref, o_ref):
    s = _silu(cond_ref[...]).astype(BF16)
    o_ref[...] = _dot(s, w_ref[...].astype(BF16)) + b_ref[...]


def _modulation(cond, w_ada, b_ada):
    n_out = w_ada.shape[-1]
    return pl.pallas_call(
        _mod_kernel,
        grid=(DEPTH, n_out // MOD_COLS),
        in_specs=[
            pl.BlockSpec((8, D_MODEL), lambda l, j: (0, 0)),
            pl.BlockSpec((None, D_MODEL, MOD_COLS), lambda l, j: (l, 0, j)),
            pl.BlockSpec((None, 1, MOD_COLS), lambda l, j: (l, 0, j)),
        ],
        out_specs=pl.BlockSpec((None, 8, MOD_COLS), lambda l, j: (l, 0, j)),
        out_shape=jax.ShapeDtypeStruct((DEPTH, 8, n_out), F32),
        compiler_params=pltpu.CompilerParams(
            dimension_semantics=("arbitrary", "arbitrary"),
            vmem_limit_bytes=VMEM_LIMIT_BYTES),
        name="adaln_modulation",
    )(cond, w_ada, b_ada.reshape(DEPTH, 1, n_out))


def _project(x, mod, win_ref, convw_ref, qT_scr, k_scr, vT_scr, conv_scr, seq_len, kT_out=None, vT_out=None):
    m = x.shape[0]
    sa = mod[:, 0:D_MODEL]
    ca = mod[:, D_MODEL:2 * D_MODEL]
    h = (x * (1.0 + ca) + sa).astype(BF16)

    qT_scr[...] = (_dot(h, win_ref[:, 0:ATT_WIDTH]) * QK_SCALE).T.astype(BF16)
    zk = _dot(h, win_ref[:, ATT_WIDTH:2 * ATT_WIDTH])
    for hd in range(N_HEADS):
        k_scr[hd] = zk[:, hd * HEAD_DIM:(hd + 1) * HEAD_DIM].astype(BF16)
    zvT = _dot(h, win_ref[:, 2 * ATT_WIDTH:3 * ATT_WIDTH]).T
    vT_scr[...] = zvT.astype(BF16)
    if kT_out is not None:
        zkT = zk.T
        for out, zT in ((kT_out, zkT), (vT_out, zvT)):
            for s in range(m // seq_len):
                for hd in range(N_HEADS):
                    out[s, hd] = zT[hd * HEAD_DIM:(hd + 1) * HEAD_DIM, s * seq_len:(s + 1) * seq_len]

    c0 = 3 * ATT_WIDTH
    bg = _dot(h, win_ref[:, c0:c0 + CONV_WIDTH])
    cg = _dot(h, win_ref[:, c0 + CONV_WIDTH:c0 + 2 * CONV_WIDTH])
    u = _dot(h, win_ref[:, c0 + 2 * CONV_WIDTH:c0 + 3 * CONV_WIDTH])
    y = cg * u
    t = lax.broadcasted_iota(jnp.int32, (m, 1), 0) % seq_len
    y_prev = jnp.where(t == 0, 0.0, pltpu.roll(y, 1, 0))
    y_next = jnp.where(t == seq_len - 1, 0.0, pltpu.roll(y, m - 1, 0))
    cw = convw_ref[...]
    conv = cw[0:1, :] * y_prev + cw[1:2, :] * y + cw[2:3, :] * y_next
    conv_scr[...] = (bg * conv).astype(BF16)


def _merge_and_norm(x, mod, attT_scr, conv_scr, wout_ref, g_ref, b_ref, o_ref):
    att = attT_scr[...].T.astype(BF16)
    mix = _dot(att, wout_ref[0:ATT_WIDTH, :]) + _dot(conv_scr[...], wout_ref[ATT_WIDTH:, :])
    ga = mod[:, 2 * D_MODEL:3 * D_MODEL]
    o_ref[...] = _layer_norm(ALPHA * x + ga * mix, g_ref[...], b_ref[...]).reshape(o_ref.shape)


def _softmax_keys(parts):
    mx = functools.reduce(jnp.maximum, [jnp.max(p, axis=0, keepdims=True) for p in parts])
    es = [jnp.exp(p - mx) for p in parts]
    inv = 1.0 / functools.reduce(jnp.add, [jnp.sum(e, axis=0, keepdims=True) for e in es])
    return [(e * inv).astype(BF16) for e in es]


def _head_rows(hd):
    return pl.ds(pl.multiple_of(hd * HEAD_DIM, HEAD_DIM), HEAD_DIM)


def _weight_spec(shape, index_map):
    return pl.BlockSpec(shape, index_map, pipeline_mode=pl.Buffered(1))


def _ctx_mixer_kernel(x_ref, mod_ref, win_ref, wout_ref, convw_ref, g_ref, b_ref, kT_in, vT_in,
                      o_ref, kT_ref, vT_ref, qT_scr, k_scr, vT_scr, attT_scr, conv_scr, sT_scr, pT_scr):
    del kT_in, vT_in
    sb, seq_len, _ = x_ref.shape
    x = x_ref[...].reshape(sb * seq_len, D_MODEL)
    mod = mod_ref[0:1, :]
    _project(x, mod, win_ref, convw_ref, qT_scr, k_scr, vT_scr, conv_scr, seq_len, kT_ref, vT_ref)

    pairs = [(hd, s) for hd in range(N_HEADS) for s in range(sb)]
    chan = lambda hd: slice(hd * HEAD_DIM, (hd + 1) * HEAD_DIM)
    tok = lambda s: slice(s * seq_len, (s + 1) * seq_len)
    for i, (hd, s) in enumerate(pairs):
        sT_scr[i] = _dot(k_scr[hd, tok(s), :], qT_scr[chan(hd), tok(s)])
    for i in range(len(pairs)):
        (pT_scr[i],) = _softmax_keys([sT_scr[i]])
    for i, (hd, s) in enumerate(pairs):
        attT_scr[chan(hd), tok(s)] = _dot(vT_scr[chan(hd), tok(s)], pT_scr[i])

    _merge_and_norm(x, mod, attT_scr, conv_scr, wout_ref, g_ref, b_ref, o_ref)


def _ctx_mixer(l, xp, mod, w_in, w_out, conv_w, ln_g, ln_b, kv_bufs):
    batch, seq_len, _ = xp.shape
    sb = CTX_SEQ_PER_STEP
    m = sb * seq_len
    kv_shape = jax.ShapeDtypeStruct((batch, DEPTH, N_HEADS, HEAD_DIM, seq_len), F32)
    kv_spec = pl.BlockSpec((sb, None, N_HEADS, HEAD_DIM, seq_len), lambda i: (i, l, 0, 0, 0))
    in_specs = [
        pl.BlockSpec((sb, seq_len, D_MODEL), lambda i: (i, 0, 0)),
        pl.BlockSpec((None, 8, 6 * D_MODEL), lambda i: (l, 0, 0)),
        _weight_spec((None, D_MODEL, 3 * ATT_WIDTH + 3 * CONV_WIDTH), lambda i: (l, 0, 0)),
        _weight_spec((None, D_MODEL, D_MODEL), lambda i: (l, 0, 0)),
        pl.BlockSpec((None, 3, CONV_WIDTH), lambda i: (l, 0, 0)),
        pl.BlockSpec((None, 1, D_MODEL), lambda i: (l, 0, 0)),
        pl.BlockSpec((None, 1, D_MODEL), lambda i: (l, 0, 0)),
        pl.BlockSpec(memory_space=pl.ANY),
        pl.BlockSpec(memory_space=pl.ANY),
    ]
    args = [xp, mod, w_in, w_out, conv_w, ln_g, ln_b, *kv_bufs]
    return pl.pallas_call(
        _ctx_mixer_kernel,
        grid=(batch // sb,),
        in_specs=in_specs,
        out_specs=[pl.BlockSpec((sb, seq_len, D_MODEL), lambda i: (i, 0, 0)), kv_spec, kv_spec],
        out_shape=[jax.ShapeDtypeStruct(xp.shape, F32), kv_shape, kv_shape],
        input_output_aliases={len(args) - 2: 1, len(args) - 1: 2},
        scratch_shapes=[
            pltpu.VMEM((ATT_WIDTH, m), BF16),
            pltpu.VMEM((N_HEADS, m, HEAD_DIM), BF16),
            pltpu.VMEM((ATT_WIDTH, m), BF16),
            pltpu.VMEM((ATT_WIDTH, m), F32),
            pltpu.VMEM((m, CONV_WIDTH), BF16),
            pltpu.VMEM((N_HEADS * sb, seq_len, seq_len), F32),
            pltpu.VMEM((N_HEADS * sb, seq_len, seq_len), BF16),
        ],
        compiler_params=pltpu.CompilerParams(
            dimension_semantics=("arbitrary",), vmem_limit_bytes=VMEM_LIMIT_BYTES),
        name="ctx_mixer",
    )(*args)


def _window_start(r, rows):
    return min(max(r - WIN_ROWS // 2, 0), rows - WIN_ROWS)


def _chunk_key_rows(r0, chunk_rows, rows):
    lo = _window_start(r0, rows) // 2 * 2
    hi = -(-(_window_start(r0 + chunk_rows - 1, rows) + WIN_ROWS) // 2) * 2
    return lo, hi


def _lat_mixer_kernel(x_ref, mod_ref, win_ref, wout_ref, convw_ref, g_ref, b_ref,
                      ckT_ref, cvT_ref, tbl_ref, o_ref,
                      qT_scr, k_scr, vT_scr, attT_scr, conv_scr, sT_scr, pT_scr):
    b = pl.program_id(0)
    hd = pl.program_id(1)
    seq_len = x_ref.shape[0]
    rows = seq_len // GRID_W
    mod = mod_ref[pl.ds(1 + b, 1), :]

    @pl.when(hd == 0)
    def _():
        _project(x_ref[...], mod, win_ref, convw_ref, qT_scr, k_scr, vT_scr, conv_scr, seq_len)

    chan = _head_rows(hd)
    ckT = ckT_ref[...]
    ck = jnp.concatenate([ckT, jnp.zeros_like(ckT)], axis=0).T.astype(BF16)
    cvT = cvT_ref[...].astype(BF16)
    low_half = lax.broadcasted_iota(jnp.int32, (GRID_W, 2 * GRID_W), 1) < GRID_W

    def bias_block(r_pair, rk):
        halves = []
        for r in (r_pair, r_pair + 1):
            rs = _window_start(r, rows)
            if rs <= rk < rs + WIN_ROWS:
                dr = rk - r + WIN_ROWS - 1
                halves.append(tbl_ref[dr * GRID_W:(dr + 1) * GRID_W, :])
            else:
                halves.append(jnp.full((GRID_W, 2 * GRID_W), NEG_INF, F32))
        return jnp.where(low_half, halves[0], halves[1])

    chunk_rows = LAT_Q_CHUNK // GRID_W
    past = ck.shape[0]
    max_loc = sT_scr.shape[1] - past
    chunks = []
    for qc in range(seq_len // LAT_Q_CHUNK):
        r0 = qc * chunk_rows
        k_lo, k_hi = _chunk_key_rows(r0, chunk_rows, rows)
        chunks.append((qc, r0, k_lo, k_hi, slice(qc * LAT_Q_CHUNK, (qc + 1) * LAT_Q_CHUNK)))
    for qc, r0, k_lo, k_hi, qcols in chunks:
        qT = qT_scr[chan, qcols]
        bias = jnp.concatenate(
            [jnp.concatenate([bias_block(r0 + j, rk) for j in range(0, chunk_rows, 2)], axis=1)
             for rk in range(k_lo, k_hi)], axis=0)
        sT_scr[qc, 0:(k_hi - k_lo) * GRID_W] = _dot(k_scr[hd, k_lo * GRID_W:k_hi * GRID_W, :], qT) + bias
        sT_scr[qc, max_loc:] = _dot(ck, jnp.concatenate([qT, jnp.zeros_like(qT)], axis=0))
    for qc, r0, k_lo, k_hi, qcols in chunks:
        n_loc = (k_hi - k_lo) * GRID_W
        pT_scr[qc, 0:n_loc], pT_scr[qc, max_loc:] = _softmax_keys([sT_scr[qc, 0:n_loc], sT_scr[qc, max_loc:]])
    for qc, r0, k_lo, k_hi, qcols in chunks:
        n_loc = (k_hi - k_lo) * GRID_W
        attT_scr[chan, qcols] = (_dot(vT_scr[chan, k_lo * GRID_W:k_hi * GRID_W], pT_scr[qc, 0:n_loc])
                                 + _dot(cvT, pT_scr[qc, max_loc:]))

    @pl.when(hd == N_HEADS - 1)
    def _():
        _merge_and_norm(x_ref[...], mod, attT_scr, conv_scr, wout_ref, g_ref, b_ref, o_ref)


def _lat_mixer(l, xs, mod, w_in, w_out, conv_w, ln_g, ln_b, cache_kT, cache_vT, tbl):
    batch, seq_len, _ = xs.shape
    past = cache_kT.shape[-1]
    rows, chunk_rows, n_chunks = seq_len // GRID_W, LAT_Q_CHUNK // GRID_W, seq_len // LAT_Q_CHUNK
    max_loc = GRID_W * max(hi - lo for lo, hi in
                           (_chunk_key_rows(qc * chunk_rows, chunk_rows, rows) for qc in range(n_chunks)))
    cache_spec = pl.BlockSpec((None, None, None, HEAD_DIM, past), lambda b, h: (b, l, h, 0, 0))
    return pl.pallas_call(
        _lat_mixer_kernel,
        grid=(batch, N_HEADS),
        in_specs=[
            pl.BlockSpec((None, seq_len, D_MODEL), lambda b, h: (b, 0, 0)),
            pl.BlockSpec((None, 8, 6 * D_MODEL), lambda b, h: (l, 0, 0)),
            _weight_spec((None, D_MODEL, 3 * ATT_WIDTH + 3 * CONV_WIDTH), lambda b, h: (l, 0, 0)),
            _weight_spec((None, D_MODEL, D_MODEL), lambda b, h: (l, 0, 0)),
            pl.BlockSpec((None, 3, CONV_WIDTH), lambda b, h: (l, 0, 0)),
            pl.BlockSpec((None, 1, D_MODEL), lambda b, h: (l, 0, 0)),
            pl.BlockSpec((None, 1, D_MODEL), lambda b, h: (l, 0, 0)),
            cache_spec, cache_spec,
            pl.BlockSpec((None, None) + tbl.shape[2:], lambda b, h: (l, h, 0, 0)),
        ],
        out_specs=pl.BlockSpec((None, seq_len, D_MODEL), lambda b, h: (b, 0, 0)),
        out_shape=jax.ShapeDtypeStruct(xs.shape, F32),
        scratch_shapes=[
            pltpu.VMEM((ATT_WIDTH, seq_len), BF16),
            pltpu.VMEM((N_HEADS, seq_len, HEAD_DIM), BF16),
            pltpu.VMEM((ATT_WIDTH, seq_len), BF16),
            pltpu.VMEM((ATT_WIDTH, seq_len), F32),
            pltpu.VMEM((seq_len, CONV_WIDTH), BF16),
            pltpu.VMEM((n_chunks, max_loc + past, LAT_Q_CHUNK), F32),
            pltpu.VMEM((n_chunks, max_loc + past, LAT_Q_CHUNK), BF16),
        ],
        compiler_params=pltpu.CompilerParams(
            dimension_semantics=("arbitrary", "arbitrary"), vmem_limit_bytes=VMEM_LIMIT_BYTES),
        name="lat_mixer",
    )(xs, mod, w_in, w_out, conv_w, ln_g, ln_b, cache_kT, cache_vT, tbl)


def _bias_table(rpb):
    depth, heads, n_dr, n_dc = rpb.shape
    lane0 = GRID_W - WIN_COLS
    rpb_pad = jnp.pad(rpb[..., ::-1], ((0, 0), (0, 0), (0, 16 - n_dr), (lane0, LANES - lane0 - n_dc)))
    rpb_pad = rpb_pad.reshape(depth * heads, 16, LANES)

    def body(r_ref, o_ref):
        cp = lax.broadcasted_iota(jnp.int32, (GRID_W, LANES), 0)
        lane = lax.broadcasted_iota(jnp.int32, (GRID_W, LANES), 1)
        low_half = lane < GRID_W
        c = jnp.where(low_half, lane, lane - GRID_W)
        col_start = jnp.clip(c - WIN_COLS // 2, 0, GRID_W - WIN_COLS)
        valid = (cp >= col_start) & (cp < col_start + WIN_COLS)
        for dr in range(n_dr):
            row = jnp.broadcast_to(r_ref[dr:dr + 1, :], (GRID_W, LANES))
            lo = pltpu.roll(row, LANES - GRID_W + 1, 1, stride=1, stride_axis=0)
            hi = pltpu.roll(row, 1, 1, stride=1, stride_axis=0)
            o_ref[dr * GRID_W:(dr + 1) * GRID_W, :] = jnp.where(valid, jnp.where(low_half, lo, hi), NEG_INF)

    tbl = pl.pallas_call(
        body,
        grid=(depth * heads,),
        in_specs=[pl.BlockSpec((None, 16, LANES), lambda i: (i, 0, 0))],
        out_specs=pl.BlockSpec((None, n_dr * GRID_W, LANES), lambda i: (i, 0, 0)),
        out_shape=jax.ShapeDtypeStruct((depth * heads, n_dr * GRID_W, LANES), F32),
        compiler_params=pltpu.CompilerParams(dimension_semantics=("arbitrary",)),
        name="bias_table",
    )(rpb_pad)
    return tbl.reshape(depth, heads, n_dr * GRID_W, LANES)


def _route_t(logits):
    row = lax.broadcasted_iota(jnp.int32, logits.shape, 0)
    row_f = row.astype(F32)
    big = jnp.float32(LANES)

    def first_row(cond):
        return jnp.min(jnp.where(cond, row_f, big), axis=0, keepdims=True)

    gmask = (row >= N_EXPERTS) & (row < N_EXPERTS + N_GROUPS)
    gl = jnp.where(gmask, logits, NEG_INF)
    gexp = jnp.exp(gl - jnp.max(gl, axis=0, keepdims=True))
    gprob = gexp / jnp.sum(gexp, axis=0, keepdims=True)
    g_p = jnp.max(gprob, axis=0, keepdims=True)
    g_idx = first_row(gmask & (gprob == g_p)) - N_EXPERTS

    row_group = jnp.floor(row_f * (1.0 / EXPERTS_PER_GROUP))
    emask = (row < N_EXPERTS) & (row_group == g_idx)
    el = jnp.where(emask, logits, NEG_INF)
    eexp = jnp.exp(el - jnp.max(el, axis=0, keepdims=True))
    eprob = eexp / jnp.sum(eexp, axis=0, keepdims=True)
    p1 = jnp.max(eprob, axis=0, keepdims=True)
    i1 = first_row(emask & (eprob == p1))
    rest = emask & (row_f != i1)
    p2 = jnp.max(jnp.where(rest, eprob, -1.0), axis=0, keepdims=True)
    i2 = first_row(rest & (eprob == p2))
    denom = p1 + p2
    gate = (jnp.where(row_f == i1, g_p * p1 / denom, 0.0)
            + jnp.where(row_f == i2, g_p * p2 / denom, 0.0))
    return gate, g_idx


def _split3(v):
    hi = v.astype(BF16).astype(F32)
    mid = (v - hi).astype(BF16).astype(F32)
    return hi, mid, v - hi - mid


def _moe_route_tile(x_ref, mod_ref, wrT_ref, brT_ref, h_scr, rec_scr, recT_scr, o_ref, ovf_smem):
    tm = x_ref.shape[0]
    seg = tm // mod_ref.shape[0]
    h = jnp.concatenate(
        [x_ref[s * seg:(s + 1) * seg, :] * (1.0 + mod_ref[s, :, 4 * D_MODEL:5 * D_MODEL])
         + mod_ref[s, :, 3 * D_MODEL:4 * D_MODEL] for s in range(tm // seg)], axis=0)
    h_hi = h.astype(BF16)
    h_lo = (h - h_hi.astype(F32)).astype(BF16)
    wr = wrT_ref[...]
    wr_hi = wr.astype(BF16)
    wr_lo = (wr - wr_hi.astype(F32)).astype(BF16)
    logits = (_dot_nt(wr_hi, h_hi) + _dot_nt(wr_hi, h_lo) + _dot_nt(wr_lo, h_hi))[0:ROUTE_ROWS, :]
    gate, g_idx = _route_t(logits + brT_ref[0:ROUTE_ROWS, :])

    row_f = lax.broadcasted_iota(jnp.int32, (ROUTE_ROWS, tm), 0).astype(F32)
    gate4 = jnp.concatenate(
        [jnp.sum(jnp.where(row_f == EXPERTS_PER_GROUP * g_idx + j, gate, 0.0), axis=0, keepdims=True)
         for j in range(EXPERTS_PER_GROUP)], axis=0)

    grp = lax.broadcasted_iota(jnp.int32, (8, tm), 0).astype(F32)
    onehot = jnp.where(grp == g_idx, 1.0, 0.0)
    ri = lax.broadcasted_iota(jnp.int32, (MOE_BLOCK, MOE_BLOCK), 0)
    ci = lax.broadcasted_iota(jnp.int32, (MOE_BLOCK, MOE_BLOCK), 1)
    earlier = jnp.where(ri < ci, 1.0, 0.0).astype(BF16)
    ranks = []
    most = jnp.zeros((8, 1), F32)
    for b in range(tm // MOE_BLOCK):
        oh_b = onehot[:, b * MOE_BLOCK:(b + 1) * MOE_BLOCK]
        ranks.append(_dot(oh_b.astype(BF16), earlier))
        most = jnp.maximum(most, jnp.sum(oh_b, axis=1, keepdims=True))
    rank = jnp.sum(onehot * jnp.concatenate(ranks, axis=1), axis=0, keepdims=True)

    recT = jnp.concatenate([*_split3(gate4), gate4, g_idx, rank,
                            jnp.zeros((LANES - REC_RANK - 1, tm), F32)], axis=0)
    rec = recT.T
    h_scr[:, 0:D_MODEL] = h_hi
    h_scr[:, D_MODEL:] = rec.astype(BF16)
    rec_scr[...] = rec
    recT_scr[...] = recT[REC_GROUP:REC_GROUP + 8, :]
    row8 = lax.broadcasted_iota(jnp.int32, (8, 1), 0)
    for gp in range(N_GROUPS):
        ovf_smem[gp] = (jnp.max(jnp.where(row8 == gp, most, 0.0)) > MOE_SLOT).astype(jnp.int32)
    o_ref[...] = jnp.zeros_like(o_ref)


def _moe_kernel(x_ref, mod_ref, wrT_ref, brT_ref, win_ref, wout_ref, g_ref, b_ref, o_ref,
                h_scr, rec_scr, recT_scr, act_scr, ovf_smem):
    g = pl.program_id(1)
    g_f = g.astype(F32)
    tm = x_ref.shape[0]
    n_blocks = tm // MOE_BLOCK
    n_rows = n_blocks * MOE_SLOT

    @pl.when(g == 0)
    def _():
        _moe_route_tile(x_ref, mod_ref, wrT_ref, brT_ref, h_scr, rec_scr, recT_scr, o_ref, ovf_smem)

    def experts(xb, gates, gate_lane, rows):
        for e in range(EXPERTS_PER_GROUP):
            hid = _dot(xb, win_ref[e].astype(BF16))
            act = _silu(hid[:, :D_EXPERT]) * hid[:, D_EXPERT:] * gates[:, gate_lane + e:gate_lane + e + 1]
            act_scr[rows, e * D_EXPERT:(e + 1) * D_EXPERT] = act.astype(BF16)
        return _dot(act_scr[rows, :], wout_ref[...].astype(BF16))

    @pl.when(ovf_smem[g] == 0)
    def _():
        slot_row = lax.broadcasted_iota(jnp.int32, (MOE_SLOT, MOE_BLOCK), 0).astype(F32)
        gathered = []
        for b in range(n_blocks):
            tok = slice(b * MOE_BLOCK, (b + 1) * MOE_BLOCK)
            pick = (recT_scr[0:1, tok] == g_f) & (recT_scr[1:2, tok] == slot_row)
            gathered.append(_dot(jnp.where(pick, 1.0, 0.0).astype(BF16), h_scr[tok, :]))
        xg = jnp.concatenate(gathered, axis=0)
        ge = xg[:, D_MODEL:]
        gates = ((ge + pltpu.roll(ge, LANES - EXPERTS_PER_GROUP, 1))
                 + pltpu.roll(ge, LANES - 2 * EXPERTS_PER_GROUP, 1))
        y = experts(xg[:, :D_MODEL].astype(BF16), gates, REC_SPLIT, slice(0, n_rows)).astype(BF16)
        slot_col = lax.broadcasted_iota(jnp.int32, (MOE_BLOCK, SCATTER_K), 1).astype(F32)
        for b in range(n_blocks):
            tok = slice(b * MOE_BLOCK, (b + 1) * MOE_BLOCK)
            first = min(b * MOE_SLOT, n_rows - SCATTER_K)
            place = ((rec_scr[tok, REC_GROUP:REC_GROUP + 1] == g_f)
                     & (rec_scr[tok, REC_RANK:REC_RANK + 1] + (b * MOE_SLOT - first) == slot_col))
            o_ref[tok, :] += _dot(jnp.where(place, 1.0, 0.0).astype(BF16), y[first:first + SCATTER_K, :])

    @pl.when(ovf_smem[g] != 0)
    def _():
        for b in range(n_blocks):
            tok = slice(b * MOE_BLOCK, (b + 1) * MOE_BLOCK)
            gates = jnp.where(rec_scr[tok, REC_GROUP:REC_GROUP + 1] == g_f, rec_scr[tok, :], 0.0)
            o_ref[tok, :] += experts(h_scr[tok, 0:D_MODEL], gates, REC_GATE, slice(0, MOE_BLOCK))

    @pl.when(g == N_GROUPS - 1)
    def _():
        seg = tm // mod_ref.shape[0]
        for s in range(tm // seg):
            tok = slice(s * seg, (s + 1) * seg)
            gf = mod_ref[s, :, 5 * D_MODEL:6 * D_MODEL]
            o_ref[tok, :] = _layer_norm(ALPHA * x_ref[tok, :] + gf * o_ref[tok, :], g_ref[...], b_ref[...])


def _moe(l, x, mod_rows, w_routerT, b_routerT, w_exp_in, w_exp_out, ln_g, ln_b):
    n = x.shape[0]
    tm = MOE_TOKENS
    n_seg = mod_rows.shape[2]
    n_rows = tm // MOE_BLOCK * MOE_SLOT
    assert tm % MOE_BLOCK == 0 and MOE_SLOT % 16 == 0 and SCATTER_K <= n_rows
    assert (n_rows - SCATTER_K) % 16 == 0 and MOE_SLOT <= SCATTER_K
    return pl.pallas_call(
        _moe_kernel,
        grid=(n // tm, N_GROUPS),
        in_specs=[
            pl.BlockSpec((tm, D_MODEL), lambda i, g: (i, 0)),
            pl.BlockSpec((None, None, n_seg, 1, 6 * D_MODEL), lambda i, g: (l, i, 0, 0, 0)),
            pl.BlockSpec((None, LANES, D_MODEL), lambda i, g: (l, 0, 0)),
            pl.BlockSpec((None, LANES, 1), lambda i, g: (l, 0, 0)),
            pl.BlockSpec((None, EXPERTS_PER_GROUP, D_MODEL, 2 * D_EXPERT), lambda i, g: (l, g, 0, 0)),
            pl.BlockSpec((None, None, EXPERTS_PER_GROUP * D_EXPERT, D_MODEL), lambda i, g: (l, g, 0, 0)),
            pl.BlockSpec((None, 1, D_MODEL), lambda i, g: (l, 0, 0)),
            pl.BlockSpec((None, 1, D_MODEL), lambda i, g: (l, 0, 0)),
        ],
        out_specs=pl.BlockSpec((tm, D_MODEL), lambda i, g: (i, 0)),
        out_shape=jax.ShapeDtypeStruct(x.shape, F32),
        scratch_shapes=[
            pltpu.VMEM((tm, D_MODEL + GATE_COLS), BF16),
            pltpu.VMEM((tm, LANES), F32),
            pltpu.VMEM((8, tm), F32),
            pltpu.VMEM((max(n_rows, MOE_BLOCK), EXPERTS_PER_GROUP * D_EXPERT), BF16),
            pltpu.SMEM((N_GROUPS,), jnp.int32),
        ],
        compiler_params=pltpu.CompilerParams(
            dimension_semantics=("arbitrary", "arbitrary"), vmem_limit_bytes=VMEM_LIMIT_BYTES),
        name="hier_moe",
    )(x, mod_rows, w_routerT, b_routerT, w_exp_in, w_exp_out, ln_g, ln_b)


def kernel(x_prompt, x_sample, cache_k, cache_v, c, c_ctx, w_ada, b_ada, w_in, conv_w, rpb, w_out,
           ln1_g, ln1_b, w_router_group, b_router_group, w_router_expert, b_router_expert,
           w_expert_in, w_expert_out, ln2_g, ln2_b):
    batch, seq_len, _ = x_prompt.shape
    dec_batch, dec_seq, _ = x_sample.shape
    assert dec_batch + 1 <= 8 and MOE_TOKENS % dec_seq == 0
    assert (dec_batch * dec_seq) % MOE_TOKENS == 0 and (batch * seq_len) % MOE_TOKENS == 0

    cond = jnp.concatenate([c_ctx[None, :], c, jnp.zeros((8 - 1 - dec_batch, D_MODEL), F32)], axis=0)
    mod = _modulation(cond, w_ada, b_ada)
    n_ctx_tiles = batch * seq_len // MOE_TOKENS
    mod_rows_ctx = jnp.broadcast_to(mod[:, 0:1, None, None, :], (DEPTH, n_ctx_tiles, 1, 1, 6 * D_MODEL))
    mod_rows_lat = mod[:, 1:1 + dec_batch].reshape(DEPTH, -1, MOE_TOKENS // dec_seq, 1, 6 * D_MODEL)

    w_in_bf = w_in.astype(BF16)
    w_out_bf = w_out.astype(BF16)
    pad = jnp.zeros((DEPTH, LANES - N_EXPERTS - N_GROUPS, D_MODEL), F32)
    w_routerT = jnp.concatenate(
        [jnp.swapaxes(w_router_expert, 1, 2), jnp.swapaxes(w_router_group, 1, 2), pad], axis=1)
    b_routerT = jnp.concatenate([b_router_expert, b_router_group, pad[:, :, 0]], axis=-1)[:, :, None]
    tbl = _bias_table(rpb)
    cache_kT = jnp.swapaxes(cache_k, -1, -2)
    cache_vT = jnp.swapaxes(cache_v, -1, -2)
    ln1_g3, ln1_b3 = ln1_g[:, None, :], ln1_b[:, None, :]
    ln2_g3, ln2_b3 = ln2_g[:, None, :], ln2_b[:, None, :]

    w_exp_out = w_expert_out.reshape(DEPTH, N_GROUPS, EXPERTS_PER_GROUP * D_EXPERT, D_MODEL)

    xp, xs = x_prompt, x_sample
    kv_bufs = [jnp.zeros((batch, DEPTH, N_HEADS, HEAD_DIM, seq_len), F32)] * 2
    for l in range(DEPTH):
        moe = functools.partial(_moe, l, w_routerT=w_routerT, b_routerT=b_routerT, w_exp_in=w_expert_in,
                                w_exp_out=w_exp_out, ln_g=ln2_g3, ln_b=ln2_b3)
        xp, *kv_bufs = _ctx_mixer(l, xp, mod, w_in_bf, w_out_bf, conv_w, ln1_g3, ln1_b3, kv_bufs)
        xp = moe(xp.reshape(-1, D_MODEL), mod_rows_ctx).reshape(xp.shape)
        xs = _lat_mixer(l, xs, mod, w_in_bf, w_out_bf, conv_w, ln1_g3, ln1_b3, cache_kT, cache_vT, tbl)
        xs = moe(xs.reshape(-1, D_MODEL), mod_rows_lat).reshape(xs.shape)
    new_kT, new_vT = kv_bufs
    return (xp, xs, jnp.swapaxes(new_kT, -1, -2), jnp.swapaxes(new_vT, -1, -2))
```

```python
import functools

import jax
import jax.numpy as jnp
from jax import lax
from jax.experimental import pallas as pl
from jax.experimental.pallas import tpu as pltpu

D_MODEL = 1024
DEPTH = 4
GRID_W = 64
ATT_WIDTH = D_MODEL // 2
CONV_WIDTH = D_MODEL - ATT_WIDTH
HEAD_DIM = 64
N_HEADS = ATT_WIDTH // HEAD_DIM
WIN_ROWS = 8
WIN_COLS = 16
N_GROUPS = 4
EXPERTS_PER_GROUP = 4
N_EXPERTS = N_GROUPS * EXPERTS_PER_GROUP
D_EXPERT = D_MODEL // 4
ALPHA = (2 * DEPTH) ** 0.25
LN_EPS = 1e-5
NEG_INF = -1e30
QK_SCALE = HEAD_DIM ** -0.5

F32 = jnp.float32
BF16 = jnp.bfloat16

LANES = 128
VMEM_LIMIT_BYTES = 56 * 1024 * 1024

CTX_SEQ_PER_STEP = 2
LAT_Q_CHUNK = 128
MERGE_ROWS = 256
MOE_TOKENS = 1024
MOE_BLOCK = 512
MOE_SLOTS = (160, 224)
SCATTER_K = 256
GATE_COLS = 128
ROUTE_ROWS = 24
REC_SPLIT, REC_GATE, REC_GROUP, REC_RANK = 0, 12, 16, 17
MOD_COLS = 2048


def _dot(a, b):
    return jnp.dot(a, b, preferred_element_type=F32)


def _dot_nt(a, b):
    return lax.dot_general(a, b, (((1,), (1,)), ((), ())), preferred_element_type=F32)


def _silu(x):
    return x * (1.0 / (1.0 + jnp.exp(-x)))


def _layer_norm(r, g, b):
    mu = jnp.mean(r, axis=-1, keepdims=True)
    d = r - mu
    var = jnp.mean(d * d, axis=-1, keepdims=True)
    return d * lax.rsqrt(var + LN_EPS) * g + b


def _mod_kernel(cond_ref, w_ref, b_ref, o_ref):
    s = _silu(cond_ref[...]).astype(BF16)
    o_ref[...] = _dot(s, w_ref[...].astype(BF16)) + b_ref[...]


def _modulation(cond, w_ada, b_ada):
    n_out = w_ada.shape[-1]
    return pl.pallas_call(
        _mod_kernel,
        grid=(DEPTH, n_out // MOD_COLS),
        in_specs=[
            pl.BlockSpec((8, D_MODEL), lambda l, j: (0, 0)),
            pl.BlockSpec((None, D_MODEL, MOD_COLS), lambda l, j: (l, 0, j)),
            pl.BlockSpec((None, 1, MOD_COLS), lambda l, j: (l, 0, j)),
        ],
        out_specs=pl.BlockSpec((None, 8, MOD_COLS), lambda l, j: (l, 0, j)),
        out_shape=jax.ShapeDtypeStruct((DEPTH, 8, n_out), F32),
        compiler_params=pltpu.CompilerParams(
            dimension_semantics=("arbitrary", "arbitrary"),
            vmem_limit_bytes=VMEM_LIMIT_BYTES),
        name="adaln_modulation",
    )(cond, w_ada, b_ada.reshape(DEPTH, 1, n_out))


def _project(x, mod, win_ref, convw_ref, qT_scr, k_scr, vT_scr, conv_scr, seq_len, kT_out=None, vT_out=None):
    m = x.shape[0]
    sa = mod[:, 0:D_MODEL]
    ca = mod[:, D_MODEL:2 * D_MODEL]
    h = (x * (1.0 + ca) + sa).astype(BF16)

    qT_scr[...] = (_dot(h, win_ref[:, 0:ATT_WIDTH]) * QK_SCALE).T.astype(BF16)
    zk = _dot(h, win_ref[:, ATT_WIDTH:2 * ATT_WIDTH])
    for hd in range(N_HEADS):
        k_scr[hd] = zk[:, hd * HEAD_DIM:(hd + 1) * HEAD_DIM].astype(BF16)
    zvT = _dot(h, win_ref[:, 2 * ATT_WIDTH:3 * ATT_WIDTH]).T
    vT_scr[...] = zvT.astype(BF16)
    if kT_out is not None:
        zkT = zk.T
        for out, zT in ((kT_out, zkT), (vT_out, zvT)):
            for s in range(m // seq_len):
                for hd in range(N_HEADS):
                    out[s, hd] = zT[hd * HEAD_DIM:(hd + 1) * HEAD_DIM, s * seq_len:(s + 1) * seq_len]

    c0 = 3 * ATT_WIDTH
    bg = _dot(h, win_ref[:, c0:c0 + CONV_WIDTH])
    cg = _dot(h, win_ref[:, c0 + CONV_WIDTH:c0 + 2 * CONV_WIDTH])
    u = _dot(h, win_ref[:, c0 + 2 * CONV_WIDTH:c0 + 3 * CONV_WIDTH])
    y = cg * u
    t = lax.broadcasted_iota(jnp.int32, (m, 1), 0) % seq_len
    y_prev = jnp.where(t == 0, 0.0, pltpu.roll(y, 1, 0))
    y_next = jnp.where(t == seq_len - 1, 0.0, pltpu.roll(y, m - 1, 0))
    cw = convw_ref[...]
    conv = cw[0:1, :] * y_prev + cw[1:2, :] * y + cw[2:3, :] * y_next
    conv_scr[...] = (bg * conv).astype(BF16)


def _merge_and_norm(x, mod, attT_scr, conv_scr, wout_ref, g_ref, b_ref, o_ref):
    ga = mod[:, 2 * D_MODEL:3 * D_MODEL]
    out = []
    for r in range(0, x.shape[0], MERGE_ROWS):
        rows = slice(r, r + MERGE_ROWS)
        att = attT_scr[:, rows].T.astype(BF16)
        mix = _dot(att, wout_ref[0:ATT_WIDTH, :]) + _dot(conv_scr[rows, :], wout_ref[ATT_WIDTH:, :])
        out.append(_layer_norm(ALPHA * x[rows, :] + ga * mix, g_ref[...], b_ref[...]))
    o_ref[...] = jnp.concatenate(out, axis=0).reshape(o_ref.shape)


def _softmax_keys(parts):
    mx = functools.reduce(jnp.maximum, [jnp.max(p, axis=0, keepdims=True) for p in parts])
    es = [jnp.exp(p - mx) for p in parts]
    inv = 1.0 / functools.reduce(jnp.add, [jnp.sum(e, axis=0, keepdims=True) for e in es])
    return [e.astype(BF16) for e in es], inv


def _head_rows(hd):
    return pl.ds(pl.multiple_of(hd * HEAD_DIM, HEAD_DIM), HEAD_DIM)


def _weight_spec(shape, index_map):
    return pl.BlockSpec(shape, index_map, pipeline_mode=pl.Buffered(1))


def _ctx_mixer_kernel(x_ref, mod_ref, win_ref, wout_ref, convw_ref, g_ref, b_ref, kT_in, vT_in,
                      o_ref, kT_ref, vT_ref, qT_scr, k_scr, vT_scr, attT_scr, conv_scr, sT_scr, pT_scr, inv_scr):
    del kT_in, vT_in
    sb, seq_len, _ = x_ref.shape
    x = x_ref[...].reshape(sb * seq_len, D_MODEL)
    mod = mod_ref[0:1, :]
    _project(x, mod, win_ref, convw_ref, qT_scr, k_scr, vT_scr, conv_scr, seq_len, kT_ref, vT_ref)

    pairs = [(hd, s) for hd in range(N_HEADS) for s in range(sb)]
    chan = lambda hd: slice(hd * HEAD_DIM, (hd + 1) * HEAD_DIM)
    tok = lambda s: slice(s * seq_len, (s + 1) * seq_len)
    for i, (hd, s) in enumerate(pairs):
        sT_scr[i] = _dot(k_scr[hd, tok(s), :], qT_scr[chan(hd), tok(s)])
    for i in range(len(pairs)):
        (pT_scr[i],), inv_scr[i] = _softmax_keys([sT_scr[i]])
    for i, (hd, s) in enumerate(pairs):
        attT_scr[chan(hd), tok(s)] = _dot(vT_scr[chan(hd), tok(s)], pT_scr[i]) * inv_scr[i]

    _merge_and_norm(x, mod, attT_scr, conv_scr, wout_ref, g_ref, b_ref, o_ref)


def _ctx_mixer(l, xp, mod, w_in, w_out, conv_w, ln_g, ln_b, kv_bufs):
    batch, seq_len, _ = xp.shape
    sb = CTX_SEQ_PER_STEP
    m = sb * seq_len
    kv_shape = jax.ShapeDtypeStruct((batch, DEPTH, N_HEADS, HEAD_DIM, seq_len), F32)
    kv_spec = pl.BlockSpec((sb, None, N_HEADS, HEAD_DIM, seq_len), lambda i: (i, l, 0, 0, 0))
    in_specs = [
        pl.BlockSpec((sb, seq_len, D_MODEL), lambda i: (i, 0, 0)),
        pl.BlockSpec((None, 8, 6 * D_MODEL), lambda i: (l, 0, 0)),
        _weight_spec((None, D_MODEL, 3 * ATT_WIDTH + 3 * CONV_WIDTH), lambda i: (l, 0, 0)),
        _weight_spec((None, D_MODEL, D_MODEL), lambda i: (l, 0, 0)),
        pl.BlockSpec((None, 3, CONV_WIDTH), lambda i: (l, 0, 0)),
        pl.BlockSpec((None, 1, D_MODEL), lambda i: (l, 0, 0)),
        pl.BlockSpec((None, 1, D_MODEL), lambda i: (l, 0, 0)),
        pl.BlockSpec(memory_space=pl.ANY),
        pl.BlockSpec(memory_space=pl.ANY),
    ]
    args = [xp, mod, w_in, w_out, conv_w, ln_g, ln_b, *kv_bufs]
    return pl.pallas_call(
        _ctx_mixer_kernel,
        grid=(batch // sb,),
        in_specs=in_specs,
        out_specs=[pl.BlockSpec((sb, seq_len, D_MODEL), lambda i: (i, 0, 0)), kv_spec, kv_spec],
        out_shape=[jax.ShapeDtypeStruct(xp.shape, F32), kv_shape, kv_shape],
        input_output_aliases={len(args) - 2: 1, len(args) - 1: 2},
        scratch_shapes=[
            pltpu.VMEM((ATT_WIDTH, m), BF16),
            pltpu.VMEM((N_HEADS, m, HEAD_DIM), BF16),
            pltpu.VMEM((ATT_WIDTH, m), BF16),
            pltpu.VMEM((ATT_WIDTH, m), F32),
            pltpu.VMEM((m, CONV_WIDTH), BF16),
            pltpu.VMEM((N_HEADS * sb, seq_len, seq_len), F32),
            pltpu.VMEM((N_HEADS * sb, seq_len, seq_len), BF16),
            pltpu.VMEM((N_HEADS * sb, 1, seq_len), F32),
        ],
        compiler_params=pltpu.CompilerParams(
            dimension_semantics=("arbitrary",), vmem_limit_bytes=VMEM_LIMIT_BYTES),
        name="ctx_mixer",
    )(*args)


def _window_start(r, rows):
    return min(max(r - WIN_ROWS // 2, 0), rows - WIN_ROWS)


def _chunk_key_rows(r0, chunk_rows, rows):
    lo = _window_start(r0, rows) // 2 * 2
    hi = -(-(_window_start(r0 + chunk_rows - 1, rows) + WIN_ROWS) // 2) * 2
    return lo, hi


def _lat_mixer_kernel(x_ref, mod_ref, win_ref, wout_ref, convw_ref, g_ref, b_ref,
                      ckT_ref, cvT_ref, tbl_ref, o_ref,
                      qT_scr, k_scr, vT_scr, attT_scr, conv_scr, sT_scr, pT_scr, inv_scr):
    b = pl.program_id(0)
    hd = pl.program_id(1)
    seq_len = x_ref.shape[0]
    rows = seq_len // GRID_W
    mod = mod_ref[pl.ds(1 + b, 1), :]

    @pl.when(hd == 0)
    def _():
        _project(x_ref[...], mod, win_ref, convw_ref, qT_scr, k_scr, vT_scr, conv_scr, seq_len)

    chan = _head_rows(hd)
    ckT = ckT_ref[...]
    ck = jnp.concatenate([ckT, jnp.zeros_like(ckT)], axis=0).T.astype(BF16)
    cvT = cvT_ref[...].astype(BF16)
    low_half = lax.broadcasted_iota(jnp.int32, (GRID_W, 2 * GRID_W), 1) < GRID_W

    def bias_block(r_pair, rk):
        inside = [_window_start(r, rows) <= rk < _window_start(r, rows) + WIN_ROWS for r in (r_pair, r_pair + 1)]
        d = rk - r_pair + WIN_ROWS - 1
        neg = jnp.full((GRID_W, 2 * GRID_W), NEG_INF, F32)
        if not any(inside):
            return neg
        blk = tbl_ref[d * GRID_W:(d + 1) * GRID_W, :]
        if all(inside):
            return blk
        return jnp.where(low_half, blk, neg) if inside[0] else jnp.where(low_half, neg, blk)

    chunk_rows = LAT_Q_CHUNK // GRID_W
    past = ck.shape[0]
    max_loc = sT_scr.shape[1] - past
    chunks = []
    for qc in range(seq_len // LAT_Q_CHUNK):
        r0 = qc * chunk_rows
        k_lo, k_hi = _chunk_key_rows(r0, chunk_rows, rows)
        chunks.append((qc, r0, k_lo, k_hi, slice(qc * LAT_Q_CHUNK, (qc + 1) * LAT_Q_CHUNK)))
    for qc, r0, k_lo, k_hi, qcols in chunks:
        qT = qT_scr[chan, qcols]
        bias = jnp.concatenate(
            [jnp.concatenate([bias_block(r0 + j, rk) for j in range(0, chunk_rows, 2)], axis=1)
             for rk in range(k_lo, k_hi)], axis=0)
        sT_scr[qc, 0:(k_hi - k_lo) * GRID_W] = _dot(k_scr[hd, k_lo * GRID_W:k_hi * GRID_W, :], qT) + bias
        sT_scr[qc, max_loc:] = _dot(ck, jnp.concatenate([qT, jnp.zeros_like(qT)], axis=0))
    for qc, r0, k_lo, k_hi, qcols in chunks:
        n_loc = (k_hi - k_lo) * GRID_W
        (pT_scr[qc, 0:n_loc], pT_scr[qc, max_loc:]), inv_scr[qc] = _softmax_keys(
            [sT_scr[qc, 0:n_loc], sT_scr[qc, max_loc:]])
    for qc, r0, k_lo, k_hi, qcols in chunks:
        n_loc = (k_hi - k_lo) * GRID_W
        attT_scr[chan, qcols] = (_dot(vT_scr[chan, k_lo * GRID_W:k_hi * GRID_W], pT_scr[qc, 0:n_loc])
                                 + _dot(cvT, pT_scr[qc, max_loc:])) * inv_scr[qc]

    @pl.when(hd == N_HEADS - 1)
    def _():
        _merge_and_norm(x_ref[...], mod, attT_scr, conv_scr, wout_ref, g_ref, b_ref, o_ref)


def _lat_mixer(l, xs, mod, w_in, w_out, conv_w, ln_g, ln_b, cache_kT, cache_vT, tbl):
    batch, seq_len, _ = xs.shape
    past = cache_kT.shape[-1]
    rows, chunk_rows, n_chunks = seq_len // GRID_W, LAT_Q_CHUNK // GRID_W, seq_len // LAT_Q_CHUNK
    max_loc = GRID_W * max(hi - lo for lo, hi in
                           (_chunk_key_rows(qc * chunk_rows, chunk_rows, rows) for qc in range(n_chunks)))
    cache_spec = pl.BlockSpec((None, None, None, HEAD_DIM, past), lambda b, h: (b, l, h, 0, 0))
    return pl.pallas_call(
        _lat_mixer_kernel,
        grid=(batch, N_HEADS),
        in_specs=[
            pl.BlockSpec((None, seq_len, D_MODEL), lambda b, h: (b, 0, 0)),
            pl.BlockSpec((None, 8, 6 * D_MODEL), lambda b, h: (l, 0, 0)),
            _weight_spec((None, D_MODEL, 3 * ATT_WIDTH + 3 * CONV_WIDTH), lambda b, h: (l, 0, 0)),
            _weight_spec((None, D_MODEL, D_MODEL), lambda b, h: (l, 0, 0)),
            pl.BlockSpec((None, 3, CONV_WIDTH), lambda b, h: (l, 0, 0)),
            pl.BlockSpec((None, 1, D_MODEL), lambda b, h: (l, 0, 0)),
            pl.BlockSpec((None, 1, D_MODEL), lambda b, h: (l, 0, 0)),
            cache_spec, cache_spec,
            pl.BlockSpec((None, None) + tbl.shape[2:], lambda b, h: (l, h, 0, 0)),
        ],
        out_specs=pl.BlockSpec((None, seq_len, D_MODEL), lambda b, h: (b, 0, 0)),
        out_shape=jax.ShapeDtypeStruct(xs.shape, F32),
        scratch_shapes=[
            pltpu.VMEM((ATT_WIDTH, seq_len), BF16),
            pltpu.VMEM((N_HEADS, seq_len, HEAD_DIM), BF16),
            pltpu.VMEM((ATT_WIDTH, seq_len), BF16),
            pltpu.VMEM((ATT_WIDTH, seq_len), F32),
            pltpu.VMEM((seq_len, CONV_WIDTH), BF16),
            pltpu.VMEM((n_chunks, max_loc + past, LAT_Q_CHUNK), F32),
            pltpu.VMEM((n_chunks, max_loc + past, LAT_Q_CHUNK), BF16),
            pltpu.VMEM((n_chunks, 1, LAT_Q_CHUNK), F32),
        ],
        compiler_params=pltpu.CompilerParams(
            dimension_semantics=("arbitrary", "arbitrary"), vmem_limit_bytes=VMEM_LIMIT_BYTES),
        name="lat_mixer",
    )(xs, mod, w_in, w_out, conv_w, ln_g, ln_b, cache_kT, cache_vT, tbl)


def _bias_table(rpb):
    depth, heads, n_dr, n_dc = rpb.shape
    n_blk = n_dr + 1
    lane0 = GRID_W - WIN_COLS
    rpb_pad = jnp.pad(rpb[..., ::-1], ((0, 0), (0, 0), (0, 16 - n_dr), (lane0, LANES - lane0 - n_dc)))
    rpb_pad = rpb_pad.reshape(depth * heads, 16, LANES)

    def body(r_ref, o_ref):
        cp = lax.broadcasted_iota(jnp.int32, (GRID_W, LANES), 0)
        lane = lax.broadcasted_iota(jnp.int32, (GRID_W, LANES), 1)
        low_half = lane < GRID_W
        c = jnp.where(low_half, lane, lane - GRID_W)
        col_start = jnp.clip(c - WIN_COLS // 2, 0, GRID_W - WIN_COLS)
        valid = (cp >= col_start) & (cp < col_start + WIN_COLS)
        def skewed(hd, dr, shift):
            row = jnp.broadcast_to(r_ref[hd, dr:dr + 1, :], (GRID_W, LANES))
            return pltpu.roll(row, shift, 1, stride=1, stride_axis=0)

        for hd in range(heads):
            for d in range(n_blk):
                lo = skewed(hd, d, LANES - GRID_W + 1) if d < n_dr else None
                hi = skewed(hd, d - 1, 1) if d >= 1 else None
                if lo is None:
                    blk = jnp.where(valid & ~low_half, hi, NEG_INF)
                elif hi is None:
                    blk = jnp.where(valid & low_half, lo, NEG_INF)
                else:
                    blk = jnp.where(valid, jnp.where(low_half, lo, hi), NEG_INF)
                o_ref[hd, d * GRID_W:(d + 1) * GRID_W, :] = blk

    tbl = pl.pallas_call(
        body,
        grid=(depth,),
        in_specs=[pl.BlockSpec((heads, 16, LANES), lambda i: (i, 0, 0))],
        out_specs=pl.BlockSpec((heads, n_blk * GRID_W, LANES), lambda i: (i, 0, 0)),
        out_shape=jax.ShapeDtypeStruct((depth * heads, n_blk * GRID_W, LANES), F32),
        compiler_params=pltpu.CompilerParams(dimension_semantics=("arbitrary",)),
        name="bias_table",
    )(rpb_pad)
    return tbl.reshape(depth, heads, n_blk * GRID_W, LANES)


def _route_t(logits):
    row = lax.broadcasted_iota(jnp.int32, logits.shape, 0)
    row_f = row.astype(F32)
    big = jnp.float32(LANES)

    def first_row(cond):
        return jnp.min(jnp.where(cond, row_f, big), axis=0, keepdims=True)

    gmask = (row >= N_EXPERTS) & (row < N_EXPERTS + N_GROUPS)
    gl = jnp.where(gmask, logits, NEG_INF)
    gexp = jnp.exp(gl - jnp.max(gl, axis=0, keepdims=True))
    gprob = gexp / jnp.sum(gexp, axis=0, keepdims=True)
    g_p = jnp.max(gprob, axis=0, keepdims=True)
    g_idx = first_row(gmask & (gprob == g_p)) - N_EXPERTS

    row_group = jnp.floor(row_f * (1.0 / EXPERTS_PER_GROUP))
    emask = (row < N_EXPERTS) & (row_group == g_idx)
    el = jnp.where(emask, logits, NEG_INF)
    eexp = jnp.exp(el - jnp.max(el, axis=0, keepdims=True))
    eprob = eexp / jnp.sum(eexp, axis=0, keepdims=True)
    p1 = jnp.max(eprob, axis=0, keepdims=True)
    i1 = first_row(emask & (eprob == p1))
    rest = emask & (row_f != i1)
    p2 = jnp.max(jnp.where(rest, eprob, -1.0), axis=0, keepdims=True)
    i2 = first_row(rest & (eprob == p2))
    denom = p1 + p2
    gate = (jnp.where(row_f == i1, g_p * p1 / denom, 0.0)
            + jnp.where(row_f == i2, g_p * p2 / denom, 0.0))
    return gate, g_idx


def _split3(v):
    hi = v.astype(BF16).astype(F32)
    mid = (v - hi).astype(BF16).astype(F32)
    return hi, mid, v - hi - mid


def _moe_route_tile(x_ref, mod_ref, wrT_ref, brT_ref, h_scr, rec_scr, recT_scr, o_ref, tier_smem):
    tm = x_ref.shape[0]
    seg = tm // mod_ref.shape[0]
    h = jnp.concatenate(
        [x_ref[s * seg:(s + 1) * seg, :] * (1.0 + mod_ref[s, :, 4 * D_MODEL:5 * D_MODEL])
         + mod_ref[s, :, 3 * D_MODEL:4 * D_MODEL] for s in range(tm // seg)], axis=0)
    h_hi = h.astype(BF16)
    h_lo = (h - h_hi.astype(F32)).astype(BF16)
    wr = wrT_ref[...]
    wr_hi = wr.astype(BF16)
    wr_lo = (wr - wr_hi.astype(F32)).astype(BF16)
    logits = (_dot_nt(wr_hi, h_hi) + _dot_nt(wr_hi, h_lo) + _dot_nt(wr_lo, h_hi))[0:ROUTE_ROWS, :]
    gate, g_idx = _route_t(logits + brT_ref[0:ROUTE_ROWS, :])

    row_f = lax.broadcasted_iota(jnp.int32, (ROUTE_ROWS, tm), 0).astype(F32)
    gate4 = jnp.concatenate(
        [jnp.sum(jnp.where(row_f == EXPERTS_PER_GROUP * g_idx + j, gate, 0.0), axis=0, keepdims=True)
         for j in range(EXPERTS_PER_GROUP)], axis=0)

    grp = lax.broadcasted_iota(jnp.int32, (8, tm), 0).astype(F32)
    onehot = jnp.where(grp == g_idx, 1.0, 0.0)
    ri = lax.broadcasted_iota(jnp.int32, (MOE_BLOCK, MOE_BLOCK), 0)
    ci = lax.broadcasted_iota(jnp.int32, (MOE_BLOCK, MOE_BLOCK), 1)
    earlier = jnp.where(ri < ci, 1.0, 0.0).astype(BF16)
    ranks = []
    most = jnp.zeros((8, 1), F32)
    for b in range(tm // MOE_BLOCK):
        oh_b = onehot[:, b * MOE_BLOCK:(b + 1) * MOE_BLOCK]
        ranks.append(_dot(oh_b.astype(BF16), earlier))
        most = jnp.maximum(most, jnp.sum(oh_b, axis=1, keepdims=True))
    rank = jnp.sum(onehot * jnp.concatenate(ranks, axis=1), axis=0, keepdims=True)

    recT = jnp.concatenate([*_split3(gate4), gate4, g_idx, rank,
                            jnp.zeros((LANES - REC_RANK - 1, tm), F32)], axis=0)
    rec = recT.T
    h_scr[:, 0:D_MODEL] = h_hi
    h_scr[:, D_MODEL:] = rec.astype(BF16)
    rec_scr[...] = rec
    recT_scr[...] = recT[REC_GROUP:REC_GROUP + 8, :]
    row8 = lax.broadcasted_iota(jnp.int32, (8, 1), 0)
    for gp in range(N_GROUPS):
        fullest = jnp.max(jnp.where(row8 == gp, most, 0.0))
        tier_smem[gp] = sum((fullest > slot).astype(jnp.int32) for slot in MOE_SLOTS)
    o_ref[...] = jnp.zeros_like(o_ref)


def _moe_kernel(x_ref, mod_ref, wrT_ref, brT_ref, win_ref, wout_ref, g_ref, b_ref, o_ref,
                h_scr, rec_scr, recT_scr, act_scr, tier_smem):
    g = pl.program_id(1)
    g_f = g.astype(F32)
    tm = x_ref.shape[0]
    n_blocks = tm // MOE_BLOCK

    @pl.when(g == 0)
    def _():
        _moe_route_tile(x_ref, mod_ref, wrT_ref, brT_ref, h_scr, rec_scr, recT_scr, o_ref, tier_smem)

    def experts(xb, gates, gate_lane, rows):
        for e in range(EXPERTS_PER_GROUP):
            hid = _dot(xb, win_ref[e].astype(BF16))
            act = _silu(hid[:, :D_EXPERT]) * hid[:, D_EXPERT:] * gates[:, gate_lane + e:gate_lane + e + 1]
            act_scr[rows, e * D_EXPERT:(e + 1) * D_EXPERT] = act.astype(BF16)
        return _dot(act_scr[rows, :], wout_ref[...].astype(BF16))

    def compact(slot):
        n_rows = n_blocks * slot
        slot_row = lax.broadcasted_iota(jnp.int32, (slot, MOE_BLOCK), 0).astype(F32)
        gathered = []
        for b in range(n_blocks):
            tok = slice(b * MOE_BLOCK, (b + 1) * MOE_BLOCK)
            pick = (recT_scr[0:1, tok] == g_f) & (recT_scr[1:2, tok] == slot_row)
            gathered.append(_dot(jnp.where(pick, 1.0, 0.0).astype(BF16), h_scr[tok, :]))
        xg = jnp.concatenate(gathered, axis=0)
        ge = xg[:, D_MODEL:]
        gates = ((ge + pltpu.roll(ge, LANES - EXPERTS_PER_GROUP, 1))
                 + pltpu.roll(ge, LANES - 2 * EXPERTS_PER_GROUP, 1))
        y = experts(xg[:, :D_MODEL].astype(BF16), gates, REC_SPLIT, slice(0, n_rows)).astype(BF16)
        slot_col = lax.broadcasted_iota(jnp.int32, (MOE_BLOCK, SCATTER_K), 1).astype(F32)
        for b in range(n_blocks):
            tok = slice(b * MOE_BLOCK, (b + 1) * MOE_BLOCK)
            first = min(b * slot, n_rows - SCATTER_K)
            place = ((rec_scr[tok, REC_GROUP:REC_GROUP + 1] == g_f)
                     & (rec_scr[tok, REC_RANK:REC_RANK + 1] + (b * slot - first) == slot_col))
            o_ref[tok, :] += _dot(jnp.where(place, 1.0, 0.0).astype(BF16), y[first:first + SCATTER_K, :])

    for tier, slot in enumerate(MOE_SLOTS):
        pl.when(tier_smem[g] == tier)(functools.partial(compact, slot))

    @pl.when(tier_smem[g] == len(MOE_SLOTS))
    def _():
        for b in range(n_blocks):
            tok = slice(b * MOE_BLOCK, (b + 1) * MOE_BLOCK)
            gates = jnp.where(rec_scr[tok, REC_GROUP:REC_GROUP + 1] == g_f, rec_scr[tok, :], 0.0)
            o_ref[tok, :] += experts(h_scr[tok, 0:D_MODEL], gates, REC_GATE, slice(0, MOE_BLOCK))

    @pl.when(g == N_GROUPS - 1)
    def _():
        seg = tm // mod_ref.shape[0]
        for s in range(tm // seg):
            tok = slice(s * seg, (s + 1) * seg)
            gf = mod_ref[s, :, 5 * D_MODEL:6 * D_MODEL]
            o_ref[tok, :] = _layer_norm(ALPHA * x_ref[tok, :] + gf * o_ref[tok, :], g_ref[...], b_ref[...])


def _moe(l, x, mod_rows, w_routerT, b_routerT, w_exp_in, w_exp_out, ln_g, ln_b):
    n = x.shape[0]
    tm = MOE_TOKENS
    n_seg = mod_rows.shape[2]
    n_blocks = tm // MOE_BLOCK
    assert tm % MOE_BLOCK == 0
    for slot in MOE_SLOTS:
        assert slot % 16 == 0 and slot <= SCATTER_K <= n_blocks * slot
    n_rows = n_blocks * max(MOE_SLOTS)
    return pl.pallas_call(
        _moe_kernel,
        grid=(n // tm, N_GROUPS),
        in_specs=[
            pl.BlockSpec((tm, D_MODEL), lambda i, g: (i, 0)),
            pl.BlockSpec((None, None, n_seg, 1, 6 * D_MODEL), lambda i, g: (l, i, 0, 0, 0)),
            pl.BlockSpec((None, LANES, D_MODEL), lambda i, g: (l, 0, 0)),
            pl.BlockSpec((None, LANES, 1), lambda i, g: (l, 0, 0)),
            pl.BlockSpec((None, EXPERTS_PER_GROUP, D_MODEL, 2 * D_EXPERT), lambda i, g: (l, g, 0, 0)),
            pl.BlockSpec((None, None, EXPERTS_PER_GROUP * D_EXPERT, D_MODEL), lambda i, g: (l, g, 0, 0)),
            pl.BlockSpec((None, 1, D_MODEL), lambda i, g: (l, 0, 0)),
            pl.BlockSpec((None, 1, D_MODEL), lambda i, g: (l, 0, 0)),
        ],
        out_specs=pl.BlockSpec((tm, D_MODEL), lambda i, g: (i, 0)),
        out_shape=jax.ShapeDtypeStruct(x.shape, F32),
        scratch_shapes=[
            pltpu.VMEM((tm, D_MODEL + GATE_COLS), BF16),
            pltpu.VMEM((tm, LANES), F32),
            pltpu.VMEM((8, tm), F32),
            pltpu.VMEM((max(n_rows, MOE_BLOCK), EXPERTS_PER_GROUP * D_EXPERT), BF16),
            pltpu.SMEM((N_GROUPS,), jnp.int32),
        ],
        compiler_params=pltpu.CompilerParams(
            dimension_semantics=("arbitrary", "arbitrary"), vmem_limit_bytes=VMEM_LIMIT_BYTES),
        name="hier_moe",
    )(x, mod_rows, w_routerT, b_routerT, w_exp_in, w_exp_out, ln_g, ln_b)


def kernel(x_prompt, x_sample, cache_k, cache_v, c, c_ctx, w_ada, b_ada, w_in, conv_w, rpb, w_out,
           ln1_g, ln1_b, w_router_group, b_router_group, w_router_expert, b_router_expert,
           w_expert_in, w_expert_out, ln2_g, ln2_b):
    batch, seq_len, _ = x_prompt.shape
    dec_batch, dec_seq, _ = x_sample.shape
    assert dec_batch + 1 <= 8 and MOE_TOKENS % dec_seq == 0
    assert (dec_batch * dec_seq) % MOE_TOKENS == 0 and (batch * seq_len) % MOE_TOKENS == 0

    cond = jnp.concatenate([c_ctx[None, :], c, jnp.zeros((8 - 1 - dec_batch, D_MODEL), F32)], axis=0)
    mod = _modulation(cond, w_ada, b_ada)
    n_ctx_tiles = batch * seq_len // MOE_TOKENS
    mod_rows_ctx = jnp.broadcast_to(mod[:, 0:1, None, None, :], (DEPTH, n_ctx_tiles, 1, 1, 6 * D_MODEL))
    mod_rows_lat = mod[:, 1:1 + dec_batch].reshape(DEPTH, -1, MOE_TOKENS // dec_seq, 1, 6 * D_MODEL)

    w_in_bf = w_in.astype(BF16)
    w_out_bf = w_out.astype(BF16)
    pad = jnp.zeros((DEPTH, LANES - N_EXPERTS - N_GROUPS, D_MODEL), F32)
    w_routerT = jnp.concatenate(
        [jnp.swapaxes(w_router_expert, 1, 2), jnp.swapaxes(w_router_group, 1, 2), pad], axis=1)
    b_routerT = jnp.concatenate([b_router_expert, b_router_group, pad[:, :, 0]], axis=-1)[:, :, None]
    tbl = _bias_table(rpb)
    cache_kT = jnp.swapaxes(cache_k, -1, -2)
    cache_vT = jnp.swapaxes(cache_v, -1, -2)
    ln1_g3, ln1_b3 = ln1_g[:, None, :], ln1_b[:, None, :]
    ln2_g3, ln2_b3 = ln2_g[:, None, :], ln2_b[:, None, :]

    w_exp_out = w_expert_out.reshape(DEPTH, N_GROUPS, EXPERTS_PER_GROUP * D_EXPERT, D_MODEL)

    xp, xs = x_prompt, x_sample
    kv_bufs = [jnp.zeros((batch, DEPTH, N_HEADS, HEAD_DIM, seq_len), F32)] * 2
    for l in range(DEPTH):
        moe = functools.partial(_moe, l, w_routerT=w_routerT, b_routerT=b_routerT, w_exp_in=w_expert_in,
                                w_exp_out=w_exp_out, ln_g=ln2_g3, ln_b=ln2_b3)
        xp, *kv_bufs = _ctx_mixer(l, xp, mod, w_in_bf, w_out_bf, conv_w, ln1_g3, ln1_b3, kv_bufs)
        xp = moe(xp.reshape(-1, D_MODEL), mod_rows_ctx).reshape(xp.shape)
        xs = _lat_mixer(l, xs, mod, w_in_bf, w_out_bf, conv_w, ln1_g3, ln1_b3, cache_kT, cache_vT, tbl)
        xs = moe(xs.reshape(-1, D_MODEL), mod_rows_lat).reshape(xs.shape)
    new_kT, new_vT = kv_bufs
    return (xp, xs, jnp.swapaxes(new_kT, -1, -2), jnp.swapaxes(new_vT, -1, -2))
```

```python
import functools

import jax
import jax.numpy as jnp
from jax import lax
from jax.experimental import pallas as pl
from jax.experimental.pallas import tpu as pltpu

D_MODEL = 1024
DEPTH = 4
GRID_W = 64
ATT_WIDTH = D_MODEL // 2
CONV_WIDTH = D_MODEL - ATT_WIDTH
HEAD_DIM = 64
N_HEADS = ATT_WIDTH // HEAD_DIM
WIN_ROWS = 8
WIN_COLS = 16
N_GROUPS = 4
EXPERTS_PER_GROUP = 4
N_EXPERTS = N_GROUPS * EXPERTS_PER_GROUP
D_EXPERT = D_MODEL // 4
ALPHA = (2 * DEPTH) ** 0.25
LN_EPS = 1e-5
NEG_INF = -1e30
QK_SCALE = HEAD_DIM ** -0.5

F32 = jnp.float32
BF16 = jnp.bfloat16

LANES = 128
VMEM_LIMIT_BYTES = 56 * 1024 * 1024

CTX_SEQ_PER_STEP = 2
LAT_Q_CHUNK = 128
MERGE_ROWS = 256
MOE_TOKENS = 1024
MOE_BLOCK = 512
MOE_SLOTS = (144, 192, 256)
SCATTER_K = 256
GATE_COLS = 128
ROUTE_ROWS = 24
REC_SPLIT, REC_GATE, REC_GROUP, REC_RANK = 0, 12, 16, 17
MOD_COLS = 2048


def _dot(a, b):
    return jnp.dot(a, b, preferred_element_type=F32)


def _dot_nt(a, b):
    return lax.dot_general(a, b, (((1,), (1,)), ((), ())), preferred_element_type=F32)


def _silu(x):
    return x * (1.0 / (1.0 + jnp.exp(-x)))


def _layer_norm(r, g, b):
    mu = jnp.mean(r, axis=-1, keepdims=True)
    d = r - mu
    var = jnp.mean(d * d, axis=-1, keepdims=True)
    return d * lax.rsqrt(var + LN_EPS) * g + b


def _mod_kernel(cond_ref, w_ref, b_ref, o_ref):
    s = _silu(cond_ref[...]).astype(BF16)
    o_ref[...] = _dot(s, w_ref[...].astype(BF16)) + b_ref[...]


def _modulation(cond, w_ada, b_ada):
    n_out = w_ada.shape[-1]
    return pl.pallas_call(
        _mod_kernel,
        grid=(DEPTH, n_out // MOD_COLS),
        in_specs=[
            pl.BlockSpec((8, D_MODEL), lambda l, j: (0, 0)),
            pl.BlockSpec((None, D_MODEL, MOD_COLS), lambda l, j: (l, 0, j)),
            pl.BlockSpec((None, 1, MOD_COLS), lambda l, j: (l, 0, j)),
        ],
        out_specs=pl.BlockSpec((None, 8, MOD_COLS), lambda l, j: (l, 0, j)),
        out_shape=jax.ShapeDtypeStruct((DEPTH, 8, n_out), F32),
        compiler_params=pltpu.CompilerParams(
            dimension_semantics=("arbitrary", "arbitrary"),
            vmem_limit_bytes=VMEM_LIMIT_BYTES),
        name="adaln_modulation",
    )(cond, w_ada, b_ada.reshape(DEPTH, 1, n_out))


def _project(x, mod, win_ref, convw_ref, qT_scr, k_scr, vT_scr, conv_scr, seq_len, kT_out=None, vT_out=None):
    m = x.shape[0]
    sa = mod[:, 0:D_MODEL]
    ca = mod[:, D_MODEL:2 * D_MODEL]
    h = (x * (1.0 + ca) + sa).astype(BF16)

    qT_scr[...] = (_dot(h, win_ref[:, 0:ATT_WIDTH]) * QK_SCALE).T.astype(BF16)
    zk = _dot(h, win_ref[:, ATT_WIDTH:2 * ATT_WIDTH])
    for hd in range(N_HEADS):
        k_scr[hd] = zk[:, hd * HEAD_DIM:(hd + 1) * HEAD_DIM].astype(BF16)
    zvT = _dot(h, win_ref[:, 2 * ATT_WIDTH:3 * ATT_WIDTH]).T
    vT_scr[...] = zvT.astype(BF16)
    if kT_out is not None:
        zkT = zk.T
        for out, zT in ((kT_out, zkT), (vT_out, zvT)):
            for s in range(m // seq_len):
                for hd in range(N_HEADS):
                    out[s, hd] = zT[hd * HEAD_DIM:(hd + 1) * HEAD_DIM, s * seq_len:(s + 1) * seq_len]

    c0 = 3 * ATT_WIDTH
    bg = _dot(h, win_ref[:, c0:c0 + CONV_WIDTH])
    cg = _dot(h, win_ref[:, c0 + CONV_WIDTH:c0 + 2 * CONV_WIDTH])
    u = _dot(h, win_ref[:, c0 + 2 * CONV_WIDTH:c0 + 3 * CONV_WIDTH])
    y = cg * u
    t = lax.broadcasted_iota(jnp.int32, (m, 1), 0) % seq_len
    y_prev = jnp.where(t == 0, 0.0, pltpu.roll(y, 1, 0))
    y_next = jnp.where(t == seq_len - 1, 0.0, pltpu.roll(y, m - 1, 0))
    cw = convw_ref[...]
    conv = cw[0:1, :] * y_prev + cw[1:2, :] * y + cw[2:3, :] * y_next
    conv_scr[...] = (bg * conv).astype(BF16)


def _merge_and_norm(x, mod, attT_scr, conv_scr, wout_ref, g_ref, b_ref, o_ref):
    ga = mod[:, 2 * D_MODEL:3 * D_MODEL]
    out = []
    for r in range(0, x.shape[0], MERGE_ROWS):
        rows = slice(r, r + MERGE_ROWS)
        att = attT_scr[:, rows].T.astype(BF16)
        mix = _dot(att, wout_ref[0:ATT_WIDTH, :]) + _dot(conv_scr[rows, :], wout_ref[ATT_WIDTH:, :])
        out.append(_layer_norm(ALPHA * x[rows, :] + ga * mix, g_ref[...], b_ref[...]))
    o_ref[...] = jnp.concatenate(out, axis=0).reshape(o_ref.shape)


def _softmax_keys(parts):
    mx = functools.reduce(jnp.maximum, [jnp.max(p, axis=0, keepdims=True) for p in parts])
    es = [jnp.exp(p - mx) for p in parts]
    inv = 1.0 / functools.reduce(jnp.add, [jnp.sum(e, axis=0, keepdims=True) for e in es])
    return [e.astype(BF16) for e in es], inv


def _head_rows(hd):
    return pl.ds(pl.multiple_of(hd * HEAD_DIM, HEAD_DIM), HEAD_DIM)


def _weight_spec(shape, index_map):
    return pl.BlockSpec(shape, index_map, pipeline_mode=pl.Buffered(1))


def _ctx_mixer_kernel(x_ref, mod_ref, win_ref, wout_ref, convw_ref, g_ref, b_ref, kT_in, vT_in,
                      o_ref, kT_ref, vT_ref, qT_scr, k_scr, vT_scr, attT_scr, conv_scr, sT_scr, pT_scr, inv_scr):
    del kT_in, vT_in
    sb, seq_len, _ = x_ref.shape
    x = x_ref[...].reshape(sb * seq_len, D_MODEL)
    mod = mod_ref[0:1, :]
    _project(x, mod, win_ref, convw_ref, qT_scr, k_scr, vT_scr, conv_scr, seq_len, kT_ref, vT_ref)

    pairs = [(hd, s) for hd in range(N_HEADS) for s in range(sb)]
    chan = lambda hd: slice(hd * HEAD_DIM, (hd + 1) * HEAD_DIM)
    tok = lambda s: slice(s * seq_len, (s + 1) * seq_len)
    for i, (hd, s) in enumerate(pairs):
        sT_scr[i] = _dot(k_scr[hd, tok(s), :], qT_scr[chan(hd), tok(s)])
    for i in range(len(pairs)):
        (pT_scr[i],), inv_scr[i] = _softmax_keys([sT_scr[i]])
    for i, (hd, s) in enumerate(pairs):
        attT_scr[chan(hd), tok(s)] = _dot(vT_scr[chan(hd), tok(s)], pT_scr[i]) * inv_scr[i]

    _merge_and_norm(x, mod, attT_scr, conv_scr, wout_ref, g_ref, b_ref, o_ref)


def _ctx_mixer(l, xp, mod, w_in, w_out, conv_w, ln_g, ln_b, kv_bufs):
    batch, seq_len, _ = xp.shape
    sb = CTX_SEQ_PER_STEP
    m = sb * seq_len
    kv_shape = jax.ShapeDtypeStruct((batch, DEPTH, N_HEADS, HEAD_DIM, seq_len), F32)
    kv_spec = pl.BlockSpec((sb, None, N_HEADS, HEAD_DIM, seq_len), lambda i: (i, l, 0, 0, 0))
    in_specs = [
        pl.BlockSpec((sb, seq_len, D_MODEL), lambda i: (i, 0, 0)),
        pl.BlockSpec((None, 8, 6 * D_MODEL), lambda i: (l, 0, 0)),
        _weight_spec((None, D_MODEL, 3 * ATT_WIDTH + 3 * CONV_WIDTH), lambda i: (l, 0, 0)),
        _weight_spec((None, D_MODEL, D_MODEL), lambda i: (l, 0, 0)),
        pl.BlockSpec((None, 3, CONV_WIDTH), lambda i: (l, 0, 0)),
        pl.BlockSpec((None, 1, D_MODEL), lambda i: (l, 0, 0)),
        pl.BlockSpec((None, 1, D_MODEL), lambda i: (l, 0, 0)),
        pl.BlockSpec(memory_space=pl.ANY),
        pl.BlockSpec(memory_space=pl.ANY),
    ]
    args = [xp, mod, w_in, w_out, conv_w, ln_g, ln_b, *kv_bufs]
    return pl.pallas_call(
        _ctx_mixer_kernel,
        grid=(batch // sb,),
        in_specs=in_specs,
        out_specs=[pl.BlockSpec((sb, seq_len, D_MODEL), lambda i: (i, 0, 0)), kv_spec, kv_spec],
        out_shape=[jax.ShapeDtypeStruct(xp.shape, F32), kv_shape, kv_shape],
        input_output_aliases={len(args) - 2: 1, len(args) - 1: 2},
        scratch_shapes=[
            pltpu.VMEM((ATT_WIDTH, m), BF16),
            pltpu.VMEM((N_HEADS, m, HEAD_DIM), BF16),
            pltpu.VMEM((ATT_WIDTH, m), BF16),
            pltpu.VMEM((ATT_WIDTH, m), F32),
            pltpu.VMEM((m, CONV_WIDTH), BF16),
            pltpu.VMEM((N_HEADS * sb, seq_len, seq_len), F32),
            pltpu.VMEM((N_HEADS * sb, seq_len, seq_len), BF16),
            pltpu.VMEM((N_HEADS * sb, 1, seq_len), F32),
        ],
        compiler_params=pltpu.CompilerParams(
            dimension_semantics=("arbitrary",), vmem_limit_bytes=VMEM_LIMIT_BYTES),
        name="ctx_mixer",
    )(*args)


def _window_start(r, rows):
    return min(max(r - WIN_ROWS // 2, 0), rows - WIN_ROWS)


def _chunk_key_rows(r0, chunk_rows, rows):
    lo = _window_start(r0, rows) // 2 * 2
    hi = -(-(_window_start(r0 + chunk_rows - 1, rows) + WIN_ROWS) // 2) * 2
    return lo, hi


def _lat_mixer_kernel(x_ref, mod_ref, win_ref, wout_ref, convw_ref, g_ref, b_ref,
                      ckT_ref, cvT_ref, tbl_ref, o_ref,
                      qT_scr, k_scr, vT_scr, attT_scr, conv_scr, sT_scr, pT_scr, inv_scr):
    b = pl.program_id(0)
    hd = pl.program_id(1)
    seq_len = x_ref.shape[0]
    rows = seq_len // GRID_W
    mod = mod_ref[pl.ds(1 + b, 1), :]

    @pl.when(hd == 0)
    def _():
        _project(x_ref[...], mod, win_ref, convw_ref, qT_scr, k_scr, vT_scr, conv_scr, seq_len)

    chan = _head_rows(hd)
    ckT = ckT_ref[...]
    ck = jnp.concatenate([ckT, jnp.zeros_like(ckT)], axis=0).T.astype(BF16)
    cvT = cvT_ref[...].astype(BF16)
    low_half = lax.broadcasted_iota(jnp.int32, (GRID_W, 2 * GRID_W), 1) < GRID_W

    def bias_block(r_pair, rk):
        inside = [_window_start(r, rows) <= rk < _window_start(r, rows) + WIN_ROWS for r in (r_pair, r_pair + 1)]
        d = rk - r_pair + WIN_ROWS - 1
        neg = jnp.full((GRID_W, 2 * GRID_W), NEG_INF, F32)
        if not any(inside):
            return neg
        blk = tbl_ref[d * GRID_W:(d + 1) * GRID_W, :]
        if all(inside):
            return blk
        return jnp.where(low_half, blk, neg) if inside[0] else jnp.where(low_half, neg, blk)

    chunk_rows = LAT_Q_CHUNK // GRID_W
    past = ck.shape[0]
    max_loc = sT_scr.shape[1] - past
    chunks = []
    for qc in range(seq_len // LAT_Q_CHUNK):
        r0 = qc * chunk_rows
        k_lo, k_hi = _chunk_key_rows(r0, chunk_rows, rows)
        chunks.append((qc, r0, k_lo, k_hi, slice(qc * LAT_Q_CHUNK, (qc + 1) * LAT_Q_CHUNK)))
    for qc, r0, k_lo, k_hi, qcols in chunks:
        qT = qT_scr[chan, qcols]
        bias = jnp.concatenate(
            [jnp.concatenate([bias_block(r0 + j, rk) for j in range(0, chunk_rows, 2)], axis=1)
             for rk in range(k_lo, k_hi)], axis=0)
        sT_scr[qc, 0:(k_hi - k_lo) * GRID_W] = _dot(k_scr[hd, k_lo * GRID_W:k_hi * GRID_W, :], qT) + bias
        sT_scr[qc, max_loc:] = _dot(ck, jnp.concatenate([qT, jnp.zeros_like(qT)], axis=0))
    for qc, r0, k_lo, k_hi, qcols in chunks:
        n_loc = (k_hi - k_lo) * GRID_W
        (pT_scr[qc, 0:n_loc], pT_scr[qc, max_loc:]), inv_scr[qc] = _softmax_keys(
            [sT_scr[qc, 0:n_loc], sT_scr[qc, max_loc:]])
    for qc, r0, k_lo, k_hi, qcols in chunks:
        n_loc = (k_hi - k_lo) * GRID_W
        attT_scr[chan, qcols] = (_dot(vT_scr[chan, k_lo * GRID_W:k_hi * GRID_W], pT_scr[qc, 0:n_loc])
                                 + _dot(cvT, pT_scr[qc, max_loc:])) * inv_scr[qc]

    @pl.when(hd == N_HEADS - 1)
    def _():
        _merge_and_norm(x_ref[...], mod, attT_scr, conv_scr, wout_ref, g_ref, b_ref, o_ref)


def _lat_mixer(l, xs, mod, w_in, w_out, conv_w, ln_g, ln_b, cache_kT, cache_vT, tbl):
    batch, seq_len, _ = xs.shape
    past = cache_kT.shape[-1]
    rows, chunk_rows, n_chunks = seq_len // GRID_W, LAT_Q_CHUNK // GRID_W, seq_len // LAT_Q_CHUNK
    max_loc = GRID_W * max(hi - lo for lo, hi in
                           (_chunk_key_rows(qc * chunk_rows, chunk_rows, rows) for qc in range(n_chunks)))
    cache_spec = pl.BlockSpec((None, None, None, HEAD_DIM, past), lambda b, h: (b, l, h, 0, 0))
    return pl.pallas_call(
        _lat_mixer_kernel,
        grid=(batch, N_HEADS),
        in_specs=[
            pl.BlockSpec((None, seq_len, D_MODEL), lambda b, h: (b, 0, 0)),
            pl.BlockSpec((None, 8, 6 * D_MODEL), lambda b, h: (l, 0, 0)),
            _weight_spec((None, D_MODEL, 3 * ATT_WIDTH + 3 * CONV_WIDTH), lambda b, h: (l, 0, 0)),
            _weight_spec((None, D_MODEL, D_MODEL), lambda b, h: (l, 0, 0)),
            pl.BlockSpec((None, 3, CONV_WIDTH), lambda b, h: (l, 0, 0)),
            pl.BlockSpec((None, 1, D_MODEL), lambda b, h: (l, 0, 0)),
            pl.BlockSpec((None, 1, D_MODEL), lambda b, h: (l, 0, 0)),
            cache_spec, cache_spec,
            pl.BlockSpec((None, None) + tbl.shape[2:], lambda b, h: (l, h, 0, 0)),
        ],
        out_specs=pl.BlockSpec((None, seq_len, D_MODEL), lambda b, h: (b, 0, 0)),
        out_shape=jax.ShapeDtypeStruct(xs.shape, F32),
        scratch_shapes=[
            pltpu.VMEM((ATT_WIDTH, seq_len), BF16),
            pltpu.VMEM((N_HEADS, seq_len, HEAD_DIM), BF16),
            pltpu.VMEM((ATT_WIDTH, seq_len), BF16),
            pltpu.VMEM((ATT_WIDTH, seq_len), F32),
            pltpu.VMEM((seq_len, CONV_WIDTH), BF16),
            pltpu.VMEM((n_chunks, max_loc + past, LAT_Q_CHUNK), F32),
            pltpu.VMEM((n_chunks, max_loc + past, LAT_Q_CHUNK), BF16),
            pltpu.VMEM((n_chunks, 1, LAT_Q_CHUNK), F32),
        ],
        compiler_params=pltpu.CompilerParams(
            dimension_semantics=("arbitrary", "arbitrary"), vmem_limit_bytes=VMEM_LIMIT_BYTES),
        name="lat_mixer",
    )(xs, mod, w_in, w_out, conv_w, ln_g, ln_b, cache_kT, cache_vT, tbl)


def _bias_table(rpb):
    depth, heads, n_dr, n_dc = rpb.shape
    n_blk = n_dr + 1
    lane0 = GRID_W - WIN_COLS
    rpb_pad = jnp.pad(rpb[..., ::-1], ((0, 0), (0, 0), (0, 16 - n_dr), (lane0, LANES - lane0 - n_dc)))
    rpb_pad = rpb_pad.reshape(depth * heads, 16, LANES)

    def body(r_ref, o_ref):
        cp = lax.broadcasted_iota(jnp.int32, (GRID_W, LANES), 0)
        lane = lax.broadcasted_iota(jnp.int32, (GRID_W, LANES), 1)
        low_half = lane < GRID_W
        c = jnp.where(low_half, lane, lane - GRID_W)
        col_start = jnp.clip(c - WIN_COLS // 2, 0, GRID_W - WIN_COLS)
        valid = (cp >= col_start) & (cp < col_start + WIN_COLS)
        def skewed(hd, dr, shift):
            row = jnp.broadcast_to(r_ref[hd, dr:dr + 1, :], (GRID_W, LANES))
            return pltpu.roll(row, shift, 1, stride=1, stride_axis=0)

        for hd in range(heads):
            for d in range(n_blk):
                lo = skewed(hd, d, LANES - GRID_W + 1) if d < n_dr else None
                hi = skewed(hd, d - 1, 1) if d >= 1 else None
                if lo is None:
                    blk = jnp.where(valid & ~low_half, hi, NEG_INF)
                elif hi is None:
                    blk = jnp.where(valid & low_half, lo, NEG_INF)
                else:
                    blk = jnp.where(valid, jnp.where(low_half, lo, hi), NEG_INF)
                o_ref[hd, d * GRID_W:(d + 1) * GRID_W, :] = blk

    tbl = pl.pallas_call(
        body,
        grid=(depth,),
        in_specs=[pl.BlockSpec((heads, 16, LANES), lambda i: (i, 0, 0))],
        out_specs=pl.BlockSpec((heads, n_blk * GRID_W, LANES), lambda i: (i, 0, 0)),
        out_shape=jax.ShapeDtypeStruct((depth * heads, n_blk * GRID_W, LANES), F32),
        compiler_params=pltpu.CompilerParams(dimension_semantics=("arbitrary",)),
        name="bias_table",
    )(rpb_pad)
    return tbl.reshape(depth, heads, n_blk * GRID_W, LANES)


def _route_t(logits):
    row = lax.broadcasted_iota(jnp.int32, logits.shape, 0)
    row_f = row.astype(F32)
    big = jnp.float32(LANES)

    def first_row(cond):
        return jnp.min(jnp.where(cond, row_f, big), axis=0, keepdims=True)

    gmask = (row >= N_EXPERTS) & (row < N_EXPERTS + N_GROUPS)
    gl = jnp.where(gmask, logits, NEG_INF)
    gexp = jnp.exp(gl - jnp.max(gl, axis=0, keepdims=True))
    gprob = gexp / jnp.sum(gexp, axis=0, keepdims=True)
    g_p = jnp.max(gprob, axis=0, keepdims=True)
    g_idx = first_row(gmask & (gprob == g_p)) - N_EXPERTS

    row_group = jnp.floor(row_f * (1.0 / EXPERTS_PER_GROUP))
    emask = (row < N_EXPERTS) & (row_group == g_idx)
    el = jnp.where(emask, logits, NEG_INF)
    eexp = jnp.exp(el - jnp.max(el, axis=0, keepdims=True))
    eprob = eexp / jnp.sum(eexp, axis=0, keepdims=True)
    p1 = jnp.max(eprob, axis=0, keepdims=True)
    i1 = first_row(emask & (eprob == p1))
    rest = emask & (row_f != i1)
    p2 = jnp.max(jnp.where(rest, eprob, -1.0), axis=0, keepdims=True)
    i2 = first_row(rest & (eprob == p2))
    denom = p1 + p2
    gate = (jnp.where(row_f == i1, g_p * p1 / denom, 0.0)
            + jnp.where(row_f == i2, g_p * p2 / denom, 0.0))
    return gate, g_idx


def _split3(v):
    hi = v.astype(BF16).astype(F32)
    mid = (v - hi).astype(BF16).astype(F32)
    return hi, mid, v - hi - mid


def _moe_route_tile(x_ref, mod_ref, wrT_ref, brT_ref, h_scr, rec_scr, recT_scr, o_ref, tier_smem):
    tm = x_ref.shape[0]
    seg = tm // mod_ref.shape[0]
    h = jnp.concatenate(
        [x_ref[s * seg:(s + 1) * seg, :] * (1.0 + mod_ref[s, :, 4 * D_MODEL:5 * D_MODEL])
         + mod_ref[s, :, 3 * D_MODEL:4 * D_MODEL] for s in range(tm // seg)], axis=0)
    h_hi = h.astype(BF16)
    h_lo = (h - h_hi.astype(F32)).astype(BF16)
    wr = wrT_ref[...]
    wr_hi = wr.astype(BF16)
    wr_lo = (wr - wr_hi.astype(F32)).astype(BF16)
    logits = (_dot_nt(wr_hi, h_hi) + _dot_nt(wr_hi, h_lo) + _dot_nt(wr_lo, h_hi))[0:ROUTE_ROWS, :]
    gate, g_idx = _route_t(logits + brT_ref[0:ROUTE_ROWS, :])

    row_f = lax.broadcasted_iota(jnp.int32, (ROUTE_ROWS, tm), 0).astype(F32)
    gate4 = jnp.concatenate(
        [jnp.sum(jnp.where(row_f == EXPERTS_PER_GROUP * g_idx + j, gate, 0.0), axis=0, keepdims=True)
         for j in range(EXPERTS_PER_GROUP)], axis=0)

    grp = lax.broadcasted_iota(jnp.int32, (8, tm), 0).astype(F32)
    onehot = jnp.where(grp == g_idx, 1.0, 0.0)
    ri = lax.broadcasted_iota(jnp.int32, (MOE_BLOCK, MOE_BLOCK), 0)
    ci = lax.broadcasted_iota(jnp.int32, (MOE_BLOCK, MOE_BLOCK), 1)
    earlier = jnp.where(ri < ci, 1.0, 0.0).astype(BF16)
    ranks = []
    most = jnp.zeros((8, 1), F32)
    for b in range(tm // MOE_BLOCK):
        oh_b = onehot[:, b * MOE_BLOCK:(b + 1) * MOE_BLOCK]
        ranks.append(_dot(oh_b.astype(BF16), earlier))
        most = jnp.maximum(most, jnp.sum(oh_b, axis=1, keepdims=True))
    rank = jnp.sum(onehot * jnp.concatenate(ranks, axis=1), axis=0, keepdims=True)

    recT = jnp.concatenate([*_split3(gate4), gate4, g_idx, rank,
                            jnp.zeros((LANES - REC_RANK - 1, tm), F32)], axis=0)
    rec = recT.T
    h_scr[:, 0:D_MODEL] = h_hi
    h_scr[:, D_MODEL:] = rec.astype(BF16)
    rec_scr[...] = rec
    recT_scr[...] = recT[REC_GROUP:REC_GROUP + 8, :]
    row8 = lax.broadcasted_iota(jnp.int32, (8, 1), 0)
    for gp in range(N_GROUPS):
        fullest = jnp.max(jnp.where(row8 == gp, most, 0.0))
        tier_smem[gp] = sum((fullest > slot).astype(jnp.int32) for slot in MOE_SLOTS)
    o_ref[...] = jnp.zeros_like(o_ref)


def _moe_kernel(x_ref, mod_ref, wrT_ref, brT_ref, win_ref, wout_ref, g_ref, b_ref, o_ref,
                h_scr, rec_scr, recT_scr, act_scr, tier_smem):
    g = pl.program_id(1)
    g_f = g.astype(F32)
    tm = x_ref.shape[0]
    n_blocks = tm // MOE_BLOCK

    @pl.when(g == 0)
    def _():
        _moe_route_tile(x_ref, mod_ref, wrT_ref, brT_ref, h_scr, rec_scr, recT_scr, o_ref, tier_smem)

    def experts(xb, gates, gate_lane, rows):
        for e in range(EXPERTS_PER_GROUP):
            hid = _dot(xb, win_ref[e].astype(BF16))
            act = _silu(hid[:, :D_EXPERT]) * hid[:, D_EXPERT:] * gates[:, gate_lane + e:gate_lane + e + 1]
            act_scr[rows, e * D_EXPERT:(e + 1) * D_EXPERT] = act.astype(BF16)
        return _dot(act_scr[rows, :], wout_ref[...].astype(BF16))

    def compact(slot):
        n_rows = n_blocks * slot
        slot_row = lax.broadcasted_iota(jnp.int32, (slot, MOE_BLOCK), 0).astype(F32)
        gathered = []
        for b in range(n_blocks):
            tok = slice(b * MOE_BLOCK, (b + 1) * MOE_BLOCK)
            pick = (recT_scr[0:1, tok] == g_f) & (recT_scr[1:2, tok] == slot_row)
            gathered.append(_dot(jnp.where(pick, 1.0, 0.0).astype(BF16), h_scr[tok, :]))
        xg = jnp.concatenate(gathered, axis=0)
        ge = xg[:, D_MODEL:]
        gates = ((ge + pltpu.roll(ge, LANES - EXPERTS_PER_GROUP, 1))
                 + pltpu.roll(ge, LANES - 2 * EXPERTS_PER_GROUP, 1))
        y = experts(xg[:, :D_MODEL].astype(BF16), gates, REC_SPLIT, slice(0, n_rows)).astype(BF16)
        slot_col = lax.broadcasted_iota(jnp.int32, (MOE_BLOCK, SCATTER_K), 1).astype(F32)
        for b in range(n_blocks):
            tok = slice(b * MOE_BLOCK, (b + 1) * MOE_BLOCK)
            first = min(b * slot, n_rows - SCATTER_K)
            place = ((rec_scr[tok, REC_GROUP:REC_GROUP + 1] == g_f)
                     & (rec_scr[tok, REC_RANK:REC_RANK + 1] + (b * slot - first) == slot_col))
            o_ref[tok, :] += _dot(jnp.where(place, 1.0, 0.0).astype(BF16), y[first:first + SCATTER_K, :])

    for tier, slot in enumerate(MOE_SLOTS):
        pl.when(tier_smem[g] == tier)(functools.partial(compact, slot))

    @pl.when(tier_smem[g] == len(MOE_SLOTS))
    def _():
        for b in range(n_blocks):
            tok = slice(b * MOE_BLOCK, (b + 1) * MOE_BLOCK)
            gates = jnp.where(rec_scr[tok, REC_GROUP:REC_GROUP + 1] == g_f, rec_scr[tok, :], 0.0)
            o_ref[tok, :] += experts(h_scr[tok, 0:D_MODEL], gates, REC_GATE, slice(0, MOE_BLOCK))

    @pl.when(g == N_GROUPS - 1)
    def _():
        seg = tm // mod_ref.shape[0]
        for s in range(tm // seg):
            tok = slice(s * seg, (s + 1) * seg)
            gf = mod_ref[s, :, 5 * D_MODEL:6 * D_MODEL]
            o_ref[tok, :] = _layer_norm(ALPHA * x_ref[tok, :] + gf * o_ref[tok, :], g_ref[...], b_ref[...])


def _moe(l, x, mod_rows, w_routerT, b_routerT, w_exp_in, w_exp_out, ln_g, ln_b):
    n = x.shape[0]
    tm = MOE_TOKENS
    n_seg = mod_rows.shape[2]
    n_blocks = tm // MOE_BLOCK
    assert tm % MOE_BLOCK == 0
    for slot in MOE_SLOTS:
        assert slot % 16 == 0 and slot <= SCATTER_K <= n_blocks * slot
    n_rows = n_blocks * max(MOE_SLOTS)
    return pl.pallas_call(
        _moe_kernel,
        grid=(n // tm, N_GROUPS),
        in_specs=[
            pl.BlockSpec((tm, D_MODEL), lambda i, g: (i, 0)),
            pl.BlockSpec((None, None, n_seg, 1, 6 * D_MODEL), lambda i, g: (l, i, 0, 0, 0)),
            pl.BlockSpec((None, LANES, D_MODEL), lambda i, g: (l, 0, 0)),
            pl.BlockSpec((None, LANES, 1), lambda i, g: (l, 0, 0)),
            pl.BlockSpec((None, EXPERTS_PER_GROUP, D_MODEL, 2 * D_EXPERT), lambda i, g: (l, g, 0, 0)),
            pl.BlockSpec((None, None, EXPERTS_PER_GROUP * D_EXPERT, D_MODEL), lambda i, g: (l, g, 0, 0)),
            pl.BlockSpec((None, 1, D_MODEL), lambda i, g: (l, 0, 0)),
            pl.BlockSpec((None, 1, D_MODEL), lambda i, g: (l, 0, 0)),
        ],
        out_specs=pl.BlockSpec((tm, D_MODEL), lambda i, g: (i, 0)),
        out_shape=jax.ShapeDtypeStruct(x.shape, F32),
        scratch_shapes=[
            pltpu.VMEM((tm, D_MODEL + GATE_COLS), BF16),
            pltpu.VMEM((tm, LANES), F32),
            pltpu.VMEM((8, tm), F32),
            pltpu.VMEM((max(n_rows, MOE_BLOCK), EXPERTS_PER_GROUP * D_EXPERT), BF16),
            pltpu.SMEM((N_GROUPS,), jnp.int32),
        ],
        compiler_params=pltpu.CompilerParams(
            dimension_semantics=("arbitrary", "arbitrary"), vmem_limit_bytes=VMEM_LIMIT_BYTES),
        name="hier_moe",
    )(x, mod_rows, w_routerT, b_routerT, w_exp_in, w_exp_out, ln_g, ln_b)


def kernel(x_prompt, x_sample, cache_k, cache_v, c, c_ctx, w_ada, b_ada, w_in, conv_w, rpb, w_out,
           ln1_g, ln1_b, w_router_group, b_router_group, w_router_expert, b_router_expert,
           w_expert_in, w_expert_out, ln2_g, ln2_b):
    batch, seq_len, _ = x_prompt.shape
    dec_batch, dec_seq, _ = x_sample.shape
    assert dec_batch + 1 <= 8 and MOE_TOKENS % dec_seq == 0
    assert (dec_batch * dec_seq) % MOE_TOKENS == 0 and (batch * seq_len) % MOE_TOKENS == 0

    cond = jnp.concatenate([c_ctx[None, :], c, jnp.zeros((8 - 1 - dec_batch, D_MODEL), F32)], axis=0)
    mod = _modulation(cond, w_ada, b_ada)
    n_ctx_tiles = batch * seq_len // MOE_TOKENS
    mod_rows_ctx = jnp.broadcast_to(mod[:, 0:1, None, None, :], (DEPTH, n_ctx_tiles, 1, 1, 6 * D_MODEL))
    mod_rows_lat = mod[:, 1:1 + dec_batch].reshape(DEPTH, -1, MOE_TOKENS // dec_seq, 1, 6 * D_MODEL)

    w_in_bf = w_in.astype(BF16)
    w_out_bf = w_out.astype(BF16)
    pad = jnp.zeros((DEPTH, LANES - N_EXPERTS - N_GROUPS, D_MODEL), F32)
    w_routerT = jnp.concatenate(
        [jnp.swapaxes(w_router_expert, 1, 2), jnp.swapaxes(w_router_group, 1, 2), pad], axis=1)
    b_routerT = jnp.concatenate([b_router_expert, b_router_group, pad[:, :, 0]], axis=-1)[:, :, None]
    tbl = _bias_table(rpb)
    cache_kT = jnp.swapaxes(cache_k, -1, -2)
    cache_vT = jnp.swapaxes(cache_v, -1, -2)
    ln1_g3, ln1_b3 = ln1_g[:, None, :], ln1_b[:, None, :]
    ln2_g3, ln2_b3 = ln2_g[:, None, :], ln2_b[:, None, :]

    w_exp_out = w_expert_out.reshape(DEPTH, N_GROUPS, EXPERTS_PER_GROUP * D_EXPERT, D_MODEL)

    xp, xs = x_prompt, x_sample
    kv_bufs = [jnp.zeros((batch, DEPTH, N_HEADS, HEAD_DIM, seq_len), F32)] * 2
    for l in range(DEPTH):
        moe = functools.partial(_moe, l, w_routerT=w_routerT, b_routerT=b_routerT, w_exp_in=w_expert_in,
                                w_exp_out=w_exp_out, ln_g=ln2_g3, ln_b=ln2_b3)
        xp, *kv_bufs = _ctx_mixer(l, xp, mod, w_in_bf, w_out_bf, conv_w, ln1_g3, ln1_b3, kv_bufs)
        xp = moe(xp.reshape(-1, D_MODEL), mod_rows_ctx).reshape(xp.shape)
        xs = _lat_mixer(l, xs, mod, w_in_bf, w_out_bf, conv_w, ln1_g3, ln1_b3, cache_kT, cache_vT, tbl)
        xs = moe(xs.reshape(-1, D_MODEL), mod_rows_lat).reshape(xs.shape)
    new_kT, new_vT = kv_bufs
    return (xp, xs, jnp.swapaxes(new_kT, -1, -2), jnp.swapaxes(new_vT, -1, -2))
```

```python
import functools

import jax
import jax.numpy as jnp
from jax import lax
from jax.experimental import pallas as pl
from jax.experimental.pallas import tpu as pltpu

D_MODEL = 1024
DEPTH = 4
GRID_W = 64
ATT_WIDTH = D_MODEL // 2
CONV_WIDTH = D_MODEL - ATT_WIDTH
HEAD_DIM = 64
N_HEADS = ATT_WIDTH // HEAD_DIM
WIN_ROWS = 8
WIN_COLS = 16
N_GROUPS = 4
EXPERTS_PER_GROUP = 4
N_EXPERTS = N_GROUPS * EXPERTS_PER_GROUP
D_EXPERT = D_MODEL // 4
ALPHA = (2 * DEPTH) ** 0.25
LN_EPS = 1e-5
NEG_INF = -1e30
QK_SCALE = HEAD_DIM ** -0.5

F32 = jnp.float32
BF16 = jnp.bfloat16

LANES = 128
SUBLANES = 8
BF16_ROWS = 16
VMEM_LIMIT_BYTES = 56 * 1024 * 1024

CTX_SEQ_PER_STEP = 2
LAT_Q_CHUNK = 128
MERGE_ROWS = 256
MOE_TOKENS = 1024
MOE_BLOCK = 512
MOE_SLOTS = (144, 192, 256)
SCATTER_K = 256
GATE_COLS = 128
ROUTE_ROWS = 24
REC_SPLIT, REC_GATE, REC_GROUP, REC_RANK = 0, 12, 16, 17
MOD_COLS = 2048


def _dot(a, b):
    return jnp.dot(a, b, preferred_element_type=F32)


def _dot_nt(a, b):
    return lax.dot_general(a, b, (((1,), (1,)), ((), ())), preferred_element_type=F32)


def _silu(x):
    return x * (1.0 / (1.0 + jnp.exp(-x)))


def _layer_norm(r, g, b):
    mu = jnp.mean(r, axis=-1, keepdims=True)
    d = r - mu
    var = jnp.mean(d * d, axis=-1, keepdims=True)
    return d * lax.rsqrt(var + LN_EPS) * g + b


def _mod_kernel(cond_ref, w_ref, b_ref, o_ref):
    s = _silu(cond_ref[...]).astype(BF16)
    o_ref[...] = _dot(s, w_ref[...].astype(BF16)) + b_ref[...]


def _modulation(cond, w_ada, b_ada):
    n_out = w_ada.shape[-1]
    return pl.pallas_call(
        _mod_kernel,
        grid=(DEPTH, n_out // MOD_COLS),
        in_specs=[
            pl.BlockSpec((SUBLANES, D_MODEL), lambda l, j: (0, 0)),
            pl.BlockSpec((None, D_MODEL, MOD_COLS), lambda l, j: (l, 0, j)),
            pl.BlockSpec((None, 1, MOD_COLS), lambda l, j: (l, 0, j)),
        ],
        out_specs=pl.BlockSpec((None, SUBLANES, MOD_COLS), lambda l, j: (l, 0, j)),
        out_shape=jax.ShapeDtypeStruct((DEPTH, SUBLANES, n_out), F32),
        compiler_params=pltpu.CompilerParams(
            dimension_semantics=("arbitrary", "arbitrary"),
            vmem_limit_bytes=VMEM_LIMIT_BYTES),
        name="adaln_modulation",
    )(cond, w_ada, b_ada.reshape(DEPTH, 1, n_out))


def _project(x, mod, win_ref, convw_ref, qT_scr, k_scr, vT_scr, conv_scr, seq_len, kT_out=None, vT_out=None):
    m = x.shape[0]
    sa = mod[:, 0:D_MODEL]
    ca = mod[:, D_MODEL:2 * D_MODEL]
    h = (x * (1.0 + ca) + sa).astype(BF16)

    qT_scr[...] = (_dot(h, win_ref[:, 0:ATT_WIDTH]) * QK_SCALE).T.astype(BF16)
    zk = _dot(h, win_ref[:, ATT_WIDTH:2 * ATT_WIDTH])
    for hd in range(N_HEADS):
        k_scr[hd] = zk[:, hd * HEAD_DIM:(hd + 1) * HEAD_DIM].astype(BF16)
    zvT = _dot(h, win_ref[:, 2 * ATT_WIDTH:3 * ATT_WIDTH]).T
    vT_scr[...] = zvT.astype(BF16)
    if kT_out is not None:
        zkT = zk.T
        for out, zT in ((kT_out, zkT), (vT_out, zvT)):
            for s in range(m // seq_len):
                for hd in range(N_HEADS):
                    out[s, hd] = zT[hd * HEAD_DIM:(hd + 1) * HEAD_DIM, s * seq_len:(s + 1) * seq_len]

    c0 = 3 * ATT_WIDTH
    bg = _dot(h, win_ref[:, c0:c0 + CONV_WIDTH])
    cg = _dot(h, win_ref[:, c0 + CONV_WIDTH:c0 + 2 * CONV_WIDTH])
    u = _dot(h, win_ref[:, c0 + 2 * CONV_WIDTH:c0 + 3 * CONV_WIDTH])
    y = cg * u
    t = lax.broadcasted_iota(jnp.int32, (m, 1), 0) % seq_len
    y_prev = jnp.where(t == 0, 0.0, pltpu.roll(y, 1, 0))
    y_next = jnp.where(t == seq_len - 1, 0.0, pltpu.roll(y, m - 1, 0))
    cw = convw_ref[...]
    conv = cw[0:1, :] * y_prev + cw[1:2, :] * y + cw[2:3, :] * y_next
    conv_scr[...] = (bg * conv).astype(BF16)


def _merge_and_norm(x, mod, attT_scr, conv_scr, wout_ref, g_ref, b_ref, o_ref):
    ga = mod[:, 2 * D_MODEL:3 * D_MODEL]
    out = []
    for r in range(0, x.shape[0], MERGE_ROWS):
        rows = slice(r, r + MERGE_ROWS)
        att = attT_scr[:, rows].T.astype(BF16)
        mix = _dot(att, wout_ref[0:ATT_WIDTH, :]) + _dot(conv_scr[rows, :], wout_ref[ATT_WIDTH:, :])
        out.append(_layer_norm(ALPHA * x[rows, :] + ga * mix, g_ref[...], b_ref[...]))
    o_ref[...] = jnp.concatenate(out, axis=0).reshape(o_ref.shape)


def _softmax_keys(parts):
    mx = functools.reduce(jnp.maximum, [jnp.max(p, axis=0, keepdims=True) for p in parts])
    es = [jnp.exp(p - mx) for p in parts]
    inv = 1.0 / functools.reduce(jnp.add, [jnp.sum(e, axis=0, keepdims=True) for e in es])
    return [e.astype(BF16) for e in es], inv


def _head_rows(hd):
    return pl.ds(pl.multiple_of(hd * HEAD_DIM, HEAD_DIM), HEAD_DIM)


def _weight_spec(shape, index_map):
    return pl.BlockSpec(shape, index_map, pipeline_mode=pl.Buffered(1))


def _ctx_mixer_kernel(x_ref, mod_ref, win_ref, wout_ref, convw_ref, g_ref, b_ref, kT_in, vT_in,
                      o_ref, kT_ref, vT_ref, qT_scr, k_scr, vT_scr, attT_scr, conv_scr, sT_scr, pT_scr, inv_scr):
    del kT_in, vT_in
    sb, seq_len, _ = x_ref.shape
    x = x_ref[...].reshape(sb * seq_len, D_MODEL)
    mod = mod_ref[0:1, :]
    _project(x, mod, win_ref, convw_ref, qT_scr, k_scr, vT_scr, conv_scr, seq_len, kT_ref, vT_ref)

    pairs = [(hd, s) for hd in range(N_HEADS) for s in range(sb)]
    chan = lambda hd: slice(hd * HEAD_DIM, (hd + 1) * HEAD_DIM)
    tok = lambda s: slice(s * seq_len, (s + 1) * seq_len)
    for i, (hd, s) in enumerate(pairs):
        sT_scr[i] = _dot(k_scr[hd, tok(s), :], qT_scr[chan(hd), tok(s)])
    for i in range(len(pairs)):
        (pT_scr[i],), inv_scr[i] = _softmax_keys([sT_scr[i]])
    for i, (hd, s) in enumerate(pairs):
        attT_scr[chan(hd), tok(s)] = _dot(vT_scr[chan(hd), tok(s)], pT_scr[i]) * inv_scr[i]

    _merge_and_norm(x, mod, attT_scr, conv_scr, wout_ref, g_ref, b_ref, o_ref)


def _ctx_mixer(l, xp, mod, w_in, w_out, conv_w, ln_g, ln_b, kv_bufs):
    batch, seq_len, _ = xp.shape
    sb = CTX_SEQ_PER_STEP
    m = sb * seq_len
    kv_shape = jax.ShapeDtypeStruct((batch, DEPTH, N_HEADS, HEAD_DIM, seq_len), F32)
    kv_spec = pl.BlockSpec((sb, None, N_HEADS, HEAD_DIM, seq_len), lambda i: (i, l, 0, 0, 0))
    in_specs = [
        pl.BlockSpec((sb, seq_len, D_MODEL), lambda i: (i, 0, 0)),
        pl.BlockSpec((None, SUBLANES, 6 * D_MODEL), lambda i: (l, 0, 0)),
        _weight_spec((None, D_MODEL, 3 * ATT_WIDTH + 3 * CONV_WIDTH), lambda i: (l, 0, 0)),
        _weight_spec((None, D_MODEL, D_MODEL), lambda i: (l, 0, 0)),
        pl.BlockSpec((None, 3, CONV_WIDTH), lambda i: (l, 0, 0)),
        pl.BlockSpec((None, 1, D_MODEL), lambda i: (l, 0, 0)),
        pl.BlockSpec((None, 1, D_MODEL), lambda i: (l, 0, 0)),
        pl.BlockSpec(memory_space=pl.ANY),
        pl.BlockSpec(memory_space=pl.ANY),
    ]
    args = [xp, mod, w_in, w_out, conv_w, ln_g, ln_b, *kv_bufs]
    return pl.pallas_call(
        _ctx_mixer_kernel,
        grid=(batch // sb,),
        in_specs=in_specs,
        out_specs=[pl.BlockSpec((sb, seq_len, D_MODEL), lambda i: (i, 0, 0)), kv_spec, kv_spec],
        out_shape=[jax.ShapeDtypeStruct(xp.shape, F32), kv_shape, kv_shape],
        input_output_aliases={len(args) - 2: 1, len(args) - 1: 2},
        scratch_shapes=[
            pltpu.VMEM((ATT_WIDTH, m), BF16),
            pltpu.VMEM((N_HEADS, m, HEAD_DIM), BF16),
            pltpu.VMEM((ATT_WIDTH, m), BF16),
            pltpu.VMEM((ATT_WIDTH, m), F32),
            pltpu.VMEM((m, CONV_WIDTH), BF16),
            pltpu.VMEM((N_HEADS * sb, seq_len, seq_len), F32),
            pltpu.VMEM((N_HEADS * sb, seq_len, seq_len), BF16),
            pltpu.VMEM((N_HEADS * sb, 1, seq_len), F32),
        ],
        compiler_params=pltpu.CompilerParams(
            dimension_semantics=("arbitrary",), vmem_limit_bytes=VMEM_LIMIT_BYTES),
        name="ctx_mixer",
    )(*args)


def _window_start(r, rows):
    return min(max(r - WIN_ROWS // 2, 0), rows - WIN_ROWS)


def _chunk_key_rows(r0, chunk_rows, rows):
    lo = _window_start(r0, rows) // 2 * 2
    hi = -(-(_window_start(r0 + chunk_rows - 1, rows) + WIN_ROWS) // 2) * 2
    return lo, hi


def _lat_mixer_kernel(x_ref, mod_ref, win_ref, wout_ref, convw_ref, g_ref, b_ref,
                      ckT_ref, cvT_ref, tbl_ref, o_ref,
                      qT_scr, k_scr, vT_scr, attT_scr, conv_scr, sT_scr, pT_scr, inv_scr):
    b = pl.program_id(0)
    hd = pl.program_id(1)
    seq_len = x_ref.shape[0]
    rows = seq_len // GRID_W
    mod = mod_ref[pl.ds(1 + b, 1), :]

    @pl.when(hd == 0)
    def _():
        _project(x_ref[...], mod, win_ref, convw_ref, qT_scr, k_scr, vT_scr, conv_scr, seq_len)

    chan = _head_rows(hd)
    ckT = ckT_ref[...]
    ck = jnp.concatenate([ckT, jnp.zeros_like(ckT)], axis=0).T.astype(BF16)
    cvT = cvT_ref[...].astype(BF16)
    low_half = lax.broadcasted_iota(jnp.int32, (GRID_W, 2 * GRID_W), 1) < GRID_W

    def bias_block(r_pair, rk):
        inside = [_window_start(r, rows) <= rk < _window_start(r, rows) + WIN_ROWS for r in (r_pair, r_pair + 1)]
        d = rk - r_pair + WIN_ROWS - 1
        neg = jnp.full((GRID_W, 2 * GRID_W), NEG_INF, F32)
        if not any(inside):
            return neg
        blk = tbl_ref[d * GRID_W:(d + 1) * GRID_W, :]
        if all(inside):
            return blk
        return jnp.where(low_half, blk, neg) if inside[0] else jnp.where(low_half, neg, blk)

    chunk_rows = LAT_Q_CHUNK // GRID_W
    past = ck.shape[0]
    max_loc = sT_scr.shape[1] - past
    chunks = []
    for qc in range(seq_len // LAT_Q_CHUNK):
        r0 = qc * chunk_rows
        k_lo, k_hi = _chunk_key_rows(r0, chunk_rows, rows)
        chunks.append((qc, r0, k_lo, k_hi, slice(qc * LAT_Q_CHUNK, (qc + 1) * LAT_Q_CHUNK)))
    for qc, r0, k_lo, k_hi, qcols in chunks:
        qT = qT_scr[chan, qcols]
        bias = jnp.concatenate(
            [jnp.concatenate([bias_block(r0 + j, rk) for j in range(0, chunk_rows, 2)], axis=1)
             for rk in range(k_lo, k_hi)], axis=0)
        sT_scr[qc, 0:(k_hi - k_lo) * GRID_W] = _dot(k_scr[hd, k_lo * GRID_W:k_hi * GRID_W, :], qT) + bias
        sT_scr[qc, max_loc:] = _dot(ck, jnp.concatenate([qT, jnp.zeros_like(qT)], axis=0))
    for qc, r0, k_lo, k_hi, qcols in chunks:
        n_loc = (k_hi - k_lo) * GRID_W
        (pT_scr[qc, 0:n_loc], pT_scr[qc, max_loc:]), inv_scr[qc] = _softmax_keys(
            [sT_scr[qc, 0:n_loc], sT_scr[qc, max_loc:]])
    for qc, r0, k_lo, k_hi, qcols in chunks:
        n_loc = (k_hi - k_lo) * GRID_W
        attT_scr[chan, qcols] = (_dot(vT_scr[chan, k_lo * GRID_W:k_hi * GRID_W], pT_scr[qc, 0:n_loc])
                                 + _dot(cvT, pT_scr[qc, max_loc:])) * inv_scr[qc]

    @pl.when(hd == N_HEADS - 1)
    def _():
        _merge_and_norm(x_ref[...], mod, attT_scr, conv_scr, wout_ref, g_ref, b_ref, o_ref)


def _lat_mixer(l, xs, mod, w_in, w_out, conv_w, ln_g, ln_b, cache_kT, cache_vT, tbl):
    batch, seq_len, _ = xs.shape
    past = cache_kT.shape[-1]
    rows, chunk_rows, n_chunks = seq_len // GRID_W, LAT_Q_CHUNK // GRID_W, seq_len // LAT_Q_CHUNK
    max_loc = GRID_W * max(hi - lo for lo, hi in
                           (_chunk_key_rows(qc * chunk_rows, chunk_rows, rows) for qc in range(n_chunks)))
    cache_spec = pl.BlockSpec((None, None, None, HEAD_DIM, past), lambda b, h: (b, l, h, 0, 0))
    return pl.pallas_call(
        _lat_mixer_kernel,
        grid=(batch, N_HEADS),
        in_specs=[
            pl.BlockSpec((None, seq_len, D_MODEL), lambda b, h: (b, 0, 0)),
            pl.BlockSpec((None, SUBLANES, 6 * D_MODEL), lambda b, h: (l, 0, 0)),
            _weight_spec((None, D_MODEL, 3 * ATT_WIDTH + 3 * CONV_WIDTH), lambda b, h: (l, 0, 0)),
            _weight_spec((None, D_MODEL, D_MODEL), lambda b, h: (l, 0, 0)),
            pl.BlockSpec((None, 3, CONV_WIDTH), lambda b, h: (l, 0, 0)),
            pl.BlockSpec((None, 1, D_MODEL), lambda b, h: (l, 0, 0)),
            pl.BlockSpec((None, 1, D_MODEL), lambda b, h: (l, 0, 0)),
            cache_spec, cache_spec,
            pl.BlockSpec((None, None) + tbl.shape[2:], lambda b, h: (l, h, 0, 0)),
        ],
        out_specs=pl.BlockSpec((None, seq_len, D_MODEL), lambda b, h: (b, 0, 0)),
        out_shape=jax.ShapeDtypeStruct(xs.shape, F32),
        scratch_shapes=[
            pltpu.VMEM((ATT_WIDTH, seq_len), BF16),
            pltpu.VMEM((N_HEADS, seq_len, HEAD_DIM), BF16),
            pltpu.VMEM((ATT_WIDTH, seq_len), BF16),
            pltpu.VMEM((ATT_WIDTH, seq_len), F32),
            pltpu.VMEM((seq_len, CONV_WIDTH), BF16),
            pltpu.VMEM((n_chunks, max_loc + past, LAT_Q_CHUNK), F32),
            pltpu.VMEM((n_chunks, max_loc + past, LAT_Q_CHUNK), BF16),
            pltpu.VMEM((n_chunks, 1, LAT_Q_CHUNK), F32),
        ],
        compiler_params=pltpu.CompilerParams(
            dimension_semantics=("arbitrary", "arbitrary"), vmem_limit_bytes=VMEM_LIMIT_BYTES),
        name="lat_mixer",
    )(xs, mod, w_in, w_out, conv_w, ln_g, ln_b, cache_kT, cache_vT, tbl)


def _bias_table(rpb):
    depth, heads, n_dr, n_dc = rpb.shape
    n_blk = n_dr + 1
    lane0 = GRID_W - WIN_COLS
    n_pad = -(-n_dr // SUBLANES) * SUBLANES
    rpb_pad = jnp.pad(rpb[..., ::-1], ((0, 0), (0, 0), (0, n_pad - n_dr), (lane0, LANES - lane0 - n_dc)))
    rpb_pad = rpb_pad.reshape(depth * heads, n_pad, LANES)

    def body(r_ref, o_ref):
        cp = lax.broadcasted_iota(jnp.int32, (GRID_W, LANES), 0)
        lane = lax.broadcasted_iota(jnp.int32, (GRID_W, LANES), 1)
        low_half = lane < GRID_W
        c = jnp.where(low_half, lane, lane - GRID_W)
        col_start = jnp.clip(c - WIN_COLS // 2, 0, GRID_W - WIN_COLS)
        valid = (cp >= col_start) & (cp < col_start + WIN_COLS)
        def skewed(hd, dr, shift):
            row = jnp.broadcast_to(r_ref[hd, dr:dr + 1, :], (GRID_W, LANES))
            return pltpu.roll(row, shift, 1, stride=1, stride_axis=0)

        for hd in range(heads):
            for d in range(n_blk):
                lo = skewed(hd, d, LANES - GRID_W + 1) if d < n_dr else None
                hi = skewed(hd, d - 1, 1) if d >= 1 else None
                if lo is None:
                    blk = jnp.where(valid & ~low_half, hi, NEG_INF)
                elif hi is None:
                    blk = jnp.where(valid & low_half, lo, NEG_INF)
                else:
                    blk = jnp.where(valid, jnp.where(low_half, lo, hi), NEG_INF)
                o_ref[hd, d * GRID_W:(d + 1) * GRID_W, :] = blk

    tbl = pl.pallas_call(
        body,
        grid=(depth,),
        in_specs=[pl.BlockSpec((heads, n_pad, LANES), lambda i: (i, 0, 0))],
        out_specs=pl.BlockSpec((heads, n_blk * GRID_W, LANES), lambda i: (i, 0, 0)),
        out_shape=jax.ShapeDtypeStruct((depth * heads, n_blk * GRID_W, LANES), F32),
        compiler_params=pltpu.CompilerParams(dimension_semantics=("arbitrary",)),
        name="bias_table",
    )(rpb_pad)
    return tbl.reshape(depth, heads, n_blk * GRID_W, LANES)


def _route_t(logits):
    row = lax.broadcasted_iota(jnp.int32, logits.shape, 0)
    row_f = row.astype(F32)
    big = jnp.float32(LANES)

    def first_row(cond):
        return jnp.min(jnp.where(cond, row_f, big), axis=0, keepdims=True)

    gmask = (row >= N_EXPERTS) & (row < N_EXPERTS + N_GROUPS)
    gl = jnp.where(gmask, logits, NEG_INF)
    gexp = jnp.exp(gl - jnp.max(gl, axis=0, keepdims=True))
    gprob = gexp / jnp.sum(gexp, axis=0, keepdims=True)
    g_p = jnp.max(gprob, axis=0, keepdims=True)
    g_idx = first_row(gmask & (gprob == g_p)) - N_EXPERTS

    row_group = jnp.floor(row_f * (1.0 / EXPERTS_PER_GROUP))
    emask = (row < N_EXPERTS) & (row_group == g_idx)
    el = jnp.where(emask, logits, NEG_INF)
    eexp = jnp.exp(el - jnp.max(el, axis=0, keepdims=True))
    eprob = eexp / jnp.sum(eexp, axis=0, keepdims=True)
    p1 = jnp.max(eprob, axis=0, keepdims=True)
    i1 = first_row(emask & (eprob == p1))
    rest = emask & (row_f != i1)
    p2 = jnp.max(jnp.where(rest, eprob, -1.0), axis=0, keepdims=True)
    i2 = first_row(rest & (eprob == p2))
    denom = p1 + p2
    gate = (jnp.where(row_f == i1, g_p * p1 / denom, 0.0)
            + jnp.where(row_f == i2, g_p * p2 / denom, 0.0))
    return gate, g_idx


def _split3(v):
    hi = v.astype(BF16).astype(F32)
    mid = (v - hi).astype(BF16).astype(F32)
    return hi, mid, v - hi - mid


def _moe_route_tile(x_ref, mod_ref, wrT_ref, brT_ref, h_scr, rec_scr, recT_scr, o_ref, tier_smem):
    tm = x_ref.shape[0]
    seg = tm // mod_ref.shape[0]
    h = jnp.concatenate(
        [x_ref[s * seg:(s + 1) * seg, :] * (1.0 + mod_ref[s, :, 4 * D_MODEL:5 * D_MODEL])
         + mod_ref[s, :, 3 * D_MODEL:4 * D_MODEL] for s in range(tm // seg)], axis=0)
    h_hi = h.astype(BF16)
    h_lo = (h - h_hi.astype(F32)).astype(BF16)
    wr = wrT_ref[...]
    wr_hi = wr.astype(BF16)
    wr_lo = (wr - wr_hi.astype(F32)).astype(BF16)
    logits = (_dot_nt(wr_hi, h_hi) + _dot_nt(wr_hi, h_lo) + _dot_nt(wr_lo, h_hi))[0:ROUTE_ROWS, :]
    gate, g_idx = _route_t(logits + brT_ref[0:ROUTE_ROWS, :])

    row_f = lax.broadcasted_iota(jnp.int32, (ROUTE_ROWS, tm), 0).astype(F32)
    gate4 = jnp.concatenate(
        [jnp.sum(jnp.where(row_f == EXPERTS_PER_GROUP * g_idx + j, gate, 0.0), axis=0, keepdims=True)
         for j in range(EXPERTS_PER_GROUP)], axis=0)

    grp = lax.broadcasted_iota(jnp.int32, (SUBLANES, tm), 0).astype(F32)
    onehot = jnp.where(grp == g_idx, 1.0, 0.0)
    ri = lax.broadcasted_iota(jnp.int32, (MOE_BLOCK, MOE_BLOCK), 0)
    ci = lax.broadcasted_iota(jnp.int32, (MOE_BLOCK, MOE_BLOCK), 1)
    earlier = jnp.where(ri < ci, 1.0, 0.0).astype(BF16)
    ranks = []
    most = jnp.zeros((SUBLANES, 1), F32)
    for b in range(tm // MOE_BLOCK):
        oh_b = onehot[:, b * MOE_BLOCK:(b + 1) * MOE_BLOCK]
        ranks.append(_dot(oh_b.astype(BF16), earlier))
        most = jnp.maximum(most, jnp.sum(oh_b, axis=1, keepdims=True))
    rank = jnp.sum(onehot * jnp.concatenate(ranks, axis=1), axis=0, keepdims=True)

    recT = jnp.concatenate([*_split3(gate4), gate4, g_idx, rank,
                            jnp.zeros((LANES - REC_RANK - 1, tm), F32)], axis=0)
    rec = recT.T
    h_scr[:, 0:D_MODEL] = h_hi
    h_scr[:, D_MODEL:] = rec.astype(BF16)
    rec_scr[...] = rec
    recT_scr[...] = recT[REC_GROUP:REC_GROUP + SUBLANES, :]
    grp_row = lax.broadcasted_iota(jnp.int32, (SUBLANES, 1), 0)
    for gp in range(N_GROUPS):
        fullest = jnp.max(jnp.where(grp_row == gp, most, 0.0))
        tier_smem[gp] = sum((fullest > slot).astype(jnp.int32) for slot in MOE_SLOTS)
    o_ref[...] = jnp.zeros_like(o_ref)


def _moe_kernel(x_ref, mod_ref, wrT_ref, brT_ref, win_ref, wout_ref, g_ref, b_ref, o_ref,
                h_scr, rec_scr, recT_scr, act_scr, tier_smem):
    g = pl.program_id(1)
    g_f = g.astype(F32)
    tm = x_ref.shape[0]
    n_blocks = tm // MOE_BLOCK

    @pl.when(g == 0)
    def _():
        _moe_route_tile(x_ref, mod_ref, wrT_ref, brT_ref, h_scr, rec_scr, recT_scr, o_ref, tier_smem)

    def experts(xb, gates, gate_lane, rows):
        for e in range(EXPERTS_PER_GROUP):
            hid = _dot(xb, win_ref[e].astype(BF16))
            act = _silu(hid[:, :D_EXPERT]) * hid[:, D_EXPERT:] * gates[:, gate_lane + e:gate_lane + e + 1]
            act_scr[rows, e * D_EXPERT:(e + 1) * D_EXPERT] = act.astype(BF16)
        return _dot(act_scr[rows, :], wout_ref[...].astype(BF16))

    def compact(slot):
        n_rows = n_blocks * slot
        slot_row = lax.broadcasted_iota(jnp.int32, (slot, MOE_BLOCK), 0).astype(F32)
        gathered = []
        for b in range(n_blocks):
            tok = slice(b * MOE_BLOCK, (b + 1) * MOE_BLOCK)
            pick = (recT_scr[0:1, tok] == g_f) & (recT_scr[1:2, tok] == slot_row)
            gathered.append(_dot(jnp.where(pick, 1.0, 0.0).astype(BF16), h_scr[tok, :]))
        xg = jnp.concatenate(gathered, axis=0)
        ge = xg[:, D_MODEL:]
        gates = ((ge + pltpu.roll(ge, LANES - EXPERTS_PER_GROUP, 1))
                 + pltpu.roll(ge, LANES - 2 * EXPERTS_PER_GROUP, 1))
        y = experts(xg[:, :D_MODEL].astype(BF16), gates, REC_SPLIT, slice(0, n_rows)).astype(BF16)
        slot_col = lax.broadcasted_iota(jnp.int32, (MOE_BLOCK, SCATTER_K), 1).astype(F32)
        for b in range(n_blocks):
            tok = slice(b * MOE_BLOCK, (b + 1) * MOE_BLOCK)
            first = min(b * slot, n_rows - SCATTER_K)
            place = ((rec_scr[tok, REC_GROUP:REC_GROUP + 1] == g_f)
                     & (rec_scr[tok, REC_RANK:REC_RANK + 1] + (b * slot - first) == slot_col))
            o_ref[tok, :] += _dot(jnp.where(place, 1.0, 0.0).astype(BF16), y[first:first + SCATTER_K, :])

    for tier, slot in enumerate(MOE_SLOTS):
        pl.when(tier_smem[g] == tier)(functools.partial(compact, slot))

    @pl.when(tier_smem[g] == len(MOE_SLOTS))
    def _():
        for b in range(n_blocks):
            tok = slice(b * MOE_BLOCK, (b + 1) * MOE_BLOCK)
            gates = jnp.where(rec_scr[tok, REC_GROUP:REC_GROUP + 1] == g_f, rec_scr[tok, :], 0.0)
            o_ref[tok, :] += experts(h_scr[tok, 0:D_MODEL], gates, REC_GATE, slice(0, MOE_BLOCK))

    @pl.when(g == N_GROUPS - 1)
    def _():
        seg = tm // mod_ref.shape[0]
        for s in range(tm // seg):
            tok = slice(s * seg, (s + 1) * seg)
            gf = mod_ref[s, :, 5 * D_MODEL:6 * D_MODEL]
            o_ref[tok, :] = _layer_norm(ALPHA * x_ref[tok, :] + gf * o_ref[tok, :], g_ref[...], b_ref[...])


def _moe(l, x, mod_rows, w_routerT, b_routerT, w_exp_in, w_exp_out, ln_g, ln_b):
    n = x.shape[0]
    tm = MOE_TOKENS
    n_seg = mod_rows.shape[2]
    n_blocks = tm // MOE_BLOCK
    assert tm % MOE_BLOCK == 0
    for slot in MOE_SLOTS:
        assert slot % BF16_ROWS == 0 and slot <= SCATTER_K <= n_blocks * slot
    n_rows = n_blocks * max(MOE_SLOTS)
    return pl.pallas_call(
        _moe_kernel,
        grid=(n // tm, N_GROUPS),
        in_specs=[
            pl.BlockSpec((tm, D_MODEL), lambda i, g: (i, 0)),
            pl.BlockSpec((None, None, n_seg, 1, 6 * D_MODEL), lambda i, g: (l, i, 0, 0, 0)),
            pl.BlockSpec((None, LANES, D_MODEL), lambda i, g: (l, 0, 0)),
            pl.BlockSpec((None, LANES, 1), lambda i, g: (l, 0, 0)),
            pl.BlockSpec((None, EXPERTS_PER_GROUP, D_MODEL, 2 * D_EXPERT), lambda i, g: (l, g, 0, 0)),
            pl.BlockSpec((None, None, EXPERTS_PER_GROUP * D_EXPERT, D_MODEL), lambda i, g: (l, g, 0, 0)),
            pl.BlockSpec((None, 1, D_MODEL), lambda i, g: (l, 0, 0)),
            pl.BlockSpec((None, 1, D_MODEL), lambda i, g: (l, 0, 0)),
        ],
        out_specs=pl.BlockSpec((tm, D_MODEL), lambda i, g: (i, 0)),
        out_shape=jax.ShapeDtypeStruct(x.shape, F32),
        scratch_shapes=[
            pltpu.VMEM((tm, D_MODEL + GATE_COLS), BF16),
            pltpu.VMEM((tm, LANES), F32),
            pltpu.VMEM((SUBLANES, tm), F32),
            pltpu.VMEM((max(n_rows, MOE_BLOCK), EXPERTS_PER_GROUP * D_EXPERT), BF16),
            pltpu.SMEM((N_GROUPS,), jnp.int32),
        ],
        compiler_params=pltpu.CompilerParams(
            dimension_semantics=("arbitrary", "arbitrary"), vmem_limit_bytes=VMEM_LIMIT_BYTES),
        name="hier_moe",
    )(x, mod_rows, w_routerT, b_routerT, w_exp_in, w_exp_out, ln_g, ln_b)


def kernel(x_prompt, x_sample, cache_k, cache_v, c, c_ctx, w_ada, b_ada, w_in, conv_w, rpb, w_out,
           ln1_g, ln1_b, w_router_group, b_router_group, w_router_expert, b_router_expert,
           w_expert_in, w_expert_out, ln2_g, ln2_b):
    batch, seq_len, _ = x_prompt.shape
    dec_batch, dec_seq, _ = x_sample.shape
    assert dec_batch + 1 <= SUBLANES and MOE_TOKENS % dec_seq == 0
    assert (dec_batch * dec_seq) % MOE_TOKENS == 0 and (batch * seq_len) % MOE_TOKENS == 0

    cond = jnp.concatenate([c_ctx[None, :], c, jnp.zeros((SUBLANES - 1 - dec_batch, D_MODEL), F32)], axis=0)
    mod = _modulation(cond, w_ada, b_ada)
    n_ctx_tiles = batch * seq_len // MOE_TOKENS
    mod_rows_ctx = jnp.broadcast_to(mod[:, 0:1, None, None, :], (DEPTH, n_ctx_tiles, 1, 1, 6 * D_MODEL))
    mod_rows_lat = mod[:, 1:1 + dec_batch].reshape(DEPTH, -1, MOE_TOKENS // dec_seq, 1, 6 * D_MODEL)

    w_in_bf = w_in.astype(BF16)
    w_out_bf = w_out.astype(BF16)
    pad = jnp.zeros((DEPTH, LANES - N_EXPERTS - N_GROUPS, D_MODEL), F32)
    w_routerT = jnp.concatenate(
        [jnp.swapaxes(w_router_expert, 1, 2), jnp.swapaxes(w_router_group, 1, 2), pad], axis=1)
    b_routerT = jnp.concatenate([b_router_expert, b_router_group, pad[:, :, 0]], axis=-1)[:, :, None]
    tbl = _bias_table(rpb)
    cache_kT = jnp.swapaxes(cache_k, -1, -2)
    cache_vT = jnp.swapaxes(cache_v, -1, -2)
    ln1_g3, ln1_b3 = ln1_g[:, None, :], ln1_b[:, None, :]
    ln2_g3, ln2_b3 = ln2_g[:, None, :], ln2_b[:, None, :]

    w_exp_out = w_expert_out.reshape(DEPTH, N_GROUPS, EXPERTS_PER_GROUP * D_EXPERT, D_MODEL)

    xp, xs = x_prompt, x_sample
    kv_bufs = [jnp.zeros((batch, DEPTH, N_HEADS, HEAD_DIM, seq_len), F32)] * 2
    for l in range(DEPTH):
        moe = functools.partial(_moe, l, w_routerT=w_routerT, b_routerT=b_routerT, w_exp_in=w_expert_in,
                                w_exp_out=w_exp_out, ln_g=ln2_g3, ln_b=ln2_b3)
        xp, *kv_bufs = _ctx_mixer(l, xp, mod, w_in_bf, w_out_bf, conv_w, ln1_g3, ln1_b3, kv_bufs)
        xp = moe(xp.reshape(-1, D_MODEL), mod_rows_ctx).reshape(xp.shape)
        xs = _lat_mixer(l, xs, mod, w_in_bf, w_out_bf, conv_w, ln1_g3, ln1_b3, cache_kT, cache_vT, tbl)
        xs = moe(xs.reshape(-1, D_MODEL), mod_rows_lat).reshape(xs.shape)
    new_kT, new_vT = kv_bufs
    return (xp, xs, jnp.swapaxes(new_kT, -1, -2), jnp.swapaxes(new_vT, -1, -2))
```

```python
import functools

import jax
import jax.numpy as jnp
from jax import lax
from jax.experimental import pallas as pl
from jax.experimental.pallas import tpu as pltpu

D_MODEL = 1024
DEPTH = 4
GRID_W = 64
ATT_WIDTH = D_MODEL // 2
CONV_WIDTH = D_MODEL - ATT_WIDTH
HEAD_DIM = 64
N_HEADS = ATT_WIDTH // HEAD_DIM
WIN_ROWS = 8
WIN_COLS = 16
N_GROUPS = 4
EXPERTS_PER_GROUP = 4
N_EXPERTS = N_GROUPS * EXPERTS_PER_GROUP
D_EXPERT = D_MODEL // 4
ALPHA = (2 * DEPTH) ** 0.25
LN_EPS = 1e-5
NEG_INF = -1e30
QK_SCALE = HEAD_DIM ** -0.5

F32 = jnp.float32
BF16 = jnp.bfloat16

LANES = 128
SUBLANES = 8
BF16_ROWS = 16
VMEM_LIMIT_BYTES = 56 * 1024 * 1024

CTX_SEQ_PER_STEP = 2
LAT_Q_CHUNK = 128
MERGE_ROWS = 256
MOE_TOKENS = 1024
MOE_BLOCK = 512
MOE_SLOTS = (144, 192, 256)
SCATTER_K = 256
GATE_COLS = 128
ROUTE_ROWS = 24
REC_SPLIT, REC_GATE, REC_GROUP, REC_RANK = 0, 12, 16, 17
MOD_COLS = 2048


def _dot(a, b):
    return jnp.dot(a, b, preferred_element_type=F32)


def _dot_nt(a, b):
    return lax.dot_general(a, b, (((1,), (1,)), ((), ())), preferred_element_type=F32)


def _silu(x):
    return x * (1.0 / (1.0 + jnp.exp(-x)))


def _layer_norm(r, g, b):
    mu = jnp.mean(r, axis=-1, keepdims=True)
    d = r - mu
    var = jnp.mean(d * d, axis=-1, keepdims=True)
    return d * lax.rsqrt(var + LN_EPS) * g + b


def _mod_kernel(cond_ref, w_ref, b_ref, o_ref):
    s = _silu(cond_ref[...]).astype(BF16)
    o_ref[...] = _dot(s, w_ref[...].astype(BF16)) + b_ref[...]


def _modulation(cond, w_ada, b_ada):
    n_out = w_ada.shape[-1]
    return pl.pallas_call(
        _mod_kernel,
        grid=(DEPTH, n_out // MOD_COLS),
        in_specs=[
            pl.BlockSpec((SUBLANES, D_MODEL), lambda l, j: (0, 0)),
            pl.BlockSpec((None, D_MODEL, MOD_COLS), lambda l, j: (l, 0, j)),
            pl.BlockSpec((None, 1, MOD_COLS), lambda l, j: (l, 0, j)),
        ],
        out_specs=pl.BlockSpec((None, SUBLANES, MOD_COLS), lambda l, j: (l, 0, j)),
        out_shape=jax.ShapeDtypeStruct((DEPTH, SUBLANES, n_out), F32),
        compiler_params=pltpu.CompilerParams(
            dimension_semantics=("arbitrary", "arbitrary"),
            vmem_limit_bytes=VMEM_LIMIT_BYTES),
        name="adaln_modulation",
    )(cond, w_ada, b_ada.reshape(DEPTH, 1, n_out))


def _project(x, mod, win_ref, convw_ref, qT_scr, k_scr, vT_scr, conv_scr, seq_len, kT_out=None, vT_out=None):
    m = x.shape[0]
    sa = mod[:, 0:D_MODEL]
    ca = mod[:, D_MODEL:2 * D_MODEL]
    h = (x * (1.0 + ca) + sa).astype(BF16)

    qT_scr[...] = (_dot(h, win_ref[:, 0:ATT_WIDTH]) * QK_SCALE).T.astype(BF16)
    zk = _dot(h, win_ref[:, ATT_WIDTH:2 * ATT_WIDTH])
    for hd in range(N_HEADS):
        k_scr[hd] = zk[:, hd * HEAD_DIM:(hd + 1) * HEAD_DIM].astype(BF16)
    zvT = _dot(h, win_ref[:, 2 * ATT_WIDTH:3 * ATT_WIDTH]).T
    vT_scr[...] = zvT.astype(BF16)
    if kT_out is not None:
        zkT = zk.T
        for out, zT in ((kT_out, zkT), (vT_out, zvT)):
            for s in range(m // seq_len):
                for hd in range(N_HEADS):
                    out[s, hd] = zT[hd * HEAD_DIM:(hd + 1) * HEAD_DIM, s * seq_len:(s + 1) * seq_len]

    c0 = 3 * ATT_WIDTH
    bg = _dot(h, win_ref[:, c0:c0 + CONV_WIDTH])
    cg = _dot(h, win_ref[:, c0 + CONV_WIDTH:c0 + 2 * CONV_WIDTH])
    u = _dot(h, win_ref[:, c0 + 2 * CONV_WIDTH:c0 + 3 * CONV_WIDTH])
    y = cg * u
    t = lax.broadcasted_iota(jnp.int32, (m, 1), 0) % seq_len
    y_prev = jnp.where(t == 0, 0.0, pltpu.roll(y, 1, 0))
    y_next = jnp.where(t == seq_len - 1, 0.0, pltpu.roll(y, m - 1, 0))
    cw = convw_ref[...]
    conv = cw[0:1, :] * y_prev + cw[1:2, :] * y + cw[2:3, :] * y_next
    conv_scr[...] = (bg * conv).astype(BF16)


def _merge_and_norm(x, mod, attT_scr, conv_scr, wout_ref, g_ref, b_ref, o_ref):
    ga = mod[:, 2 * D_MODEL:3 * D_MODEL]
    out = []
    for r in range(0, x.shape[0], MERGE_ROWS):
        rows = slice(r, r + MERGE_ROWS)
        att = attT_scr[:, rows].T.astype(BF16)
        mix = _dot(att, wout_ref[0:ATT_WIDTH, :]) + _dot(conv_scr[rows, :], wout_ref[ATT_WIDTH:, :])
        out.append(_layer_norm(ALPHA * x[rows, :] + ga * mix, g_ref[...], b_ref[...]))
    o_ref[...] = jnp.concatenate(out, axis=0).reshape(o_ref.shape)


def _softmax_keys(parts):
    mx = functools.reduce(jnp.maximum, [jnp.max(p, axis=0, keepdims=True) for p in parts])
    es = [jnp.exp(p - mx) for p in parts]
    inv = 1.0 / functools.reduce(jnp.add, [jnp.sum(e, axis=0, keepdims=True) for e in es])
    return [e.astype(BF16) for e in es], inv


def _head_rows(hd):
    return pl.ds(pl.multiple_of(hd * HEAD_DIM, HEAD_DIM), HEAD_DIM)


def _weight_spec(shape, index_map):
    return pl.BlockSpec(shape, index_map, pipeline_mode=pl.Buffered(1))


def _ctx_mixer_kernel(x_ref, mod_ref, win_ref, wout_ref, convw_ref, g_ref, b_ref, kT_in, vT_in,
                      o_ref, kT_ref, vT_ref, qT_scr, k_scr, vT_scr, attT_scr, conv_scr, sT_scr, pT_scr, inv_scr):
    del kT_in, vT_in
    sb, seq_len, _ = x_ref.shape
    x = x_ref[...].reshape(sb * seq_len, D_MODEL)
    mod = mod_ref[0:1, :]
    _project(x, mod, win_ref, convw_ref, qT_scr, k_scr, vT_scr, conv_scr, seq_len, kT_ref, vT_ref)

    pairs = [(hd, s) for hd in range(N_HEADS) for s in range(sb)]
    chan = lambda hd: slice(hd * HEAD_DIM, (hd + 1) * HEAD_DIM)
    tok = lambda s: slice(s * seq_len, (s + 1) * seq_len)
    for i, (hd, s) in enumerate(pairs):
        sT_scr[i] = _dot(k_scr[hd, tok(s), :], qT_scr[chan(hd), tok(s)])
    for i in range(len(pairs)):
        (pT_scr[i],), inv_scr[i] = _softmax_keys([sT_scr[i]])
    for i, (hd, s) in enumerate(pairs):
        attT_scr[chan(hd), tok(s)] = _dot(vT_scr[chan(hd), tok(s)], pT_scr[i]) * inv_scr[i]

    _merge_and_norm(x, mod, attT_scr, conv_scr, wout_ref, g_ref, b_ref, o_ref)


def _ctx_mixer(l, xp, mod, w_in, w_out, conv_w, ln_g, ln_b, kv_bufs):
    batch, seq_len, _ = xp.shape
    sb = CTX_SEQ_PER_STEP
    m = sb * seq_len
    kv_shape = jax.ShapeDtypeStruct((batch, DEPTH, N_HEADS, HEAD_DIM, seq_len), F32)
    kv_spec = pl.BlockSpec((sb, None, N_HEADS, HEAD_DIM, seq_len), lambda i: (i, l, 0, 0, 0))
    in_specs = [
        pl.BlockSpec((sb, seq_len, D_MODEL), lambda i: (i, 0, 0)),
        pl.BlockSpec((None, SUBLANES, 6 * D_MODEL), lambda i: (l, 0, 0)),
        _weight_spec((None, D_MODEL, 3 * ATT_WIDTH + 3 * CONV_WIDTH), lambda i: (l, 0, 0)),
        _weight_spec((None, D_MODEL, D_MODEL), lambda i: (l, 0, 0)),
        pl.BlockSpec((None, 3, CONV_WIDTH), lambda i: (l, 0, 0)),
        pl.BlockSpec((None, 1, D_MODEL), lambda i: (l, 0, 0)),
        pl.BlockSpec((None, 1, D_MODEL), lambda i: (l, 0, 0)),
        pl.BlockSpec(memory_space=pl.ANY),
        pl.BlockSpec(memory_space=pl.ANY),
    ]
    args = [xp, mod, w_in, w_out, conv_w, ln_g, ln_b, *kv_bufs]
    return pl.pallas_call(
        _ctx_mixer_kernel,
        grid=(batch // sb,),
        in_specs=in_specs,
        out_specs=[pl.BlockSpec((sb, seq_len, D_MODEL), lambda i: (i, 0, 0)), kv_spec, kv_spec],
        out_shape=[jax.ShapeDtypeStruct(xp.shape, F32), kv_shape, kv_shape],
        input_output_aliases={len(args) - 2: 1, len(args) - 1: 2},
        scratch_shapes=[
            pltpu.VMEM((ATT_WIDTH, m), BF16),
            pltpu.VMEM((N_HEADS, m, HEAD_DIM), BF16),
            pltpu.VMEM((ATT_WIDTH, m), BF16),
            pltpu.VMEM((ATT_WIDTH, m), F32),
            pltpu.VMEM((m, CONV_WIDTH), BF16),
            pltpu.VMEM((N_HEADS * sb, seq_len, seq_len), F32),
            pltpu.VMEM((N_HEADS * sb, seq_len, seq_len), BF16),
            pltpu.VMEM((N_HEADS * sb, 1, seq_len), F32),
        ],
        compiler_params=pltpu.CompilerParams(
            dimension_semantics=("arbitrary",), vmem_limit_bytes=VMEM_LIMIT_BYTES),
        name="ctx_mixer",
    )(*args)


def _window_start(r, rows):
    return min(max(r - WIN_ROWS // 2, 0), rows - WIN_ROWS)


def _chunk_key_rows(r0, chunk_rows, rows):
    lo = _window_start(r0, rows) // 2 * 2
    hi = -(-(_window_start(r0 + chunk_rows - 1, rows) + WIN_ROWS) // 2) * 2
    return lo, hi


def _lat_mixer_kernel(x_ref, mod_ref, win_ref, wout_ref, convw_ref, g_ref, b_ref,
                      ckT_ref, cvT_ref, tbl_ref, o_ref,
                      qT_scr, k_scr, vT_scr, attT_scr, conv_scr, sT_scr, pT_scr, inv_scr):
    b = pl.program_id(0)
    hd = pl.program_id(1)
    seq_len = x_ref.shape[0]
    rows = seq_len // GRID_W
    mod = mod_ref[pl.ds(1 + b, 1), :]

    @pl.when(hd == 0)
    def _():
        _project(x_ref[...], mod, win_ref, convw_ref, qT_scr, k_scr, vT_scr, conv_scr, seq_len)

    chan = _head_rows(hd)
    ckT = ckT_ref[...]
    ck = jnp.concatenate([ckT, jnp.zeros_like(ckT)], axis=0).T.astype(BF16)
    cvT = cvT_ref[...].astype(BF16)
    low_half = lax.broadcasted_iota(jnp.int32, (GRID_W, 2 * GRID_W), 1) < GRID_W

    def bias_block(r_pair, rk):
        inside = [_window_start(r, rows) <= rk < _window_start(r, rows) + WIN_ROWS for r in (r_pair, r_pair + 1)]
        d = rk - r_pair + WIN_ROWS - 1
        neg = jnp.full((GRID_W, 2 * GRID_W), NEG_INF, F32)
        if not any(inside):
            return neg
        blk = tbl_ref[d * GRID_W:(d + 1) * GRID_W, :]
        if all(inside):
            return blk
        return jnp.where(low_half, blk, neg) if inside[0] else jnp.where(low_half, neg, blk)

    chunk_rows = LAT_Q_CHUNK // GRID_W
    past = ck.shape[0]
    max_loc = sT_scr.shape[1] - past
    chunks = []
    for qc in range(seq_len // LAT_Q_CHUNK):
        r0 = qc * chunk_rows
        k_lo, k_hi = _chunk_key_rows(r0, chunk_rows, rows)
        chunks.append((qc, r0, k_lo, k_hi, slice(qc * LAT_Q_CHUNK, (qc + 1) * LAT_Q_CHUNK)))
    for qc, r0, k_lo, k_hi, qcols in chunks:
        qT = qT_scr[chan, qcols]
        bias = jnp.concatenate(
            [jnp.concatenate([bias_block(r0 + j, rk) for j in range(0, chunk_rows, 2)], axis=1)
             for rk in range(k_lo, k_hi)], axis=0)
        sT_scr[qc, 0:(k_hi - k_lo) * GRID_W] = _dot(k_scr[hd, k_lo * GRID_W:k_hi * GRID_W, :], qT) + bias
        sT_scr[qc, max_loc:] = _dot(ck, jnp.concatenate([qT, jnp.zeros_like(qT)], axis=0))
    for qc, r0, k_lo, k_hi, qcols in chunks:
        n_loc = (k_hi - k_lo) * GRID_W
        (pT_scr[qc, 0:n_loc], pT_scr[qc, max_loc:]), inv_scr[qc] = _softmax_keys(
            [sT_scr[qc, 0:n_loc], sT_scr[qc, max_loc:]])
    for qc, r0, k_lo, k_hi, qcols in chunks:
        n_loc = (k_hi - k_lo) * GRID_W
        attT_scr[chan, qcols] = (_dot(vT_scr[chan, k_lo * GRID_W:k_hi * GRID_W], pT_scr[qc, 0:n_loc])
                                 + _dot(cvT, pT_scr[qc, max_loc:])) * inv_scr[qc]

    @pl.when(hd == N_HEADS - 1)
    def _():
        _merge_and_norm(x_ref[...], mod, attT_scr, conv_scr, wout_ref, g_ref, b_ref, o_ref)


def _lat_mixer(l, xs, mod, w_in, w_out, conv_w, ln_g, ln_b, cache_kT, cache_vT, tbl):
    batch, seq_len, _ = xs.shape
    past = cache_kT.shape[-1]
    rows, chunk_rows, n_chunks = seq_len // GRID_W, LAT_Q_CHUNK // GRID_W, seq_len // LAT_Q_CHUNK
    max_loc = GRID_W * max(hi - lo for lo, hi in
                           (_chunk_key_rows(qc * chunk_rows, chunk_rows, rows) for qc in range(n_chunks)))
    cache_spec = pl.BlockSpec((None, None, None, HEAD_DIM, past), lambda b, h: (b, l, h, 0, 0))
    return pl.pallas_call(
        _lat_mixer_kernel,
        grid=(batch, N_HEADS),
        in_specs=[
            pl.BlockSpec((None, seq_len, D_MODEL), lambda b, h: (b, 0, 0)),
            pl.BlockSpec((None, SUBLANES, 6 * D_MODEL), lambda b, h: (l, 0, 0)),
            _weight_spec((None, D_MODEL, 3 * ATT_WIDTH + 3 * CONV_WIDTH), lambda b, h: (l, 0, 0)),
            _weight_spec((None, D_MODEL, D_MODEL), lambda b, h: (l, 0, 0)),
            pl.BlockSpec((None, 3, CONV_WIDTH), lambda b, h: (l, 0, 0)),
            pl.BlockSpec((None, 1, D_MODEL), lambda b, h: (l, 0, 0)),
            pl.BlockSpec((None, 1, D_MODEL), lambda b, h: (l, 0, 0)),
            cache_spec, cache_spec,
            pl.BlockSpec((None, None) + tbl.shape[2:], lambda b, h: (l, h, 0, 0)),
        ],
        out_specs=pl.BlockSpec((None, seq_len, D_MODEL), lambda b, h: (b, 0, 0)),
        out_shape=jax.ShapeDtypeStruct(xs.shape, F32),
        scratch_shapes=[
            pltpu.VMEM((ATT_WIDTH, seq_len), BF16),
            pltpu.VMEM((N_HEADS, seq_len, HEAD_DIM), BF16),
            pltpu.VMEM((ATT_WIDTH, seq_len), BF16),
            pltpu.VMEM((ATT_WIDTH, seq_len), F32),
            pltpu.VMEM((seq_len, CONV_WIDTH), BF16),
            pltpu.VMEM((n_chunks, max_loc + past, LAT_Q_CHUNK), F32),
            pltpu.VMEM((n_chunks, max_loc + past, LAT_Q_CHUNK), BF16),
            pltpu.VMEM((n_chunks, 1, LAT_Q_CHUNK), F32),
        ],
        compiler_params=pltpu.CompilerParams(
            dimension_semantics=("arbitrary", "arbitrary"), vmem_limit_bytes=VMEM_LIMIT_BYTES),
        name="lat_mixer",
    )(xs, mod, w_in, w_out, conv_w, ln_g, ln_b, cache_kT, cache_vT, tbl)


def _bias_table(rpb):
    depth, heads, n_dr, n_dc = rpb.shape
    n_blk = n_dr + 1
    lane0 = GRID_W - WIN_COLS
    n_pad = -(-n_dr // SUBLANES) * SUBLANES
    rpb_pad = jnp.pad(rpb[..., ::-1], ((0, 0), (0, 0), (0, n_pad - n_dr), (lane0, LANES - lane0 - n_dc)))
    rpb_pad = rpb_pad.reshape(depth * heads, n_pad, LANES)

    def body(r_ref, o_ref):
        cp = lax.broadcasted_iota(jnp.int32, (GRID_W, LANES), 0)
        lane = lax.broadcasted_iota(jnp.int32, (GRID_W, LANES), 1)
        low_half = lane < GRID_W
        c = jnp.where(low_half, lane, lane - GRID_W)
        col_start = jnp.clip(c - WIN_COLS // 2, 0, GRID_W - WIN_COLS)
        valid = (cp >= col_start) & (cp < col_start + WIN_COLS)
        def skewed(hd, dr, shift):
            row = jnp.broadcast_to(r_ref[hd, dr:dr + 1, :], (GRID_W, LANES))
            return pltpu.roll(row, shift, 1, stride=1, stride_axis=0)

        for hd in range(heads):
            for d in range(n_blk):
                lo = skewed(hd, d, LANES - GRID_W + 1) if d < n_dr else None
                hi = skewed(hd, d - 1, 1) if d >= 1 else None
                if lo is None:
                    blk = jnp.where(valid & ~low_half, hi, NEG_INF)
                elif hi is None:
                    blk = jnp.where(valid & low_half, lo, NEG_INF)
                else:
                    blk = jnp.where(valid, jnp.where(low_half, lo, hi), NEG_INF)
                o_ref[hd, d * GRID_W:(d + 1) * GRID_W, :] = blk

    tbl = pl.pallas_call(
        body,
        grid=(depth,),
        in_specs=[pl.BlockSpec((heads, n_pad, LANES), lambda i: (i, 0, 0))],
        out_specs=pl.BlockSpec((heads, n_blk * GRID_W, LANES), lambda i: (i, 0, 0)),
        out_shape=jax.ShapeDtypeStruct((depth * heads, n_blk * GRID_W, LANES), F32),
        compiler_params=pltpu.CompilerParams(dimension_semantics=("arbitrary",)),
        name="bias_table",
    )(rpb_pad)
    return tbl.reshape(depth, heads, n_blk * GRID_W, LANES)


def _route_t(logits):
    row = lax.broadcasted_iota(jnp.int32, logits.shape, 0)
    row_f = row.astype(F32)
    big = jnp.float32(LANES)

    def first_row(cond):
        return jnp.min(jnp.where(cond, row_f, big), axis=0, keepdims=True)

    gmask = (row >= N_EXPERTS) & (row < N_EXPERTS + N_GROUPS)
    gl = jnp.where(gmask, logits, NEG_INF)
    gexp = jnp.exp(gl - jnp.max(gl, axis=0, keepdims=True))
    gprob = gexp / jnp.sum(gexp, axis=0, keepdims=True)
    g_p = jnp.max(gprob, axis=0, keepdims=True)
    g_idx = first_row(gmask & (gprob == g_p)) - N_EXPERTS

    row_group = jnp.floor(row_f * (1.0 / EXPERTS_PER_GROUP))
    emask = (row < N_EXPERTS) & (row_group == g_idx)
    el = jnp.where(emask, logits, NEG_INF)
    eexp = jnp.exp(el - jnp.max(el, axis=0, keepdims=True))
    eprob = eexp / jnp.sum(eexp, axis=0, keepdims=True)
    p1 = jnp.max(eprob, axis=0, keepdims=True)
    i1 = first_row(emask & (eprob == p1))
    rest = emask & (row_f != i1)
    p2 = jnp.max(jnp.where(rest, eprob, -1.0), axis=0, keepdims=True)
    i2 = first_row(rest & (eprob == p2))
    denom = p1 + p2
    gate = (jnp.where(row_f == i1, g_p * p1 / denom, 0.0)
            + jnp.where(row_f == i2, g_p * p2 / denom, 0.0))
    return gate, g_idx


def _split3(v):
    hi = v.astype(BF16).astype(F32)
    mid = (v - hi).astype(BF16).astype(F32)
    return hi, mid, v - hi - mid


def _moe_route_tile(x_ref, mod_ref, wrT_ref, brT_ref, h_scr, rec_scr, recT_scr, o_ref, tier_smem):
    tm = x_ref.shape[0]
    seg = tm // mod_ref.shape[0]
    h = jnp.concatenate(
        [x_ref[s * seg:(s + 1) * seg, :] * (1.0 + mod_ref[s, :, 4 * D_MODEL:5 * D_MODEL])
         + mod_ref[s, :, 3 * D_MODEL:4 * D_MODEL] for s in range(tm // seg)], axis=0)
    h_hi = h.astype(BF16)
    h_lo = (h - h_hi.astype(F32)).astype(BF16)
    wr = wrT_ref[...]
    wr_hi = wr.astype(BF16)
    wr_lo = (wr - wr_hi.astype(F32)).astype(BF16)
    logits = (_dot_nt(wr_hi, h_hi) + _dot_nt(wr_hi, h_lo) + _dot_nt(wr_lo, h_hi))[0:ROUTE_ROWS, :]
    gate, g_idx = _route_t(logits + brT_ref[0:ROUTE_ROWS, :])

    row_f = lax.broadcasted_iota(jnp.int32, (ROUTE_ROWS, tm), 0).astype(F32)
    gate4 = jnp.concatenate(
        [jnp.sum(jnp.where(row_f == EXPERTS_PER_GROUP * g_idx + j, gate, 0.0), axis=0, keepdims=True)
         for j in range(EXPERTS_PER_GROUP)], axis=0)

    grp = lax.broadcasted_iota(jnp.int32, (SUBLANES, tm), 0).astype(F32)
    onehot = jnp.where(grp == g_idx, 1.0, 0.0)
    ri = lax.broadcasted_iota(jnp.int32, (MOE_BLOCK, MOE_BLOCK), 0)
    ci = lax.broadcasted_iota(jnp.int32, (MOE_BLOCK, MOE_BLOCK), 1)
    earlier = jnp.where(ri < ci, 1.0, 0.0).astype(BF16)
    ranks = []
    most = jnp.zeros((SUBLANES, 1), F32)
    for b in range(tm // MOE_BLOCK):
        oh_b = onehot[:, b * MOE_BLOCK:(b + 1) * MOE_BLOCK]
        ranks.append(_dot(oh_b.astype(BF16), earlier))
        most = jnp.maximum(most, jnp.sum(oh_b, axis=1, keepdims=True))
    rank = jnp.sum(onehot * jnp.concatenate(ranks, axis=1), axis=0, keepdims=True)

    recT = jnp.concatenate([*_split3(gate4), gate4, g_idx, rank,
                            jnp.zeros((LANES - REC_RANK - 1, tm), F32)], axis=0)
    rec = recT.T
    h_scr[:, 0:D_MODEL] = h_hi
    h_scr[:, D_MODEL:] = rec.astype(BF16)
    rec_scr[...] = rec
    recT_scr[...] = recT[REC_GROUP:REC_GROUP + SUBLANES, :]
    grp_row = lax.broadcasted_iota(jnp.int32, (SUBLANES, 1), 0)
    for gp in range(N_GROUPS):
        fullest = jnp.max(jnp.where(grp_row == gp, most, 0.0))
        tier_smem[gp] = sum((fullest > slot).astype(jnp.int32) for slot in MOE_SLOTS)
    o_ref[...] = jnp.zeros_like(o_ref)


def _moe_kernel(layer, x_ref, mod_ref, wrT_ref, brT_ref, win_hbm, wout_hbm, g_ref, b_ref, o_ref,
                h_scr, rec_scr, recT_scr, act_scr, win_buf, wout_buf, w_sem, tier_smem):
    g = pl.program_id(1)
    g_f = g.astype(F32)
    tm = x_ref.shape[0]
    n_blocks = tm // MOE_BLOCK
    step = pl.program_id(0) * N_GROUPS + g
    buf = step % 2

    def weight_copies(group, buf):
        experts_of_group = pl.ds(group * EXPERTS_PER_GROUP, EXPERTS_PER_GROUP)
        return (pltpu.make_async_copy(win_hbm.at[layer, experts_of_group], win_buf.at[buf], w_sem.at[0, buf]),
                pltpu.make_async_copy(wout_hbm.at[layer, group], wout_buf.at[buf], w_sem.at[1, buf]))

    @pl.when(step == 0)
    def _():
        for copy in weight_copies(0, 0):
            copy.start()

    @pl.when(step + 1 < pl.num_programs(0) * N_GROUPS)
    def _():
        for copy in weight_copies((g + 1) % N_GROUPS, 1 - buf):
            copy.start()

    @pl.when(g == 0)
    def _():
        _moe_route_tile(x_ref, mod_ref, wrT_ref, brT_ref, h_scr, rec_scr, recT_scr, o_ref, tier_smem)

    for copy in weight_copies(g, buf):
        copy.wait()
    win_ref = win_buf.at[buf]
    wout_ref = wout_buf.at[buf]

    def experts(xb, gates, gate_lane, rows):
        for e in range(EXPERTS_PER_GROUP):
            hid = _dot(xb, win_ref[e].astype(BF16))
            act = _silu(hid[:, :D_EXPERT]) * hid[:, D_EXPERT:] * gates[:, gate_lane + e:gate_lane + e + 1]
            act_scr[rows, e * D_EXPERT:(e + 1) * D_EXPERT] = act.astype(BF16)
        return _dot(act_scr[rows, :], wout_ref[...].astype(BF16))

    def compact(slot):
        n_rows = n_blocks * slot
        slot_row = lax.broadcasted_iota(jnp.int32, (slot, MOE_BLOCK), 0).astype(F32)
        gathered = []
        for b in range(n_blocks):
            tok = slice(b * MOE_BLOCK, (b + 1) * MOE_BLOCK)
            pick = (recT_scr[0:1, tok] == g_f) & (recT_scr[1:2, tok] == slot_row)
            gathered.append(_dot(jnp.where(pick, 1.0, 0.0).astype(BF16), h_scr[tok, :]))
        xg = jnp.concatenate(gathered, axis=0)
        ge = xg[:, D_MODEL:]
        gates = ((ge + pltpu.roll(ge, LANES - EXPERTS_PER_GROUP, 1))
                 + pltpu.roll(ge, LANES - 2 * EXPERTS_PER_GROUP, 1))
        y = experts(xg[:, :D_MODEL].astype(BF16), gates, REC_SPLIT, slice(0, n_rows)).astype(BF16)
        slot_col = lax.broadcasted_iota(jnp.int32, (MOE_BLOCK, SCATTER_K), 1).astype(F32)
        for b in range(n_blocks):
            tok = slice(b * MOE_BLOCK, (b + 1) * MOE_BLOCK)
            first = min(b * slot, n_rows - SCATTER_K)
            place = ((rec_scr[tok, REC_GROUP:REC_GROUP + 1] == g_f)
                     & (rec_scr[tok, REC_RANK:REC_RANK + 1] + (b * slot - first) == slot_col))
            o_ref[tok, :] += _dot(jnp.where(place, 1.0, 0.0).astype(BF16), y[first:first + SCATTER_K, :])

    for tier, slot in enumerate(MOE_SLOTS):
        pl.when(tier_smem[g] == tier)(functools.partial(compact, slot))

    @pl.when(tier_smem[g] == len(MOE_SLOTS))
    def _():
        for b in range(n_blocks):
            tok = slice(b * MOE_BLOCK, (b + 1) * MOE_BLOCK)
            gates = jnp.where(rec_scr[tok, REC_GROUP:REC_GROUP + 1] == g_f, rec_scr[tok, :], 0.0)
            o_ref[tok, :] += experts(h_scr[tok, 0:D_MODEL], gates, REC_GATE, slice(0, MOE_BLOCK))

    @pl.when(g == N_GROUPS - 1)
    def _():
        seg = tm // mod_ref.shape[0]
        for s in range(tm // seg):
            tok = slice(s * seg, (s + 1) * seg)
            gf = mod_ref[s, :, 5 * D_MODEL:6 * D_MODEL]
            o_ref[tok, :] = _layer_norm(ALPHA * x_ref[tok, :] + gf * o_ref[tok, :], g_ref[...], b_ref[...])


def _moe(l, x, mod_rows, w_routerT, b_routerT, w_exp_in, w_exp_out, ln_g, ln_b):
    n = x.shape[0]
    tm = MOE_TOKENS
    n_seg = mod_rows.shape[2]
    n_blocks = tm // MOE_BLOCK
    assert tm % MOE_BLOCK == 0
    for slot in MOE_SLOTS:
        assert slot % BF16_ROWS == 0 and slot <= SCATTER_K <= n_blocks * slot
    n_rows = n_blocks * max(MOE_SLOTS)
    d_act = EXPERTS_PER_GROUP * D_EXPERT
    return pl.pallas_call(
        functools.partial(_moe_kernel, l),
        grid=(n // tm, N_GROUPS),
        in_specs=[
            pl.BlockSpec((tm, D_MODEL), lambda i, g: (i, 0)),
            pl.BlockSpec((None, None, n_seg, 1, 6 * D_MODEL), lambda i, g: (l, i, 0, 0, 0)),
            pl.BlockSpec((None, LANES, D_MODEL), lambda i, g: (l, 0, 0)),
            pl.BlockSpec((None, LANES, 1), lambda i, g: (l, 0, 0)),
            pl.BlockSpec(memory_space=pl.ANY),
            pl.BlockSpec(memory_space=pl.ANY),
            pl.BlockSpec((None, 1, D_MODEL), lambda i, g: (l, 0, 0)),
            pl.BlockSpec((None, 1, D_MODEL), lambda i, g: (l, 0, 0)),
        ],
        out_specs=pl.BlockSpec((tm, D_MODEL), lambda i, g: (i, 0)),
        out_shape=jax.ShapeDtypeStruct(x.shape, F32),
        scratch_shapes=[
            pltpu.VMEM((tm, D_MODEL + GATE_COLS), BF16),
            pltpu.VMEM((tm, LANES), F32),
            pltpu.VMEM((SUBLANES, tm), F32),
            pltpu.VMEM((max(n_rows, MOE_BLOCK), d_act), BF16),
            pltpu.VMEM((2, EXPERTS_PER_GROUP, D_MODEL, 2 * D_EXPERT), F32),
            pltpu.VMEM((2, d_act, D_MODEL), F32),
            pltpu.SemaphoreType.DMA((2, 2)),
            pltpu.SMEM((N_GROUPS,), jnp.int32),
        ],
        compiler_params=pltpu.CompilerParams(
            dimension_semantics=("arbitrary", "arbitrary"), vmem_limit_bytes=VMEM_LIMIT_BYTES),
        name="hier_moe",
    )(x, mod_rows, w_routerT, b_routerT, w_exp_in, w_exp_out, ln_g, ln_b)


def kernel(x_prompt, x_sample, cache_k, cache_v, c, c_ctx, w_ada, b_ada, w_in, conv_w, rpb, w_out,
           ln1_g, ln1_b, w_router_group, b_router_group, w_router_expert, b_router_expert,
           w_expert_in, w_expert_out, ln2_g, ln2_b):
    batch, seq_len, _ = x_prompt.shape
    dec_batch, dec_seq, _ = x_sample.shape
    assert dec_batch + 1 <= SUBLANES and MOE_TOKENS % dec_seq == 0
    assert (dec_batch * dec_seq) % MOE_TOKENS == 0 and (batch * seq_len) % MOE_TOKENS == 0

    cond = jnp.concatenate([c_ctx[None, :], c, jnp.zeros((SUBLANES - 1 - dec_batch, D_MODEL), F32)], axis=0)
    mod = _modulation(cond, w_ada, b_ada)
    n_ctx_tiles = batch * seq_len // MOE_TOKENS
    mod_rows_ctx = jnp.broadcast_to(mod[:, 0:1, None, None, :], (DEPTH, n_ctx_tiles, 1, 1, 6 * D_MODEL))
    mod_rows_lat = mod[:, 1:1 + dec_batch].reshape(DEPTH, -1, MOE_TOKENS // dec_seq, 1, 6 * D_MODEL)

    w_in_bf = w_in.astype(BF16)
    w_out_bf = w_out.astype(BF16)
    pad = jnp.zeros((DEPTH, LANES - N_EXPERTS - N_GROUPS, D_MODEL), F32)
    w_routerT = jnp.concatenate(
        [jnp.swapaxes(w_router_expert, 1, 2), jnp.swapaxes(w_router_group, 1, 2), pad], axis=1)
    b_routerT = jnp.concatenate([b_router_expert, b_router_group, pad[:, :, 0]], axis=-1)[:, :, None]
    tbl = _bias_table(rpb)
    cache_kT = jnp.swapaxes(cache_k, -1, -2)
    cache_vT = jnp.swapaxes(cache_v, -1, -2)
    ln1_g3, ln1_b3 = ln1_g[:, None, :], ln1_b[:, None, :]
    ln2_g3, ln2_b3 = ln2_g[:, None, :], ln2_b[:, None, :]

    w_exp_out = w_expert_out.reshape(DEPTH, N_GROUPS, EXPERTS_PER_GROUP * D_EXPERT, D_MODEL)

    xp, xs = x_prompt, x_sample
    kv_bufs = [jnp.zeros((batch, DEPTH, N_HEADS, HEAD_DIM, seq_len), F32)] * 2
    for l in range(DEPTH):
        moe = functools.partial(_moe, l, w_routerT=w_routerT, b_routerT=b_routerT, w_exp_in=w_expert_in,
                                w_exp_out=w_exp_out, ln_g=ln2_g3, ln_b=ln2_b3)
        xp, *kv_bufs = _ctx_mixer(l, xp, mod, w_in_bf, w_out_bf, conv_w, ln1_g3, ln1_b3, kv_bufs)
        xp = moe(xp.reshape(-1, D_MODEL), mod_rows_ctx).reshape(xp.shape)
        xs = _lat_mixer(l, xs, mod, w_in_bf, w_out_bf, conv_w, ln1_g3, ln1_b3, cache_kT, cache_vT, tbl)
        xs = moe(xs.reshape(-1, D_MODEL), mod_rows_lat).reshape(xs.shape)
    new_kT, new_vT = kv_bufs
    return (xp, xs, jnp.swapaxes(new_kT, -1, -2), jnp.swapaxes(new_vT, -1, -2))
```

```python
import functools

import jax
import jax.numpy as jnp
from jax import lax
from jax.experimental import pallas as pl
from jax.experimental.pallas import tpu as pltpu

D_MODEL = 1024
DEPTH = 4
GRID_W = 64
ATT_WIDTH = D_MODEL // 2
CONV_WIDTH = D_MODEL - ATT_WIDTH
HEAD_DIM = 64
N_HEADS = ATT_WIDTH // HEAD_DIM
WIN_ROWS = 8
WIN_COLS = 16
N_GROUPS = 4
EXPERTS_PER_GROUP = 4
N_EXPERTS = N_GROUPS * EXPERTS_PER_GROUP
D_EXPERT = D_MODEL // 4
ALPHA = (2 * DEPTH) ** 0.25
LN_EPS = 1e-5
NEG_INF = -1e30
QK_SCALE = HEAD_DIM ** -0.5

F32 = jnp.float32
BF16 = jnp.bfloat16

LANES = 128
SUBLANES = 8
BF16_ROWS = 16
VMEM_LIMIT_BYTES = 56 * 1024 * 1024

CTX_SEQ_PER_STEP = 2
LAT_Q_CHUNK = 128
MERGE_ROWS = 256
MOE_TOKENS = 1024
MOE_BLOCK = 512
MOE_SLOTS = (144, 192, 256)
SCATTER_K = 256
GATE_COLS = 128
ROUTE_ROWS = 24
REC_SPLIT, REC_GATE, REC_GROUP, REC_RANK = 0, 12, 16, 17
MOD_COLS = 2048


def _dot(a, b):
    return jnp.dot(a, b, preferred_element_type=F32)


def _dot_nt(a, b):
    return lax.dot_general(a, b, (((1,), (1,)), ((), ())), preferred_element_type=F32)


def _silu(x):
    return x * (1.0 / (1.0 + jnp.exp(-x)))


def _layer_norm(r, g, b):
    mu = jnp.mean(r, axis=-1, keepdims=True)
    d = r - mu
    var = jnp.mean(d * d, axis=-1, keepdims=True)
    return d * lax.rsqrt(var + LN_EPS) * g + b


def _mod_kernel(cond_ref, w_ref, b_ref, o_ref):
    s = _silu(cond_ref[...]).astype(BF16)
    o_ref[...] = _dot(s, w_ref[...].astype(BF16)) + b_ref[...]


def _modulation(cond, w_ada, b_ada):
    n_out = w_ada.shape[-1]
    return pl.pallas_call(
        _mod_kernel,
        grid=(DEPTH, n_out // MOD_COLS),
        in_specs=[
            pl.BlockSpec((SUBLANES, D_MODEL), lambda l, j: (0, 0)),
            pl.BlockSpec((None, D_MODEL, MOD_COLS), lambda l, j: (l, 0, j)),
            pl.BlockSpec((None, 1, MOD_COLS), lambda l, j: (l, 0, j)),
        ],
        out_specs=pl.BlockSpec((None, SUBLANES, MOD_COLS), lambda l, j: (l, 0, j)),
        out_shape=jax.ShapeDtypeStruct((DEPTH, SUBLANES, n_out), F32),
        compiler_params=pltpu.CompilerParams(
            dimension_semantics=("arbitrary", "arbitrary"),
            vmem_limit_bytes=VMEM_LIMIT_BYTES),
        name="adaln_modulation",
    )(cond, w_ada, b_ada.reshape(DEPTH, 1, n_out))


def _project(x, mod, win_ref, convw_ref, qT_scr, k_scr, vT_scr, conv_scr, seq_len, kT_out=None, vT_out=None):
    m = x.shape[0]
    sa = mod[:, 0:D_MODEL]
    ca = mod[:, D_MODEL:2 * D_MODEL]
    h = (x * (1.0 + ca) + sa).astype(BF16)

    qT_scr[...] = (_dot(h, win_ref[:, 0:ATT_WIDTH]) * QK_SCALE).T.astype(BF16)
    zk = _dot(h, win_ref[:, ATT_WIDTH:2 * ATT_WIDTH])
    for hd in range(N_HEADS):
        k_scr[hd] = zk[:, hd * HEAD_DIM:(hd + 1) * HEAD_DIM].astype(BF16)
    zvT = _dot(h, win_ref[:, 2 * ATT_WIDTH:3 * ATT_WIDTH]).T
    vT_scr[...] = zvT.astype(BF16)
    if kT_out is not None:
        zkT = zk.T
        for out, zT in ((kT_out, zkT), (vT_out, zvT)):
            for s in range(m // seq_len):
                for hd in range(N_HEADS):
                    out[s, hd] = zT[hd * HEAD_DIM:(hd + 1) * HEAD_DIM, s * seq_len:(s + 1) * seq_len]

    c0 = 3 * ATT_WIDTH
    bg = _dot(h, win_ref[:, c0:c0 + CONV_WIDTH])
    cg = _dot(h, win_ref[:, c0 + CONV_WIDTH:c0 + 2 * CONV_WIDTH])
    u = _dot(h, win_ref[:, c0 + 2 * CONV_WIDTH:c0 + 3 * CONV_WIDTH])
    y = cg * u
    t = lax.broadcasted_iota(jnp.int32, (m, 1), 0) % seq_len
    y_prev = jnp.where(t == 0, 0.0, pltpu.roll(y, 1, 0))
    y_next = jnp.where(t == seq_len - 1, 0.0, pltpu.roll(y, m - 1, 0))
    cw = convw_ref[...]
    conv = cw[0:1, :] * y_prev + cw[1:2, :] * y + cw[2:3, :] * y_next
    conv_scr[...] = (bg * conv).astype(BF16)


def _merge_and_norm(x, mod, attT_scr, conv_scr, wout_ref, g_ref, b_ref, o_ref):
    ga = mod[:, 2 * D_MODEL:3 * D_MODEL]
    out = []
    for r in range(0, x.shape[0], MERGE_ROWS):
        rows = slice(r, r + MERGE_ROWS)
        att = attT_scr[:, rows].T.astype(BF16)
        mix = _dot(att, wout_ref[0:ATT_WIDTH, :]) + _dot(conv_scr[rows, :], wout_ref[ATT_WIDTH:, :])
        out.append(_layer_norm(ALPHA * x[rows, :] + ga * mix, g_ref[...], b_ref[...]))
    o_ref[...] = jnp.concatenate(out, axis=0).reshape(o_ref.shape)


def _softmax_keys(parts):
    mx = functools.reduce(jnp.maximum, [jnp.max(p, axis=0, keepdims=True) for p in parts])
    es = [jnp.exp(p - mx) for p in parts]
    inv = 1.0 / functools.reduce(jnp.add, [jnp.sum(e, axis=0, keepdims=True) for e in es])
    return [e.astype(BF16) for e in es], inv


def _head_rows(hd):
    return pl.ds(pl.multiple_of(hd * HEAD_DIM, HEAD_DIM), HEAD_DIM)


def _weight_spec(shape, index_map):
    return pl.BlockSpec(shape, index_map, pipeline_mode=pl.Buffered(1))


def _ctx_mixer_kernel(x_ref, mod_ref, win_ref, wout_ref, convw_ref, g_ref, b_ref, kT_in, vT_in,
                      o_ref, kT_ref, vT_ref, qT_scr, k_scr, vT_scr, attT_scr, conv_scr, sT_scr, pT_scr, inv_scr):
    del kT_in, vT_in
    sb, seq_len, _ = x_ref.shape
    x = x_ref[...].reshape(sb * seq_len, D_MODEL)
    mod = mod_ref[0:1, :]
    _project(x, mod, win_ref, convw_ref, qT_scr, k_scr, vT_scr, conv_scr, seq_len, kT_ref, vT_ref)

    pairs = [(hd, s) for hd in range(N_HEADS) for s in range(sb)]
    chan = lambda hd: slice(hd * HEAD_DIM, (hd + 1) * HEAD_DIM)
    tok = lambda s: slice(s * seq_len, (s + 1) * seq_len)
    for i, (hd, s) in enumerate(pairs):
        sT_scr[i] = _dot(k_scr[hd, tok(s), :], qT_scr[chan(hd), tok(s)])
    for i in range(len(pairs)):
        (pT_scr[i],), inv_scr[i] = _softmax_keys([sT_scr[i]])
    for i, (hd, s) in enumerate(pairs):
        attT_scr[chan(hd), tok(s)] = _dot(vT_scr[chan(hd), tok(s)], pT_scr[i]) * inv_scr[i]

    _merge_and_norm(x, mod, attT_scr, conv_scr, wout_ref, g_ref, b_ref, o_ref)


def _ctx_mixer(l, xp, mod, w_in, w_out, conv_w, ln_g, ln_b, kv_bufs):
    batch, seq_len, _ = xp.shape
    sb = CTX_SEQ_PER_STEP
    m = sb * seq_len
    kv_shape = jax.ShapeDtypeStruct((batch, DEPTH, N_HEADS, HEAD_DIM, seq_len), F32)
    kv_spec = pl.BlockSpec((sb, None, N_HEADS, HEAD_DIM, seq_len), lambda i: (i, l, 0, 0, 0))
    in_specs = [
        pl.BlockSpec((sb, seq_len, D_MODEL), lambda i: (i, 0, 0)),
        pl.BlockSpec((None, SUBLANES, 6 * D_MODEL), lambda i: (l, 0, 0)),
        _weight_spec((None, D_MODEL, 3 * ATT_WIDTH + 3 * CONV_WIDTH), lambda i: (l, 0, 0)),
        _weight_spec((None, D_MODEL, D_MODEL), lambda i: (l, 0, 0)),
        pl.BlockSpec((None, 3, CONV_WIDTH), lambda i: (l, 0, 0)),
        pl.BlockSpec((None, 1, D_MODEL), lambda i: (l, 0, 0)),
        pl.BlockSpec((None, 1, D_MODEL), lambda i: (l, 0, 0)),
        pl.BlockSpec(memory_space=pl.ANY),
        pl.BlockSpec(memory_space=pl.ANY),
    ]
    args = [xp, mod, w_in, w_out, conv_w, ln_g, ln_b, *kv_bufs]
    return pl.pallas_call(
        _ctx_mixer_kernel,
        grid=(batch // sb,),
        in_specs=in_specs,
        out_specs=[pl.BlockSpec((sb, seq_len, D_MODEL), lambda i: (i, 0, 0)), kv_spec, kv_spec],
        out_shape=[jax.ShapeDtypeStruct(xp.shape, F32), kv_shape, kv_shape],
        input_output_aliases={len(args) - 2: 1, len(args) - 1: 2},
        scratch_shapes=[
            pltpu.VMEM((ATT_WIDTH, m), BF16),
            pltpu.VMEM((N_HEADS, m, HEAD_DIM), BF16),
            pltpu.VMEM((ATT_WIDTH, m), BF16),
            pltpu.VMEM((ATT_WIDTH, m), F32),
            pltpu.VMEM((m, CONV_WIDTH), BF16),
            pltpu.VMEM((N_HEADS * sb, seq_len, seq_len), F32),
            pltpu.VMEM((N_HEADS * sb, seq_len, seq_len), BF16),
            pltpu.VMEM((N_HEADS * sb, 1, seq_len), F32),
        ],
        compiler_params=pltpu.CompilerParams(
            dimension_semantics=("arbitrary",), vmem_limit_bytes=VMEM_LIMIT_BYTES),
        name="ctx_mixer",
    )(*args)


def _window_start(r, rows):
    return min(max(r - WIN_ROWS // 2, 0), rows - WIN_ROWS)


def _chunk_key_rows(r0, chunk_rows, rows):
    lo = _window_start(r0, rows) // 2 * 2
    hi = -(-(_window_start(r0 + chunk_rows - 1, rows) + WIN_ROWS) // 2) * 2
    return lo, hi


def _lat_mixer_kernel(x_ref, mod_ref, win_ref, wout_ref, convw_ref, g_ref, b_ref,
                      ckT_ref, cvT_ref, tbl_ref, o_ref,
                      qT_scr, k_scr, vT_scr, attT_scr, conv_scr, sT_scr, pT_scr, inv_scr):
    b = pl.program_id(0)
    seq_len = x_ref.shape[0]
    rows = seq_len // GRID_W
    mod = mod_ref[pl.ds(1 + b, 1), :]
    _project(x_ref[...], mod, win_ref, convw_ref, qT_scr, k_scr, vT_scr, conv_scr, seq_len)
    low_half = lax.broadcasted_iota(jnp.int32, (GRID_W, 2 * GRID_W), 1) < GRID_W

    def bias_block(hd, r_pair, rk):
        inside = [_window_start(r, rows) <= rk < _window_start(r, rows) + WIN_ROWS for r in (r_pair, r_pair + 1)]
        d = rk - r_pair + WIN_ROWS - 1
        neg = jnp.full((GRID_W, 2 * GRID_W), NEG_INF, F32)
        if not any(inside):
            return neg
        blk = tbl_ref[hd, d * GRID_W:(d + 1) * GRID_W, :]
        if all(inside):
            return blk
        return jnp.where(low_half, blk, neg) if inside[0] else jnp.where(low_half, neg, blk)

    chunk_rows = LAT_Q_CHUNK // GRID_W
    past = ckT_ref.shape[-1]
    max_loc = sT_scr.shape[1] - past
    chunks = []
    for qc in range(seq_len // LAT_Q_CHUNK):
        r0 = qc * chunk_rows
        k_lo, k_hi = _chunk_key_rows(r0, chunk_rows, rows)
        chunks.append((qc, r0, k_lo, k_hi, slice(qc * LAT_Q_CHUNK, (qc + 1) * LAT_Q_CHUNK)))
    for hd in range(N_HEADS):
        chan = slice(hd * HEAD_DIM, (hd + 1) * HEAD_DIM)
        ckT = ckT_ref[hd]
        ck = jnp.concatenate([ckT, jnp.zeros_like(ckT)], axis=0).T.astype(BF16)
        cvT = cvT_ref[hd].astype(BF16)
        for qc, r0, k_lo, k_hi, qcols in chunks:
            qT = qT_scr[chan, qcols]
            bias = jnp.concatenate(
                [jnp.concatenate([bias_block(hd, r0 + j, rk) for j in range(0, chunk_rows, 2)], axis=1)
                 for rk in range(k_lo, k_hi)], axis=0)
            sT_scr[qc, 0:(k_hi - k_lo) * GRID_W] = _dot(k_scr[hd, k_lo * GRID_W:k_hi * GRID_W, :], qT) + bias
            sT_scr[qc, max_loc:] = _dot(ck, jnp.concatenate([qT, jnp.zeros_like(qT)], axis=0))
        for qc, r0, k_lo, k_hi, qcols in chunks:
            n_loc = (k_hi - k_lo) * GRID_W
            (pT_scr[qc, 0:n_loc], pT_scr[qc, max_loc:]), inv_scr[qc] = _softmax_keys(
                [sT_scr[qc, 0:n_loc], sT_scr[qc, max_loc:]])
        for qc, r0, k_lo, k_hi, qcols in chunks:
            n_loc = (k_hi - k_lo) * GRID_W
            attT_scr[chan, qcols] = (_dot(vT_scr[chan, k_lo * GRID_W:k_hi * GRID_W], pT_scr[qc, 0:n_loc])
                                     + _dot(cvT, pT_scr[qc, max_loc:])) * inv_scr[qc]

    _merge_and_norm(x_ref[...], mod, attT_scr, conv_scr, wout_ref, g_ref, b_ref, o_ref)


def _lat_mixer(l, xs, mod, w_in, w_out, conv_w, ln_g, ln_b, cache_kT, cache_vT, tbl):
    batch, seq_len, _ = xs.shape
    past = cache_kT.shape[-1]
    rows, chunk_rows, n_chunks = seq_len // GRID_W, LAT_Q_CHUNK // GRID_W, seq_len // LAT_Q_CHUNK
    max_loc = GRID_W * max(hi - lo for lo, hi in
                           (_chunk_key_rows(qc * chunk_rows, chunk_rows, rows) for qc in range(n_chunks)))
    cache_spec = pl.BlockSpec((None, None, N_HEADS, HEAD_DIM, past), lambda b: (b, l, 0, 0, 0))
    return pl.pallas_call(
        _lat_mixer_kernel,
        grid=(batch,),
        in_specs=[
            pl.BlockSpec((None, seq_len, D_MODEL), lambda b: (b, 0, 0)),
            pl.BlockSpec((None, SUBLANES, 6 * D_MODEL), lambda b: (l, 0, 0)),
            _weight_spec((None, D_MODEL, 3 * ATT_WIDTH + 3 * CONV_WIDTH), lambda b: (l, 0, 0)),
            _weight_spec((None, D_MODEL, D_MODEL), lambda b: (l, 0, 0)),
            pl.BlockSpec((None, 3, CONV_WIDTH), lambda b: (l, 0, 0)),
            pl.BlockSpec((None, 1, D_MODEL), lambda b: (l, 0, 0)),
            pl.BlockSpec((None, 1, D_MODEL), lambda b: (l, 0, 0)),
            cache_spec, cache_spec,
            _weight_spec((None,) + tbl.shape[1:], lambda b: (l, 0, 0, 0)),
        ],
        out_specs=pl.BlockSpec((None, seq_len, D_MODEL), lambda b: (b, 0, 0)),
        out_shape=jax.ShapeDtypeStruct(xs.shape, F32),
        scratch_shapes=[
            pltpu.VMEM((ATT_WIDTH, seq_len), BF16),
            pltpu.VMEM((N_HEADS, seq_len, HEAD_DIM), BF16),
            pltpu.VMEM((ATT_WIDTH, seq_len), BF16),
            pltpu.VMEM((ATT_WIDTH, seq_len), F32),
            pltpu.VMEM((seq_len, CONV_WIDTH), BF16),
            pltpu.VMEM((n_chunks, max_loc + past, LAT_Q_CHUNK), F32),
            pltpu.VMEM((n_chunks, max_loc + past, LAT_Q_CHUNK), BF16),
            pltpu.VMEM((n_chunks, 1, LAT_Q_CHUNK), F32),
        ],
        compiler_params=pltpu.CompilerParams(
            dimension_semantics=("arbitrary",), vmem_limit_bytes=VMEM_LIMIT_BYTES),
        name="lat_mixer",
    )(xs, mod, w_in, w_out, conv_w, ln_g, ln_b, cache_kT, cache_vT, tbl)


def _bias_table(rpb):
    depth, heads, n_dr, n_dc = rpb.shape
    n_blk = n_dr + 1
    lane0 = GRID_W - WIN_COLS
    n_pad = -(-n_dr // SUBLANES) * SUBLANES
    rpb_pad = jnp.pad(rpb[..., ::-1], ((0, 0), (0, 0), (0, n_pad - n_dr), (lane0, LANES - lane0 - n_dc)))
    rpb_pad = rpb_pad.reshape(depth * heads, n_pad, LANES)

    def body(r_ref, o_ref):
        cp = lax.broadcasted_iota(jnp.int32, (GRID_W, LANES), 0)
        lane = lax.broadcasted_iota(jnp.int32, (GRID_W, LANES), 1)
        low_half = lane < GRID_W
        c = jnp.where(low_half, lane, lane - GRID_W)
        col_start = jnp.clip(c - WIN_COLS // 2, 0, GRID_W - WIN_COLS)
        valid = (cp >= col_start) & (cp < col_start + WIN_COLS)
        def skewed(hd, dr, shift):
            row = jnp.broadcast_to(r_ref[hd, dr:dr + 1, :], (GRID_W, LANES))
            return pltpu.roll(row, shift, 1, stride=1, stride_axis=0)

        for hd in range(heads):
            for d in range(n_blk):
                lo = skewed(hd, d, LANES - GRID_W + 1) if d < n_dr else None
                hi = skewed(hd, d - 1, 1) if d >= 1 else None
                if lo is None:
                    blk = jnp.where(valid & ~low_half, hi, NEG_INF)
                elif hi is None:
                    blk = jnp.where(valid & low_half, lo, NEG_INF)
                else:
                    blk = jnp.where(valid, jnp.where(low_half, lo, hi), NEG_INF)
                o_ref[hd, d * GRID_W:(d + 1) * GRID_W, :] = blk

    tbl = pl.pallas_call(
        body,
        grid=(depth,),
        in_specs=[pl.BlockSpec((heads, n_pad, LANES), lambda i: (i, 0, 0))],
        out_specs=pl.BlockSpec((heads, n_blk * GRID_W, LANES), lambda i: (i, 0, 0)),
        out_shape=jax.ShapeDtypeStruct((depth * heads, n_blk * GRID_W, LANES), F32),
        compiler_params=pltpu.CompilerParams(dimension_semantics=("arbitrary",)),
        name="bias_table",
    )(rpb_pad)
    return tbl.reshape(depth, heads, n_blk * GRID_W, LANES)


def _route_t(logits):
    row = lax.broadcasted_iota(jnp.int32, logits.shape, 0)
    row_f = row.astype(F32)
    big = jnp.float32(LANES)

    def first_row(cond):
        return jnp.min(jnp.where(cond, row_f, big), axis=0, keepdims=True)

    gmask = (row >= N_EXPERTS) & (row < N_EXPERTS + N_GROUPS)
    gl = jnp.where(gmask, logits, NEG_INF)
    gexp = jnp.exp(gl - jnp.max(gl, axis=0, keepdims=True))
    gprob = gexp / jnp.sum(gexp, axis=0, keepdims=True)
    g_p = jnp.max(gprob, axis=0, keepdims=True)
    g_idx = first_row(gmask & (gprob == g_p)) - N_EXPERTS

    row_group = jnp.floor(row_f * (1.0 / EXPERTS_PER_GROUP))
    emask = (row < N_EXPERTS) & (row_group == g_idx)
    el = jnp.where(emask, logits, NEG_INF)
    eexp = jnp.exp(el - jnp.max(el, axis=0, keepdims=True))
    eprob = eexp / jnp.sum(eexp, axis=0, keepdims=True)
    p1 = jnp.max(eprob, axis=0, keepdims=True)
    i1 = first_row(emask & (eprob == p1))
    rest = emask & (row_f != i1)
    p2 = jnp.max(jnp.where(rest, eprob, -1.0), axis=0, keepdims=True)
    i2 = first_row(rest & (eprob == p2))
    denom = p1 + p2
    gate = (jnp.where(row_f == i1, g_p * p1 / denom, 0.0)
            + jnp.where(row_f == i2, g_p * p2 / denom, 0.0))
    return gate, g_idx


def _split3(v):
    hi = v.astype(BF16).astype(F32)
    mid = (v - hi).astype(BF16).astype(F32)
    return hi, mid, v - hi - mid


def _moe_route_tile(x_ref, mod_ref, wrT_ref, brT_ref, h_scr, rec_scr, recT_scr, o_ref, tier_smem):
    tm = x_ref.shape[0]
    seg = tm // mod_ref.shape[0]
    h = jnp.concatenate(
        [x_ref[s * seg:(s + 1) * seg, :] * (1.0 + mod_ref[s, :, 4 * D_MODEL:5 * D_MODEL])
         + mod_ref[s, :, 3 * D_MODEL:4 * D_MODEL] for s in range(tm // seg)], axis=0)
    h_hi = h.astype(BF16)
    h_lo = (h - h_hi.astype(F32)).astype(BF16)
    wr = wrT_ref[...]
    wr_hi = wr.astype(BF16)
    wr_lo = (wr - wr_hi.astype(F32)).astype(BF16)
    logits = (_dot_nt(wr_hi, h_hi) + _dot_nt(wr_hi, h_lo) + _dot_nt(wr_lo, h_hi))[0:ROUTE_ROWS, :]
    gate, g_idx = _route_t(logits + brT_ref[0:ROUTE_ROWS, :])

    row_f = lax.broadcasted_iota(jnp.int32, (ROUTE_ROWS, tm), 0).astype(F32)
    gate4 = jnp.concatenate(
        [jnp.sum(jnp.where(row_f == EXPERTS_PER_GROUP * g_idx + j, gate, 0.0), axis=0, keepdims=True)
         for j in range(EXPERTS_PER_GROUP)], axis=0)

    grp = lax.broadcasted_iota(jnp.int32, (SUBLANES, tm), 0).astype(F32)
    onehot = jnp.where(grp == g_idx, 1.0, 0.0)
    ri = lax.broadcasted_iota(jnp.int32, (MOE_BLOCK, MOE_BLOCK), 0)
    ci = lax.broadcasted_iota(jnp.int32, (MOE_BLOCK, MOE_BLOCK), 1)
    earlier = jnp.where(ri < ci, 1.0, 0.0).astype(BF16)
    ranks = []
    most = jnp.zeros((SUBLANES, 1), F32)
    for b in range(tm // MOE_BLOCK):
        oh_b = onehot[:, b * MOE_BLOCK:(b + 1) * MOE_BLOCK]
        ranks.append(_dot(oh_b.astype(BF16), earlier))
        most = jnp.maximum(most, jnp.sum(oh_b, axis=1, keepdims=True))
    rank = jnp.sum(onehot * jnp.concatenate(ranks, axis=1), axis=0, keepdims=True)

    recT = jnp.concatenate([*_split3(gate4), gate4, g_idx, rank,
                            jnp.zeros((LANES - REC_RANK - 1, tm), F32)], axis=0)
    rec = recT.T
    h_scr[:, 0:D_MODEL] = h_hi
    h_scr[:, D_MODEL:] = rec.astype(BF16)
    rec_scr[...] = rec
    recT_scr[...] = recT[REC_GROUP:REC_GROUP + SUBLANES, :]
    grp_row = lax.broadcasted_iota(jnp.int32, (SUBLANES, 1), 0)
    for gp in range(N_GROUPS):
        fullest = jnp.max(jnp.where(grp_row == gp, most, 0.0))
        tier_smem[gp] = sum((fullest > slot).astype(jnp.int32) for slot in MOE_SLOTS)
    o_ref[...] = jnp.zeros_like(o_ref)


def _moe_kernel(layer, x_ref, mod_ref, wrT_ref, brT_ref, win_hbm, wout_hbm, g_ref, b_ref, o_ref,
                h_scr, rec_scr, recT_scr, act_scr, win_buf, wout_buf, w_sem, tier_smem):
    g = pl.program_id(1)
    g_f = g.astype(F32)
    tm = x_ref.shape[0]
    n_blocks = tm // MOE_BLOCK
    step = pl.program_id(0) * N_GROUPS + g
    buf = step % 2

    def weight_copies(group, buf):
        experts_of_group = pl.ds(group * EXPERTS_PER_GROUP, EXPERTS_PER_GROUP)
        return (pltpu.make_async_copy(win_hbm.at[layer, experts_of_group], win_buf.at[buf], w_sem.at[0, buf]),
                pltpu.make_async_copy(wout_hbm.at[layer, group], wout_buf.at[buf], w_sem.at[1, buf]))

    @pl.when(step == 0)
    def _():
        for copy in weight_copies(0, 0):
            copy.start()

    @pl.when(step + 1 < pl.num_programs(0) * N_GROUPS)
    def _():
        for copy in weight_copies((g + 1) % N_GROUPS, 1 - buf):
            copy.start()

    @pl.when(g == 0)
    def _():
        _moe_route_tile(x_ref, mod_ref, wrT_ref, brT_ref, h_scr, rec_scr, recT_scr, o_ref, tier_smem)

    for copy in weight_copies(g, buf):
        copy.wait()
    win_ref = win_buf.at[buf]
    wout_ref = wout_buf.at[buf]

    def experts(xb, gates, gate_lane, rows):
        for e in range(EXPERTS_PER_GROUP):
            hid = _dot(xb, win_ref[e].astype(BF16))
            act = _silu(hid[:, :D_EXPERT]) * hid[:, D_EXPERT:] * gates[:, gate_lane + e:gate_lane + e + 1]
            act_scr[rows, e * D_EXPERT:(e + 1) * D_EXPERT] = act.astype(BF16)
        return _dot(act_scr[rows, :], wout_ref[...].astype(BF16))

    def compact(slot):
        n_rows = n_blocks * slot
        slot_row = lax.broadcasted_iota(jnp.int32, (slot, MOE_BLOCK), 0).astype(F32)
        gathered = []
        for b in range(n_blocks):
            tok = slice(b * MOE_BLOCK, (b + 1) * MOE_BLOCK)
            pick = (recT_scr[0:1, tok] == g_f) & (recT_scr[1:2, tok] == slot_row)
            gathered.append(_dot(jnp.where(pick, 1.0, 0.0).astype(BF16), h_scr[tok, :]))
        xg = jnp.concatenate(gathered, axis=0)
        ge = xg[:, D_MODEL:]
        gates = ((ge + pltpu.roll(ge, LANES - EXPERTS_PER_GROUP, 1))
                 + pltpu.roll(ge, LANES - 2 * EXPERTS_PER_GROUP, 1))
        y = experts(xg[:, :D_MODEL].astype(BF16), gates, REC_SPLIT, slice(0, n_rows)).astype(BF16)
        slot_col = lax.broadcasted_iota(jnp.int32, (MOE_BLOCK, SCATTER_K), 1).astype(F32)
        for b in range(n_blocks):
            tok = slice(b * MOE_BLOCK, (b + 1) * MOE_BLOCK)
            first = min(b * slot, n_rows - SCATTER_K)
            place = ((rec_scr[tok, REC_GROUP:REC_GROUP + 1] == g_f)
                     & (rec_scr[tok, REC_RANK:REC_RANK + 1] + (b * slot - first) == slot_col))
            o_ref[tok, :] += _dot(jnp.where(place, 1.0, 0.0).astype(BF16), y[first:first + SCATTER_K, :])

    for tier, slot in enumerate(MOE_SLOTS):
        pl.when(tier_smem[g] == tier)(functools.partial(compact, slot))

    @pl.when(tier_smem[g] == len(MOE_SLOTS))
    def _():
        for b in range(n_blocks):
            tok = slice(b * MOE_BLOCK, (b + 1) * MOE_BLOCK)
            gates = jnp.where(rec_scr[tok, REC_GROUP:REC_GROUP + 1] == g_f, rec_scr[tok, :], 0.0)
            o_ref[tok, :] += experts(h_scr[tok, 0:D_MODEL], gates, REC_GATE, slice(0, MOE_BLOCK))

    @pl.when(g == N_GROUPS - 1)
    def _():
        seg = tm // mod_ref.shape[0]
        for s in range(tm // seg):
            tok = slice(s * seg, (s + 1) * seg)
            gf = mod_ref[s, :, 5 * D_MODEL:6 * D_MODEL]
            o_ref[tok, :] = _layer_norm(ALPHA * x_ref[tok, :] + gf * o_ref[tok, :], g_ref[...], b_ref[...])


def _moe(l, x, mod_rows, w_routerT, b_routerT, w_exp_in, w_exp_out, ln_g, ln_b):
    n = x.shape[0]
    tm = MOE_TOKENS
    n_seg = mod_rows.shape[2]
    n_blocks = tm // MOE_BLOCK
    assert tm % MOE_BLOCK == 0
    for slot in MOE_SLOTS:
        assert slot % BF16_ROWS == 0 and slot <= SCATTER_K <= n_blocks * slot
    n_rows = n_blocks * max(MOE_SLOTS)
    d_act = EXPERTS_PER_GROUP * D_EXPERT
    return pl.pallas_call(
        functools.partial(_moe_kernel, l),
        grid=(n // tm, N_GROUPS),
        in_specs=[
            pl.BlockSpec((tm, D_MODEL), lambda i, g: (i, 0)),
            pl.BlockSpec((None, None, n_seg, 1, 6 * D_MODEL), lambda i, g: (l, i, 0, 0, 0)),
            pl.BlockSpec((None, LANES, D_MODEL), lambda i, g: (l, 0, 0)),
            pl.BlockSpec((None, LANES, 1), lambda i, g: (l, 0, 0)),
            pl.BlockSpec(memory_space=pl.ANY),
            pl.BlockSpec(memory_space=pl.ANY),
            pl.BlockSpec((None, 1, D_MODEL), lambda i, g: (l, 0, 0)),
            pl.BlockSpec((None, 1, D_MODEL), lambda i, g: (l, 0, 0)),
        ],
        out_specs=pl.BlockSpec((tm, D_MODEL), lambda i, g: (i, 0)),
        out_shape=jax.ShapeDtypeStruct(x.shape, F32),
        scratch_shapes=[
            pltpu.VMEM((tm, D_MODEL + GATE_COLS), BF16),
            pltpu.VMEM((tm, LANES), F32),
            pltpu.VMEM((SUBLANES, tm), F32),
            pltpu.VMEM((max(n_rows, MOE_BLOCK), d_act), BF16),
            pltpu.VMEM((2, EXPERTS_PER_GROUP, D_MODEL, 2 * D_EXPERT), F32),
            pltpu.VMEM((2, d_act, D_MODEL), F32),
            pltpu.SemaphoreType.DMA((2, 2)),
            pltpu.SMEM((N_GROUPS,), jnp.int32),
        ],
        compiler_params=pltpu.CompilerParams(
            dimension_semantics=("arbitrary", "arbitrary"), vmem_limit_bytes=VMEM_LIMIT_BYTES),
        name="hier_moe",
    )(x, mod_rows, w_routerT, b_routerT, w_exp_in, w_exp_out, ln_g, ln_b)


def kernel(x_prompt, x_sample, cache_k, cache_v, c, c_ctx, w_ada, b_ada, w_in, conv_w, rpb, w_out,
           ln1_g, ln1_b, w_router_group, b_router_group, w_router_expert, b_router_expert,
           w_expert_in, w_expert_out, ln2_g, ln2_b):
    batch, seq_len, _ = x_prompt.shape
    dec_batch, dec_seq, _ = x_sample.shape
    assert dec_batch + 1 <= SUBLANES and MOE_TOKENS % dec_seq == 0
    assert (dec_batch * dec_seq) % MOE_TOKENS == 0 and (batch * seq_len) % MOE_TOKENS == 0

    cond = jnp.concatenate([c_ctx[None, :], c, jnp.zeros((SUBLANES - 1 - dec_batch, D_MODEL), F32)], axis=0)
    mod = _modulation(cond, w_ada, b_ada)
    n_ctx_tiles = batch * seq_len // MOE_TOKENS
    mod_rows_ctx = jnp.broadcast_to(mod[:, 0:1, None, None, :], (DEPTH, n_ctx_tiles, 1, 1, 6 * D_MODEL))
    mod_rows_lat = mod[:, 1:1 + dec_batch].reshape(DEPTH, -1, MOE_TOKENS // dec_seq, 1, 6 * D_MODEL)

    w_in_bf = w_in.astype(BF16)
    w_out_bf = w_out.astype(BF16)
    pad = jnp.zeros((DEPTH, LANES - N_EXPERTS - N_GROUPS, D_MODEL), F32)
    w_routerT = jnp.concatenate(
        [jnp.swapaxes(w_router_expert, 1, 2), jnp.swapaxes(w_router_group, 1, 2), pad], axis=1)
    b_routerT = jnp.concatenate([b_router_expert, b_router_group, pad[:, :, 0]], axis=-1)[:, :, None]
    tbl = _bias_table(rpb)
    cache_kT = jnp.swapaxes(cache_k, -1, -2)
    cache_vT = jnp.swapaxes(cache_v, -1, -2)
    ln1_g3, ln1_b3 = ln1_g[:, None, :], ln1_b[:, None, :]
    ln2_g3, ln2_b3 = ln2_g[:, None, :], ln2_b[:, None, :]

    w_exp_out = w_expert_out.reshape(DEPTH, N_GROUPS, EXPERTS_PER_GROUP * D_EXPERT, D_MODEL)

    xp, xs = x_prompt, x_sample
    kv_bufs = [jnp.zeros((batch, DEPTH, N_HEADS, HEAD_DIM, seq_len), F32)] * 2
    for l in range(DEPTH):
        moe = functools.partial(_moe, l, w_routerT=w_routerT, b_routerT=b_routerT, w_exp_in=w_expert_in,
                                w_exp_out=w_exp_out, ln_g=ln2_g3, ln_b=ln2_b3)
        xp, *kv_bufs = _ctx_mixer(l, xp, mod, w_in_bf, w_out_bf, conv_w, ln1_g3, ln1_b3, kv_bufs)
        xp = moe(xp.reshape(-1, D_MODEL), mod_rows_ctx).reshape(xp.shape)
        xs = _lat_mixer(l, xs, mod, w_in_bf, w_out_bf, conv_w, ln1_g3, ln1_b3, cache_kT, cache_vT, tbl)
        xs = moe(xs.reshape(-1, D_MODEL), mod_rows_lat).reshape(xs.shape)
    new_kT, new_vT = kv_bufs
    return (xp, xs, jnp.swapaxes(new_kT, -1, -2), jnp.swapaxes(new_vT, -1, -2))
```

```python
import functools

import jax
import jax.numpy as jnp
from jax import lax
from jax.experimental import pallas as pl
from jax.experimental.pallas import tpu as pltpu

D_MODEL = 1024
DEPTH = 4
GRID_W = 64
ATT_WIDTH = D_MODEL // 2
CONV_WIDTH = D_MODEL - ATT_WIDTH
HEAD_DIM = 64
N_HEADS = ATT_WIDTH // HEAD_DIM
WIN_ROWS = 8
WIN_COLS = 16
N_GROUPS = 4
EXPERTS_PER_GROUP = 4
N_EXPERTS = N_GROUPS * EXPERTS_PER_GROUP
D_EXPERT = D_MODEL // 4
ALPHA = (2 * DEPTH) ** 0.25
LN_EPS = 1e-5
NEG_INF = -1e30
QK_SCALE = HEAD_DIM ** -0.5

F32 = jnp.float32
BF16 = jnp.bfloat16

LANES = 128
SUBLANES = 8
BF16_ROWS = 16
VMEM_LIMIT_BYTES = 56 * 1024 * 1024

CTX_SEQ_PER_STEP = 2
LAT_Q_CHUNK = 128
LAT_HEADS_PER_STEP = 2
MERGE_ROWS = 256
MOE_TOKENS = 1024
MOE_BLOCK = 512
MOE_SLOTS = (144, 192, 256)
SCATTER_K = 256
GATE_COLS = 128
ROUTE_ROWS = 24
REC_SPLIT, REC_GATE, REC_GROUP, REC_RANK = 0, 12, 16, 17
MOD_COLS = 2048


def _dot(a, b):
    return jnp.dot(a, b, preferred_element_type=F32)


def _dot_nt(a, b):
    return lax.dot_general(a, b, (((1,), (1,)), ((), ())), preferred_element_type=F32)


def _silu(x):
    return x * (1.0 / (1.0 + jnp.exp(-x)))


def _layer_norm(r, g, b):
    mu = jnp.mean(r, axis=-1, keepdims=True)
    d = r - mu
    var = jnp.mean(d * d, axis=-1, keepdims=True)
    return d * lax.rsqrt(var + LN_EPS) * g + b


def _mod_kernel(cond_ref, w_ref, b_ref, o_ref):
    s = _silu(cond_ref[...]).astype(BF16)
    o_ref[...] = _dot(s, w_ref[...].astype(BF16)) + b_ref[...]


def _modulation(cond, w_ada, b_ada):
    n_out = w_ada.shape[-1]
    return pl.pallas_call(
        _mod_kernel,
        grid=(DEPTH, n_out // MOD_COLS),
        in_specs=[
            pl.BlockSpec((SUBLANES, D_MODEL), lambda l, j: (0, 0)),
            pl.BlockSpec((None, D_MODEL, MOD_COLS), lambda l, j: (l, 0, j)),
            pl.BlockSpec((None, 1, MOD_COLS), lambda l, j: (l, 0, j)),
        ],
        out_specs=pl.BlockSpec((None, SUBLANES, MOD_COLS), lambda l, j: (l, 0, j)),
        out_shape=jax.ShapeDtypeStruct((DEPTH, SUBLANES, n_out), F32),
        compiler_params=pltpu.CompilerParams(
            dimension_semantics=("arbitrary", "arbitrary"),
            vmem_limit_bytes=VMEM_LIMIT_BYTES),
        name="adaln_modulation",
    )(cond, w_ada, b_ada.reshape(DEPTH, 1, n_out))


def _project(x, mod, win_ref, convw_ref, qT_scr, k_scr, vT_scr, conv_scr, seq_len, kT_out=None, vT_out=None):
    m = x.shape[0]
    sa = mod[:, 0:D_MODEL]
    ca = mod[:, D_MODEL:2 * D_MODEL]
    h = (x * (1.0 + ca) + sa).astype(BF16)

    qT_scr[...] = (_dot(h, win_ref[:, 0:ATT_WIDTH]) * QK_SCALE).T.astype(BF16)
    zk = _dot(h, win_ref[:, ATT_WIDTH:2 * ATT_WIDTH])
    for hd in range(N_HEADS):
        k_scr[hd] = zk[:, hd * HEAD_DIM:(hd + 1) * HEAD_DIM].astype(BF16)
    zvT = _dot(h, win_ref[:, 2 * ATT_WIDTH:3 * ATT_WIDTH]).T
    vT_scr[...] = zvT.astype(BF16)
    if kT_out is not None:
        zkT = zk.T
        for out, zT in ((kT_out, zkT), (vT_out, zvT)):
            for s in range(m // seq_len):
                for hd in range(N_HEADS):
                    out[s, hd] = zT[hd * HEAD_DIM:(hd + 1) * HEAD_DIM, s * seq_len:(s + 1) * seq_len]

    c0 = 3 * ATT_WIDTH
    bg = _dot(h, win_ref[:, c0:c0 + CONV_WIDTH])
    cg = _dot(h, win_ref[:, c0 + CONV_WIDTH:c0 + 2 * CONV_WIDTH])
    u = _dot(h, win_ref[:, c0 + 2 * CONV_WIDTH:c0 + 3 * CONV_WIDTH])
    y = cg * u
    t = lax.broadcasted_iota(jnp.int32, (m, 1), 0) % seq_len
    y_prev = jnp.where(t == 0, 0.0, pltpu.roll(y, 1, 0))
    y_next = jnp.where(t == seq_len - 1, 0.0, pltpu.roll(y, m - 1, 0))
    cw = convw_ref[...]
    conv = cw[0:1, :] * y_prev + cw[1:2, :] * y + cw[2:3, :] * y_next
    conv_scr[...] = (bg * conv).astype(BF16)


def _merge_and_norm(x, mod, attT_scr, conv_scr, wout_ref, g_ref, b_ref, o_ref):
    ga = mod[:, 2 * D_MODEL:3 * D_MODEL]
    out = []
    for r in range(0, x.shape[0], MERGE_ROWS):
        rows = slice(r, r + MERGE_ROWS)
        att = attT_scr[:, rows].T.astype(BF16)
        mix = _dot(att, wout_ref[0:ATT_WIDTH, :]) + _dot(conv_scr[rows, :], wout_ref[ATT_WIDTH:, :])
        out.append(_layer_norm(ALPHA * x[rows, :] + ga * mix, g_ref[...], b_ref[...]))
    o_ref[...] = jnp.concatenate(out, axis=0).reshape(o_ref.shape)


def _softmax_keys(parts):
    mx = functools.reduce(jnp.maximum, [jnp.max(p, axis=0, keepdims=True) for p in parts])
    es = [jnp.exp(p - mx) for p in parts]
    inv = 1.0 / functools.reduce(jnp.add, [jnp.sum(e, axis=0, keepdims=True) for e in es])
    return [e.astype(BF16) for e in es], inv


def _head_rows(hd):
    return pl.ds(pl.multiple_of(hd * HEAD_DIM, HEAD_DIM), HEAD_DIM)


def _weight_spec(shape, index_map):
    return pl.BlockSpec(shape, index_map, pipeline_mode=pl.Buffered(1))


def _ctx_mixer_kernel(x_ref, mod_ref, win_ref, wout_ref, convw_ref, g_ref, b_ref, kT_in, vT_in,
                      o_ref, kT_ref, vT_ref, qT_scr, k_scr, vT_scr, attT_scr, conv_scr, sT_scr, pT_scr, inv_scr):
    del kT_in, vT_in
    sb, seq_len, _ = x_ref.shape
    x = x_ref[...].reshape(sb * seq_len, D_MODEL)
    mod = mod_ref[0:1, :]
    _project(x, mod, win_ref, convw_ref, qT_scr, k_scr, vT_scr, conv_scr, seq_len, kT_ref, vT_ref)

    pairs = [(hd, s) for hd in range(N_HEADS) for s in range(sb)]
    chan = lambda hd: slice(hd * HEAD_DIM, (hd + 1) * HEAD_DIM)
    tok = lambda s: slice(s * seq_len, (s + 1) * seq_len)
    for i, (hd, s) in enumerate(pairs):
        sT_scr[i] = _dot(k_scr[hd, tok(s), :], qT_scr[chan(hd), tok(s)])
    for i in range(len(pairs)):
        (pT_scr[i],), inv_scr[i] = _softmax_keys([sT_scr[i]])
    for i, (hd, s) in enumerate(pairs):
        attT_scr[chan(hd), tok(s)] = _dot(vT_scr[chan(hd), tok(s)], pT_scr[i]) * inv_scr[i]

    _merge_and_norm(x, mod, attT_scr, conv_scr, wout_ref, g_ref, b_ref, o_ref)


def _ctx_mixer(l, xp, mod, w_in, w_out, conv_w, ln_g, ln_b, kv_bufs):
    batch, seq_len, _ = xp.shape
    sb = CTX_SEQ_PER_STEP
    m = sb * seq_len
    kv_shape = jax.ShapeDtypeStruct((batch, DEPTH, N_HEADS, HEAD_DIM, seq_len), F32)
    kv_spec = pl.BlockSpec((sb, None, N_HEADS, HEAD_DIM, seq_len), lambda i: (i, l, 0, 0, 0))
    in_specs = [
        pl.BlockSpec((sb, seq_len, D_MODEL), lambda i: (i, 0, 0)),
        pl.BlockSpec((None, SUBLANES, 6 * D_MODEL), lambda i: (l, 0, 0)),
        _weight_spec((None, D_MODEL, 3 * ATT_WIDTH + 3 * CONV_WIDTH), lambda i: (l, 0, 0)),
        _weight_spec((None, D_MODEL, D_MODEL), lambda i: (l, 0, 0)),
        pl.BlockSpec((None, 3, CONV_WIDTH), lambda i: (l, 0, 0)),
        pl.BlockSpec((None, 1, D_MODEL), lambda i: (l, 0, 0)),
        pl.BlockSpec((None, 1, D_MODEL), lambda i: (l, 0, 0)),
        pl.BlockSpec(memory_space=pl.ANY),
        pl.BlockSpec(memory_space=pl.ANY),
    ]
    args = [xp, mod, w_in, w_out, conv_w, ln_g, ln_b, *kv_bufs]
    return pl.pallas_call(
        _ctx_mixer_kernel,
        grid=(batch // sb,),
        in_specs=in_specs,
        out_specs=[pl.BlockSpec((sb, seq_len, D_MODEL), lambda i: (i, 0, 0)), kv_spec, kv_spec],
        out_shape=[jax.ShapeDtypeStruct(xp.shape, F32), kv_shape, kv_shape],
        input_output_aliases={len(args) - 2: 1, len(args) - 1: 2},
        scratch_shapes=[
            pltpu.VMEM((ATT_WIDTH, m), BF16),
            pltpu.VMEM((N_HEADS, m, HEAD_DIM), BF16),
            pltpu.VMEM((ATT_WIDTH, m), BF16),
            pltpu.VMEM((ATT_WIDTH, m), F32),
            pltpu.VMEM((m, CONV_WIDTH), BF16),
            pltpu.VMEM((N_HEADS * sb, seq_len, seq_len), F32),
            pltpu.VMEM((N_HEADS * sb, seq_len, seq_len), BF16),
            pltpu.VMEM((N_HEADS * sb, 1, seq_len), F32),
        ],
        compiler_params=pltpu.CompilerParams(
            dimension_semantics=("arbitrary",), vmem_limit_bytes=VMEM_LIMIT_BYTES),
        name="ctx_mixer",
    )(*args)


def _window_start(r, rows):
    return min(max(r - WIN_ROWS // 2, 0), rows - WIN_ROWS)


def _chunk_key_rows(r0, chunk_rows, rows):
    lo = _window_start(r0, rows) // 2 * 2
    hi = -(-(_window_start(r0 + chunk_rows - 1, rows) + WIN_ROWS) // 2) * 2
    return lo, hi


def _lat_mixer_kernel(x_ref, mod_ref, win_ref, wout_ref, convw_ref, g_ref, b_ref,
                      ckT_ref, cvT_ref, tbl_ref, o_ref,
                      qT_scr, k_scr, vT_scr, attT_scr, conv_scr, sT_scr, pT_scr, inv_scr):
    b = pl.program_id(0)
    step = pl.program_id(1)
    seq_len = x_ref.shape[0]
    rows = seq_len // GRID_W
    mod = mod_ref[pl.ds(1 + b, 1), :]

    @pl.when(step == 0)
    def _():
        _project(x_ref[...], mod, win_ref, convw_ref, qT_scr, k_scr, vT_scr, conv_scr, seq_len)

    low_half = lax.broadcasted_iota(jnp.int32, (GRID_W, 2 * GRID_W), 1) < GRID_W

    def bias_block(j, r_pair, rk):
        inside = [_window_start(r, rows) <= rk < _window_start(r, rows) + WIN_ROWS for r in (r_pair, r_pair + 1)]
        d = rk - r_pair + WIN_ROWS - 1
        neg = jnp.full((GRID_W, 2 * GRID_W), NEG_INF, F32)
        if not any(inside):
            return neg
        blk = tbl_ref[j, d * GRID_W:(d + 1) * GRID_W, :]
        if all(inside):
            return blk
        return jnp.where(low_half, blk, neg) if inside[0] else jnp.where(low_half, neg, blk)

    chunk_rows = LAT_Q_CHUNK // GRID_W
    n_chunks = seq_len // LAT_Q_CHUNK
    past = ckT_ref.shape[-1]
    max_loc = sT_scr.shape[1] - past
    chunks = []
    for qc in range(n_chunks):
        r0 = qc * chunk_rows
        k_lo, k_hi = _chunk_key_rows(r0, chunk_rows, rows)
        chunks.append((qc, r0, k_lo, k_hi, slice(qc * LAT_Q_CHUNK, (qc + 1) * LAT_Q_CHUNK)))
    for j in range(LAT_HEADS_PER_STEP):
        hd = step * LAT_HEADS_PER_STEP + j
        chan = _head_rows(hd)
        ckT = ckT_ref[j]
        ck = jnp.concatenate([ckT, jnp.zeros_like(ckT)], axis=0).T.astype(BF16)
        cvT = cvT_ref[j].astype(BF16)
        for qc, r0, k_lo, k_hi, qcols in chunks:
            i = j * n_chunks + qc
            qT = qT_scr[chan, qcols]
            bias = jnp.concatenate(
                [jnp.concatenate([bias_block(j, r0 + jj, rk) for jj in range(0, chunk_rows, 2)], axis=1)
                 for rk in range(k_lo, k_hi)], axis=0)
            sT_scr[i, 0:(k_hi - k_lo) * GRID_W] = _dot(k_scr[hd, k_lo * GRID_W:k_hi * GRID_W, :], qT) + bias
            sT_scr[i, max_loc:] = _dot(ck, jnp.concatenate([qT, jnp.zeros_like(qT)], axis=0))
        for qc, r0, k_lo, k_hi, qcols in chunks:
            i = j * n_chunks + qc
            n_loc = (k_hi - k_lo) * GRID_W
            (pT_scr[i, 0:n_loc], pT_scr[i, max_loc:]), inv_scr[i] = _softmax_keys(
                [sT_scr[i, 0:n_loc], sT_scr[i, max_loc:]])
        for qc, r0, k_lo, k_hi, qcols in chunks:
            i = j * n_chunks + qc
            n_loc = (k_hi - k_lo) * GRID_W
            attT_scr[chan, qcols] = (_dot(vT_scr[chan, k_lo * GRID_W:k_hi * GRID_W], pT_scr[i, 0:n_loc])
                                     + _dot(cvT, pT_scr[i, max_loc:])) * inv_scr[i]

    @pl.when(step == pl.num_programs(1) - 1)
    def _():
        _merge_and_norm(x_ref[...], mod, attT_scr, conv_scr, wout_ref, g_ref, b_ref, o_ref)


def _lat_mixer(l, xs, mod, w_in, w_out, conv_w, ln_g, ln_b, cache_kT, cache_vT, tbl):
    batch, seq_len, _ = xs.shape
    past = cache_kT.shape[-1]
    rows, chunk_rows, n_chunks = seq_len // GRID_W, LAT_Q_CHUNK // GRID_W, seq_len // LAT_Q_CHUNK
    max_loc = GRID_W * max(hi - lo for lo, hi in
                           (_chunk_key_rows(qc * chunk_rows, chunk_rows, rows) for qc in range(n_chunks)))
    hps = LAT_HEADS_PER_STEP
    cache_spec = pl.BlockSpec((None, None, hps, HEAD_DIM, past), lambda b, h: (b, l, h, 0, 0))
    return pl.pallas_call(
        _lat_mixer_kernel,
        grid=(batch, N_HEADS // hps),
        in_specs=[
            pl.BlockSpec((None, seq_len, D_MODEL), lambda b, h: (b, 0, 0)),
            pl.BlockSpec((None, SUBLANES, 6 * D_MODEL), lambda b, h: (l, 0, 0)),
            _weight_spec((None, D_MODEL, 3 * ATT_WIDTH + 3 * CONV_WIDTH), lambda b, h: (l, 0, 0)),
            _weight_spec((None, D_MODEL, D_MODEL), lambda b, h: (l, 0, 0)),
            pl.BlockSpec((None, 3, CONV_WIDTH), lambda b, h: (l, 0, 0)),
            pl.BlockSpec((None, 1, D_MODEL), lambda b, h: (l, 0, 0)),
            pl.BlockSpec((None, 1, D_MODEL), lambda b, h: (l, 0, 0)),
            cache_spec, cache_spec,
            pl.BlockSpec((None, hps) + tbl.shape[2:], lambda b, h: (l, h, 0, 0)),
        ],
        out_specs=pl.BlockSpec((None, seq_len, D_MODEL), lambda b, h: (b, 0, 0)),
        out_shape=jax.ShapeDtypeStruct(xs.shape, F32),
        scratch_shapes=[
            pltpu.VMEM((ATT_WIDTH, seq_len), BF16),
            pltpu.VMEM((N_HEADS, seq_len, HEAD_DIM), BF16),
            pltpu.VMEM((ATT_WIDTH, seq_len), BF16),
            pltpu.VMEM((ATT_WIDTH, seq_len), F32),
            pltpu.VMEM((seq_len, CONV_WIDTH), BF16),
            pltpu.VMEM((hps * n_chunks, max_loc + past, LAT_Q_CHUNK), F32),
            pltpu.VMEM((hps * n_chunks, max_loc + past, LAT_Q_CHUNK), BF16),
            pltpu.VMEM((hps * n_chunks, 1, LAT_Q_CHUNK), F32),
        ],
        compiler_params=pltpu.CompilerParams(
            dimension_semantics=("arbitrary", "arbitrary"), vmem_limit_bytes=VMEM_LIMIT_BYTES),
        name="lat_mixer",
    )(xs, mod, w_in, w_out, conv_w, ln_g, ln_b, cache_kT, cache_vT, tbl)


def _bias_table(rpb):
    depth, heads, n_dr, n_dc = rpb.shape
    n_blk = n_dr + 1
    lane0 = GRID_W - WIN_COLS
    n_pad = -(-n_dr // SUBLANES) * SUBLANES
    rpb_pad = jnp.pad(rpb[..., ::-1], ((0, 0), (0, 0), (0, n_pad - n_dr), (lane0, LANES - lane0 - n_dc)))
    rpb_pad = rpb_pad.reshape(depth * heads, n_pad, LANES)

    def body(r_ref, o_ref):
        cp = lax.broadcasted_iota(jnp.int32, (GRID_W, LANES), 0)
        lane = lax.broadcasted_iota(jnp.int32, (GRID_W, LANES), 1)
        low_half = lane < GRID_W
        c = jnp.where(low_half, lane, lane - GRID_W)
        col_start = jnp.clip(c - WIN_COLS // 2, 0, GRID_W - WIN_COLS)
        valid = (cp >= col_start) & (cp < col_start + WIN_COLS)
        def skewed(hd, dr, shift):
            row = jnp.broadcast_to(r_ref[hd, dr:dr + 1, :], (GRID_W, LANES))
            return pltpu.roll(row, shift, 1, stride=1, stride_axis=0)

        for hd in range(heads):
            for d in range(n_blk):
                lo = skewed(hd, d, LANES - GRID_W + 1) if d < n_dr else None
                hi = skewed(hd, d - 1, 1) if d >= 1 else None
                if lo is None:
                    blk = jnp.where(valid & ~low_half, hi, NEG_INF)
                elif hi is None:
                    blk = jnp.where(valid & low_half, lo, NEG_INF)
                else:
                    blk = jnp.where(valid, jnp.where(low_half, lo, hi), NEG_INF)
                o_ref[hd, d * GRID_W:(d + 1) * GRID_W, :] = blk

    tbl = pl.pallas_call(
        body,
        grid=(depth,),
        in_specs=[pl.BlockSpec((heads, n_pad, LANES), lambda i: (i, 0, 0))],
        out_specs=pl.BlockSpec((heads, n_blk * GRID_W, LANES), lambda i: (i, 0, 0)),
        out_shape=jax.ShapeDtypeStruct((depth * heads, n_blk * GRID_W, LANES), F32),
        compiler_params=pltpu.CompilerParams(dimension_semantics=("arbitrary",)),
        name="bias_table",
    )(rpb_pad)
    return tbl.reshape(depth, heads, n_blk * GRID_W, LANES)


def _route_t(logits):
    row = lax.broadcasted_iota(jnp.int32, logits.shape, 0)
    row_f = row.astype(F32)
    big = jnp.float32(LANES)

    def first_row(cond):
        return jnp.min(jnp.where(cond, row_f, big), axis=0, keepdims=True)

    gmask = (row >= N_EXPERTS) & (row < N_EXPERTS + N_GROUPS)
    gl = jnp.where(gmask, logits, NEG_INF)
    gexp = jnp.exp(gl - jnp.max(gl, axis=0, keepdims=True))
    gprob = gexp / jnp.sum(gexp, axis=0, keepdims=True)
    g_p = jnp.max(gprob, axis=0, keepdims=True)
    g_idx = first_row(gmask & (gprob == g_p)) - N_EXPERTS

    row_group = jnp.floor(row_f * (1.0 / EXPERTS_PER_GROUP))
    emask = (row < N_EXPERTS) & (row_group == g_idx)
    el = jnp.where(emask, logits, NEG_INF)
    eexp = jnp.exp(el - jnp.max(el, axis=0, keepdims=True))
    eprob = eexp / jnp.sum(eexp, axis=0, keepdims=True)
    p1 = jnp.max(eprob, axis=0, keepdims=True)
    i1 = first_row(emask & (eprob == p1))
    rest = emask & (row_f != i1)
    p2 = jnp.max(jnp.where(rest, eprob, -1.0), axis=0, keepdims=True)
    i2 = first_row(rest & (eprob == p2))
    denom = p1 + p2
    gate = (jnp.where(row_f == i1, g_p * p1 / denom, 0.0)
            + jnp.where(row_f == i2, g_p * p2 / denom, 0.0))
    return gate, g_idx


def _split3(v):
    hi = v.astype(BF16).astype(F32)
    mid = (v - hi).astype(BF16).astype(F32)
    return hi, mid, v - hi - mid


def _moe_route_tile(x_ref, mod_ref, wrT_ref, brT_ref, h_scr, rec_scr, recT_scr, o_ref, tier_smem):
    tm = x_ref.shape[0]
    seg = tm // mod_ref.shape[0]
    h = jnp.concatenate(
        [x_ref[s * seg:(s + 1) * seg, :] * (1.0 + mod_ref[s, :, 4 * D_MODEL:5 * D_MODEL])
         + mod_ref[s, :, 3 * D_MODEL:4 * D_MODEL] for s in range(tm // seg)], axis=0)
    h_hi = h.astype(BF16)
    h_lo = (h - h_hi.astype(F32)).astype(BF16)
    wr = wrT_ref[...]
    wr_hi = wr.astype(BF16)
    wr_lo = (wr - wr_hi.astype(F32)).astype(BF16)
    logits = (_dot_nt(wr_hi, h_hi) + _dot_nt(wr_hi, h_lo) + _dot_nt(wr_lo, h_hi))[0:ROUTE_ROWS, :]
    gate, g_idx = _route_t(logits + brT_ref[0:ROUTE_ROWS, :])

    row_f = lax.broadcasted_iota(jnp.int32, (ROUTE_ROWS, tm), 0).astype(F32)
    gate4 = jnp.concatenate(
        [jnp.sum(jnp.where(row_f == EXPERTS_PER_GROUP * g_idx + j, gate, 0.0), axis=0, keepdims=True)
         for j in range(EXPERTS_PER_GROUP)], axis=0)

    grp = lax.broadcasted_iota(jnp.int32, (SUBLANES, tm), 0).astype(F32)
    onehot = jnp.where(grp == g_idx, 1.0, 0.0)
    ri = lax.broadcasted_iota(jnp.int32, (MOE_BLOCK, MOE_BLOCK), 0)
    ci = lax.broadcasted_iota(jnp.int32, (MOE_BLOCK, MOE_BLOCK), 1)
    earlier = jnp.where(ri < ci, 1.0, 0.0).astype(BF16)
    ranks = []
    most = jnp.zeros((SUBLANES, 1), F32)
    for b in range(tm // MOE_BLOCK):
        oh_b = onehot[:, b * MOE_BLOCK:(b + 1) * MOE_BLOCK]
        ranks.append(_dot(oh_b.astype(BF16), earlier))
        most = jnp.maximum(most, jnp.sum(oh_b, axis=1, keepdims=True))
    rank = jnp.sum(onehot * jnp.concatenate(ranks, axis=1), axis=0, keepdims=True)

    recT = jnp.concatenate([*_split3(gate4), gate4, g_idx, rank,
                            jnp.zeros((LANES - REC_RANK - 1, tm), F32)], axis=0)
    rec = recT.T
    h_scr[:, 0:D_MODEL] = h_hi
    h_scr[:, D_MODEL:] = rec.astype(BF16)
    rec_scr[...] = rec
    recT_scr[...] = recT[REC_GROUP:REC_GROUP + SUBLANES, :]
    grp_row = lax.broadcasted_iota(jnp.int32, (SUBLANES, 1), 0)
    for gp in range(N_GROUPS):
        fullest = jnp.max(jnp.where(grp_row == gp, most, 0.0))
        tier_smem[gp] = sum((fullest > slot).astype(jnp.int32) for slot in MOE_SLOTS)
    o_ref[...] = jnp.zeros_like(o_ref)


def _moe_kernel(layer, x_ref, mod_ref, wrT_ref, brT_ref, win_hbm, wout_hbm, g_ref, b_ref, o_ref,
                h_scr, rec_scr, recT_scr, act_scr, win_buf, wout_buf, w_sem, tier_smem):
    g = pl.program_id(1)
    g_f = g.astype(F32)
    tm = x_ref.shape[0]
    n_blocks = tm // MOE_BLOCK
    step = pl.program_id(0) * N_GROUPS + g
    buf = step % 2

    def weight_copies(group, buf):
        experts_of_group = pl.ds(group * EXPERTS_PER_GROUP, EXPERTS_PER_GROUP)
        return (pltpu.make_async_copy(win_hbm.at[layer, experts_of_group], win_buf.at[buf], w_sem.at[0, buf]),
                pltpu.make_async_copy(wout_hbm.at[layer, group], wout_buf.at[buf], w_sem.at[1, buf]))

    @pl.when(step == 0)
    def _():
        for copy in weight_copies(0, 0):
            copy.start()

    @pl.when(step + 1 < pl.num_programs(0) * N_GROUPS)
    def _():
        for copy in weight_copies((g + 1) % N_GROUPS, 1 - buf):
            copy.start()

    @pl.when(g == 0)
    def _():
        _moe_route_tile(x_ref, mod_ref, wrT_ref, brT_ref, h_scr, rec_scr, recT_scr, o_ref, tier_smem)

    for copy in weight_copies(g, buf):
        copy.wait()
    win_ref = win_buf.at[buf]
    wout_ref = wout_buf.at[buf]

    def experts(xb, gates, gate_lane, rows):
        for e in range(EXPERTS_PER_GROUP):
            hid = _dot(xb, win_ref[e].astype(BF16))
            act = _silu(hid[:, :D_EXPERT]) * hid[:, D_EXPERT:] * gates[:, gate_lane + e:gate_lane + e + 1]
            act_scr[rows, e * D_EXPERT:(e + 1) * D_EXPERT] = act.astype(BF16)
        return _dot(act_scr[rows, :], wout_ref[...].astype(BF16))

    def compact(slot):
        n_rows = n_blocks * slot
        slot_row = lax.broadcasted_iota(jnp.int32, (slot, MOE_BLOCK), 0).astype(F32)
        gathered = []
        for b in range(n_blocks):
            tok = slice(b * MOE_BLOCK, (b + 1) * MOE_BLOCK)
            pick = (recT_scr[0:1, tok] == g_f) & (recT_scr[1:2, tok] == slot_row)
            gathered.append(_dot(jnp.where(pick, 1.0, 0.0).astype(BF16), h_scr[tok, :]))
        xg = jnp.concatenate(gathered, axis=0)
        ge = xg[:, D_MODEL:]
        gates = ((ge + pltpu.roll(ge, LANES - EXPERTS_PER_GROUP, 1))
                 + pltpu.roll(ge, LANES - 2 * EXPERTS_PER_GROUP, 1))
        y = experts(xg[:, :D_MODEL].astype(BF16), gates, REC_SPLIT, slice(0, n_rows)).astype(BF16)
        slot_col = lax.broadcasted_iota(jnp.int32, (MOE_BLOCK, SCATTER_K), 1).astype(F32)
        for b in range(n_blocks):
            tok = slice(b * MOE_BLOCK, (b + 1) * MOE_BLOCK)
            first = min(b * slot, n_rows - SCATTER_K)
            place = ((rec_scr[tok, REC_GROUP:REC_GROUP + 1] == g_f)
                     & (rec_scr[tok, REC_RANK:REC_RANK + 1] + (b * slot - first) == slot_col))
            o_ref[tok, :] += _dot(jnp.where(place, 1.0, 0.0).astype(BF16), y[first:first + SCATTER_K, :])

    for tier, slot in enumerate(MOE_SLOTS):
        pl.when(tier_smem[g] == tier)(functools.partial(compact, slot))

    @pl.when(tier_smem[g] == len(MOE_SLOTS))
    def _():
        for b in range(n_blocks):
            tok = slice(b * MOE_BLOCK, (b + 1) * MOE_BLOCK)
            gates = jnp.where(rec_scr[tok, REC_GROUP:REC_GROUP + 1] == g_f, rec_scr[tok, :], 0.0)
            o_ref[tok, :] += experts(h_scr[tok, 0:D_MODEL], gates, REC_GATE, slice(0, MOE_BLOCK))

    @pl.when(g == N_GROUPS - 1)
    def _():
        seg = tm // mod_ref.shape[0]
        for s in range(tm // seg):
            tok = slice(s * seg, (s + 1) * seg)
            gf = mod_ref[s, :, 5 * D_MODEL:6 * D_MODEL]
            o_ref[tok, :] = _layer_norm(ALPHA * x_ref[tok, :] + gf * o_ref[tok, :], g_ref[...], b_ref[...])


def _moe(l, x, mod_rows, w_routerT, b_routerT, w_exp_in, w_exp_out, ln_g, ln_b):
    n = x.shape[0]
    tm = MOE_TOKENS
    n_seg = mod_rows.shape[2]
    n_blocks = tm // MOE_BLOCK
    assert tm % MOE_BLOCK == 0
    for slot in MOE_SLOTS:
        assert slot % BF16_ROWS == 0 and slot <= SCATTER_K <= n_blocks * slot
    n_rows = n_blocks * max(MOE_SLOTS)
    d_act = EXPERTS_PER_GROUP * D_EXPERT
    return pl.pallas_call(
        functools.partial(_moe_kernel, l),
        grid=(n // tm, N_GROUPS),
        in_specs=[
            pl.BlockSpec((tm, D_MODEL), lambda i, g: (i, 0)),
            pl.BlockSpec((None, None, n_seg, 1, 6 * D_MODEL), lambda i, g: (l, i, 0, 0, 0)),
            pl.BlockSpec((None, LANES, D_MODEL), lambda i, g: (l, 0, 0)),
            pl.BlockSpec((None, LANES, 1), lambda i, g: (l, 0, 0)),
            pl.BlockSpec(memory_space=pl.ANY),
            pl.BlockSpec(memory_space=pl.ANY),
            pl.BlockSpec((None, 1, D_MODEL), lambda i, g: (l, 0, 0)),
            pl.BlockSpec((None, 1, D_MODEL), lambda i, g: (l, 0, 0)),
        ],
        out_specs=pl.BlockSpec((tm, D_MODEL), lambda i, g: (i, 0)),
        out_shape=jax.ShapeDtypeStruct(x.shape, F32),
        scratch_shapes=[
            pltpu.VMEM((tm, D_MODEL + GATE_COLS), BF16),
            pltpu.VMEM((tm, LANES), F32),
            pltpu.VMEM((SUBLANES, tm), F32),
            pltpu.VMEM((max(n_rows, MOE_BLOCK), d_act), BF16),
            pltpu.VMEM((2, EXPERTS_PER_GROUP, D_MODEL, 2 * D_EXPERT), F32),
            pltpu.VMEM((2, d_act, D_MODEL), F32),
            pltpu.SemaphoreType.DMA((2, 2)),
            pltpu.SMEM((N_GROUPS,), jnp.int32),
        ],
        compiler_params=pltpu.CompilerParams(
            dimension_semantics=("arbitrary", "arbitrary"), vmem_limit_bytes=VMEM_LIMIT_BYTES),
        name="hier_moe",
    )(x, mod_rows, w_routerT, b_routerT, w_exp_in, w_exp_out, ln_g, ln_b)


def kernel(x_prompt, x_sample, cache_k, cache_v, c, c_ctx, w_ada, b_ada, w_in, conv_w, rpb, w_out,
           ln1_g, ln1_b, w_router_group, b_router_group, w_router_expert, b_router_expert,
           w_expert_in, w_expert_out, ln2_g, ln2_b):
    batch, seq_len, _ = x_prompt.shape
    dec_batch, dec_seq, _ = x_sample.shape
    assert dec_batch + 1 <= SUBLANES and MOE_TOKENS % dec_seq == 0
    assert (dec_batch * dec_seq) % MOE_TOKENS == 0 and (batch * seq_len) % MOE_TOKENS == 0

    cond = jnp.concatenate([c_ctx[None, :], c, jnp.zeros((SUBLANES - 1 - dec_batch, D_MODEL), F32)], axis=0)
    mod = _modulation(cond, w_ada, b_ada)
    n_ctx_tiles = batch * seq_len // MOE_TOKENS
    mod_rows_ctx = jnp.broadcast_to(mod[:, 0:1, None, None, :], (DEPTH, n_ctx_tiles, 1, 1, 6 * D_MODEL))
    mod_rows_lat = mod[:, 1:1 + dec_batch].reshape(DEPTH, -1, MOE_TOKENS // dec_seq, 1, 6 * D_MODEL)

    w_in_bf = w_in.astype(BF16)
    w_out_bf = w_out.astype(BF16)
    pad = jnp.zeros((DEPTH, LANES - N_EXPERTS - N_GROUPS, D_MODEL), F32)
    w_routerT = jnp.concatenate(
        [jnp.swapaxes(w_router_expert, 1, 2), jnp.swapaxes(w_router_group, 1, 2), pad], axis=1)
    b_routerT = jnp.concatenate([b_router_expert, b_router_group, pad[:, :, 0]], axis=-1)[:, :, None]
    tbl = _bias_table(rpb)
    cache_kT = jnp.swapaxes(cache_k, -1, -2)
    cache_vT = jnp.swapaxes(cache_v, -1, -2)
    ln1_g3, ln1_b3 = ln1_g[:, None, :], ln1_b[:, None, :]
    ln2_g3, ln2_b3 = ln2_g[:, None, :], ln2_b[:, None, :]

    w_exp_out = w_expert_out.reshape(DEPTH, N_GROUPS, EXPERTS_PER_GROUP * D_EXPERT, D_MODEL)

    xp, xs = x_prompt, x_sample
    kv_bufs = [jnp.zeros((batch, DEPTH, N_HEADS, HEAD_DIM, seq_len), F32)] * 2
    for l in range(DEPTH):
        moe = functools.partial(_moe, l, w_routerT=w_routerT, b_routerT=b_routerT, w_exp_in=w_expert_in,
                                w_exp_out=w_exp_out, ln_g=ln2_g3, ln_b=ln2_b3)
        xp, *kv_bufs = _ctx_mixer(l, xp, mod, w_in_bf, w_out_bf, conv_w, ln1_g3, ln1_b3, kv_bufs)
        xp = moe(xp.reshape(-1, D_MODEL), mod_rows_ctx).reshape(xp.shape)
        xs = _lat_mixer(l, xs, mod, w_in_bf, w_out_bf, conv_w, ln1_g3, ln1_b3, cache_kT, cache_vT, tbl)
        xs = moe(xs.reshape(-1, D_MODEL), mod_rows_lat).reshape(xs.shape)
    new_kT, new_vT = kv_bufs
    return (xp, xs, jnp.swapaxes(new_kT, -1, -2), jnp.swapaxes(new_vT, -1, -2))
```

```python
import functools

import jax
import jax.numpy as jnp
from jax import lax
from jax.experimental import pallas as pl
from jax.experimental.pallas import tpu as pltpu

D_MODEL = 1024
DEPTH = 4
GRID_W = 64
ATT_WIDTH = D_MODEL // 2
CONV_WIDTH = D_MODEL - ATT_WIDTH
HEAD_DIM = 64
N_HEADS = ATT_WIDTH // HEAD_DIM
WIN_ROWS = 8
WIN_COLS = 16
N_GROUPS = 4
EXPERTS_PER_GROUP = 4
N_EXPERTS = N_GROUPS * EXPERTS_PER_GROUP
D_EXPERT = D_MODEL // 4
ALPHA = (2 * DEPTH) ** 0.25
LN_EPS = 1e-5
NEG_INF = -1e30
QK_SCALE = HEAD_DIM ** -0.5

F32 = jnp.float32
BF16 = jnp.bfloat16

LANES = 128
SUBLANES = 8
BF16_ROWS = 16
VMEM_LIMIT_BYTES = 56 * 1024 * 1024

CTX_SEQ_PER_STEP = 2
LAT_Q_CHUNK = 128
LAT_HEADS_PER_STEP = 4
MERGE_ROWS = 256
MOE_TOKENS = 1024
MOE_BLOCK = 512
MOE_SLOTS = (144, 192, 256)
SCATTER_K = 256
GATE_COLS = 128
ROUTE_ROWS = 24
REC_SPLIT, REC_GATE, REC_GROUP, REC_RANK = 0, 12, 16, 17
MOD_COLS = 2048


def _dot(a, b):
    return jnp.dot(a, b, preferred_element_type=F32)


def _dot_nt(a, b):
    return lax.dot_general(a, b, (((1,), (1,)), ((), ())), preferred_element_type=F32)


def _silu(x):
    return x * (1.0 / (1.0 + jnp.exp(-x)))


def _layer_norm(r, g, b):
    mu = jnp.mean(r, axis=-1, keepdims=True)
    d = r - mu
    var = jnp.mean(d * d, axis=-1, keepdims=True)
    return d * lax.rsqrt(var + LN_EPS) * g + b


def _mod_kernel(cond_ref, w_ref, b_ref, o_ref):
    s = _silu(cond_ref[...]).astype(BF16)
    o_ref[...] = _dot(s, w_ref[...].astype(BF16)) + b_ref[...]


def _modulation(cond, w_ada, b_ada):
    n_out = w_ada.shape[-1]
    return pl.pallas_call(
        _mod_kernel,
        grid=(DEPTH, n_out // MOD_COLS),
        in_specs=[
            pl.BlockSpec((SUBLANES, D_MODEL), lambda l, j: (0, 0)),
            pl.BlockSpec((None, D_MODEL, MOD_COLS), lambda l, j: (l, 0, j)),
            pl.BlockSpec((None, 1, MOD_COLS), lambda l, j: (l, 0, j)),
        ],
        out_specs=pl.BlockSpec((None, SUBLANES, MOD_COLS), lambda l, j: (l, 0, j)),
        out_shape=jax.ShapeDtypeStruct((DEPTH, SUBLANES, n_out), F32),
        compiler_params=pltpu.CompilerParams(
            dimension_semantics=("arbitrary", "arbitrary"),
            vmem_limit_bytes=VMEM_LIMIT_BYTES),
        name="adaln_modulation",
    )(cond, w_ada, b_ada.reshape(DEPTH, 1, n_out))


def _project(x, mod, win_ref, convw_ref, qT_scr, k_scr, vT_scr, conv_scr, seq_len, kT_out=None, vT_out=None):
    m = x.shape[0]
    sa = mod[:, 0:D_MODEL]
    ca = mod[:, D_MODEL:2 * D_MODEL]
    h = (x * (1.0 + ca) + sa).astype(BF16)

    qT_scr[...] = (_dot(h, win_ref[:, 0:ATT_WIDTH]) * QK_SCALE).T.astype(BF16)
    zk = _dot(h, win_ref[:, ATT_WIDTH:2 * ATT_WIDTH])
    for hd in range(N_HEADS):
        k_scr[hd] = zk[:, hd * HEAD_DIM:(hd + 1) * HEAD_DIM].astype(BF16)
    zvT = _dot(h, win_ref[:, 2 * ATT_WIDTH:3 * ATT_WIDTH]).T
    vT_scr[...] = zvT.astype(BF16)
    if kT_out is not None:
        zkT = zk.T
        for out, zT in ((kT_out, zkT), (vT_out, zvT)):
            for s in range(m // seq_len):
                for hd in range(N_HEADS):
                    out[s, hd] = zT[hd * HEAD_DIM:(hd + 1) * HEAD_DIM, s * seq_len:(s + 1) * seq_len]

    c0 = 3 * ATT_WIDTH
    bg = _dot(h, win_ref[:, c0:c0 + CONV_WIDTH])
    cg = _dot(h, win_ref[:, c0 + CONV_WIDTH:c0 + 2 * CONV_WIDTH])
    u = _dot(h, win_ref[:, c0 + 2 * CONV_WIDTH:c0 + 3 * CONV_WIDTH])
    y = cg * u
    t = lax.broadcasted_iota(jnp.int32, (m, 1), 0) % seq_len
    y_prev = jnp.where(t == 0, 0.0, pltpu.roll(y, 1, 0))
    y_next = jnp.where(t == seq_len - 1, 0.0, pltpu.roll(y, m - 1, 0))
    cw = convw_ref[...]
    conv = cw[0:1, :] * y_prev + cw[1:2, :] * y + cw[2:3, :] * y_next
    conv_scr[...] = (bg * conv).astype(BF16)


def _merge_and_norm(x, mod, attT_scr, conv_scr, wout_ref, g_ref, b_ref, o_ref):
    ga = mod[:, 2 * D_MODEL:3 * D_MODEL]
    out = []
    for r in range(0, x.shape[0], MERGE_ROWS):
        rows = slice(r, r + MERGE_ROWS)
        att = attT_scr[:, rows].T.astype(BF16)
        mix = _dot(att, wout_ref[0:ATT_WIDTH, :]) + _dot(conv_scr[rows, :], wout_ref[ATT_WIDTH:, :])
        out.append(_layer_norm(ALPHA * x[rows, :] + ga * mix, g_ref[...], b_ref[...]))
    o_ref[...] = jnp.concatenate(out, axis=0).reshape(o_ref.shape)


def _softmax_keys(parts):
    mx = functools.reduce(jnp.maximum, [jnp.max(p, axis=0, keepdims=True) for p in parts])
    es = [jnp.exp(p - mx) for p in parts]
    inv = 1.0 / functools.reduce(jnp.add, [jnp.sum(e, axis=0, keepdims=True) for e in es])
    return [e.astype(BF16) for e in es], inv


def _head_rows(hd):
    return pl.ds(pl.multiple_of(hd * HEAD_DIM, HEAD_DIM), HEAD_DIM)


def _weight_spec(shape, index_map):
    return pl.BlockSpec(shape, index_map, pipeline_mode=pl.Buffered(1))


def _ctx_mixer_kernel(x_ref, mod_ref, win_ref, wout_ref, convw_ref, g_ref, b_ref, kT_in, vT_in,
                      o_ref, kT_ref, vT_ref, qT_scr, k_scr, vT_scr, attT_scr, conv_scr, sT_scr, pT_scr, inv_scr):
    del kT_in, vT_in
    sb, seq_len, _ = x_ref.shape
    x = x_ref[...].reshape(sb * seq_len, D_MODEL)
    mod = mod_ref[0:1, :]
    _project(x, mod, win_ref, convw_ref, qT_scr, k_scr, vT_scr, conv_scr, seq_len, kT_ref, vT_ref)

    pairs = [(hd, s) for hd in range(N_HEADS) for s in range(sb)]
    chan = lambda hd: slice(hd * HEAD_DIM, (hd + 1) * HEAD_DIM)
    tok = lambda s: slice(s * seq_len, (s + 1) * seq_len)
    for i, (hd, s) in enumerate(pairs):
        sT_scr[i] = _dot(k_scr[hd, tok(s), :], qT_scr[chan(hd), tok(s)])
    for i in range(len(pairs)):
        (pT_scr[i],), inv_scr[i] = _softmax_keys([sT_scr[i]])
    for i, (hd, s) in enumerate(pairs):
        attT_scr[chan(hd), tok(s)] = _dot(vT_scr[chan(hd), tok(s)], pT_scr[i]) * inv_scr[i]

    _merge_and_norm(x, mod, attT_scr, conv_scr, wout_ref, g_ref, b_ref, o_ref)


def _ctx_mixer(l, xp, mod, w_in, w_out, conv_w, ln_g, ln_b, kv_bufs):
    batch, seq_len, _ = xp.shape
    sb = CTX_SEQ_PER_STEP
    m = sb * seq_len
    kv_shape = jax.ShapeDtypeStruct((batch, DEPTH, N_HEADS, HEAD_DIM, seq_len), F32)
    kv_spec = pl.BlockSpec((sb, None, N_HEADS, HEAD_DIM, seq_len), lambda i: (i, l, 0, 0, 0))
    in_specs = [
        pl.BlockSpec((sb, seq_len, D_MODEL), lambda i: (i, 0, 0)),
        pl.BlockSpec((None, SUBLANES, 6 * D_MODEL), lambda i: (l, 0, 0)),
        _weight_spec((None, D_MODEL, 3 * ATT_WIDTH + 3 * CONV_WIDTH), lambda i: (l, 0, 0)),
        _weight_spec((None, D_MODEL, D_MODEL), lambda i: (l, 0, 0)),
        pl.BlockSpec((None, 3, CONV_WIDTH), lambda i: (l, 0, 0)),
        pl.BlockSpec((None, 1, D_MODEL), lambda i: (l, 0, 0)),
        pl.BlockSpec((None, 1, D_MODEL), lambda i: (l, 0, 0)),
        pl.BlockSpec(memory_space=pl.ANY),
        pl.BlockSpec(memory_space=pl.ANY),
    ]
    args = [xp, mod, w_in, w_out, conv_w, ln_g, ln_b, *kv_bufs]
    return pl.pallas_call(
        _ctx_mixer_kernel,
        grid=(batch // sb,),
        in_specs=in_specs,
        out_specs=[pl.BlockSpec((sb, seq_len, D_MODEL), lambda i: (i, 0, 0)), kv_spec, kv_spec],
        out_shape=[jax.ShapeDtypeStruct(xp.shape, F32), kv_shape, kv_shape],
        input_output_aliases={len(args) - 2: 1, len(args) - 1: 2},
        scratch_shapes=[
            pltpu.VMEM((ATT_WIDTH, m), BF16),
            pltpu.VMEM((N_HEADS, m, HEAD_DIM), BF16),
            pltpu.VMEM((ATT_WIDTH, m), BF16),
            pltpu.VMEM((ATT_WIDTH, m), F32),
            pltpu.VMEM((m, CONV_WIDTH), BF16),
            pltpu.VMEM((N_HEADS * sb, seq_len, seq_len), F32),
            pltpu.VMEM((N_HEADS * sb, seq_len, seq_len), BF16),
            pltpu.VMEM((N_HEADS * sb, 1, seq_len), F32),
        ],
        compiler_params=pltpu.CompilerParams(
            dimension_semantics=("arbitrary",), vmem_limit_bytes=VMEM_LIMIT_BYTES),
        name="ctx_mixer",
    )(*args)


def _window_start(r, rows):
    return min(max(r - WIN_ROWS // 2, 0), rows - WIN_ROWS)


def _chunk_key_rows(r0, chunk_rows, rows):
    lo = _window_start(r0, rows) // 2 * 2
    hi = -(-(_window_start(r0 + chunk_rows - 1, rows) + WIN_ROWS) // 2) * 2
    return lo, hi


def _lat_mixer_kernel(x_ref, mod_ref, win_ref, wout_ref, convw_ref, g_ref, b_ref,
                      ckT_ref, cvT_ref, tbl_ref, o_ref,
                      qT_scr, k_scr, vT_scr, attT_scr, conv_scr, sT_scr, pT_scr, inv_scr):
    b = pl.program_id(0)
    step = pl.program_id(1)
    seq_len = x_ref.shape[0]
    rows = seq_len // GRID_W
    mod = mod_ref[pl.ds(1 + b, 1), :]

    @pl.when(step == 0)
    def _():
        _project(x_ref[...], mod, win_ref, convw_ref, qT_scr, k_scr, vT_scr, conv_scr, seq_len)

    low_half = lax.broadcasted_iota(jnp.int32, (GRID_W, 2 * GRID_W), 1) < GRID_W

    def bias_block(j, r_pair, rk):
        inside = [_window_start(r, rows) <= rk < _window_start(r, rows) + WIN_ROWS for r in (r_pair, r_pair + 1)]
        d = rk - r_pair + WIN_ROWS - 1
        neg = jnp.full((GRID_W, 2 * GRID_W), NEG_INF, F32)
        if not any(inside):
            return neg
        blk = tbl_ref[j, d * GRID_W:(d + 1) * GRID_W, :]
        if all(inside):
            return blk
        return jnp.where(low_half, blk, neg) if inside[0] else jnp.where(low_half, neg, blk)

    chunk_rows = LAT_Q_CHUNK // GRID_W
    n_chunks = seq_len // LAT_Q_CHUNK
    past = ckT_ref.shape[-1]
    max_loc = sT_scr.shape[1] - past
    chunks = []
    for qc in range(n_chunks):
        r0 = qc * chunk_rows
        k_lo, k_hi = _chunk_key_rows(r0, chunk_rows, rows)
        chunks.append((qc, r0, k_lo, k_hi, slice(qc * LAT_Q_CHUNK, (qc + 1) * LAT_Q_CHUNK)))
    for j in range(LAT_HEADS_PER_STEP):
        hd = step * LAT_HEADS_PER_STEP + j
        chan = _head_rows(hd)
        ckT = ckT_ref[j]
        ck = jnp.concatenate([ckT, jnp.zeros_like(ckT)], axis=0).T.astype(BF16)
        cvT = cvT_ref[j].astype(BF16)
        for qc, r0, k_lo, k_hi, qcols in chunks:
            i = j * n_chunks + qc
            qT = qT_scr[chan, qcols]
            bias = jnp.concatenate(
                [jnp.concatenate([bias_block(j, r0 + jj, rk) for jj in range(0, chunk_rows, 2)], axis=1)
                 for rk in range(k_lo, k_hi)], axis=0)
            sT_scr[i, 0:(k_hi - k_lo) * GRID_W] = _dot(k_scr[hd, k_lo * GRID_W:k_hi * GRID_W, :], qT) + bias
            sT_scr[i, max_loc:] = _dot(ck, jnp.concatenate([qT, jnp.zeros_like(qT)], axis=0))
        for qc, r0, k_lo, k_hi, qcols in chunks:
            i = j * n_chunks + qc
            n_loc = (k_hi - k_lo) * GRID_W
            (pT_scr[i, 0:n_loc], pT_scr[i, max_loc:]), inv_scr[i] = _softmax_keys(
                [sT_scr[i, 0:n_loc], sT_scr[i, max_loc:]])
        for qc, r0, k_lo, k_hi, qcols in chunks:
            i = j * n_chunks + qc
            n_loc = (k_hi - k_lo) * GRID_W
            attT_scr[chan, qcols] = (_dot(vT_scr[chan, k_lo * GRID_W:k_hi * GRID_W], pT_scr[i, 0:n_loc])
                                     + _dot(cvT, pT_scr[i, max_loc:])) * inv_scr[i]

    @pl.when(step == pl.num_programs(1) - 1)
    def _():
        _merge_and_norm(x_ref[...], mod, attT_scr, conv_scr, wout_ref, g_ref, b_ref, o_ref)


def _lat_mixer(l, xs, mod, w_in, w_out, conv_w, ln_g, ln_b, cache_kT, cache_vT, tbl):
    batch, seq_len, _ = xs.shape
    past = cache_kT.shape[-1]
    rows, chunk_rows, n_chunks = seq_len // GRID_W, LAT_Q_CHUNK // GRID_W, seq_len // LAT_Q_CHUNK
    max_loc = GRID_W * max(hi - lo for lo, hi in
                           (_chunk_key_rows(qc * chunk_rows, chunk_rows, rows) for qc in range(n_chunks)))
    hps = LAT_HEADS_PER_STEP
    cache_spec = pl.BlockSpec((None, None, hps, HEAD_DIM, past), lambda b, h: (b, l, h, 0, 0))
    return pl.pallas_call(
        _lat_mixer_kernel,
        grid=(batch, N_HEADS // hps),
        in_specs=[
            pl.BlockSpec((None, seq_len, D_MODEL), lambda b, h: (b, 0, 0)),
            pl.BlockSpec((None, SUBLANES, 6 * D_MODEL), lambda b, h: (l, 0, 0)),
            _weight_spec((None, D_MODEL, 3 * ATT_WIDTH + 3 * CONV_WIDTH), lambda b, h: (l, 0, 0)),
            _weight_spec((None, D_MODEL, D_MODEL), lambda b, h: (l, 0, 0)),
            pl.BlockSpec((None, 3, CONV_WIDTH), lambda b, h: (l, 0, 0)),
            pl.BlockSpec((None, 1, D_MODEL), lambda b, h: (l, 0, 0)),
            pl.BlockSpec((None, 1, D_MODEL), lambda b, h: (l, 0, 0)),
            cache_spec, cache_spec,
            pl.BlockSpec((None, hps) + tbl.shape[2:], lambda b, h: (l, h, 0, 0)),
        ],
        out_specs=pl.BlockSpec((None, seq_len, D_MODEL), lambda b, h: (b, 0, 0)),
        out_shape=jax.ShapeDtypeStruct(xs.shape, F32),
        scratch_shapes=[
            pltpu.VMEM((ATT_WIDTH, seq_len), BF16),
            pltpu.VMEM((N_HEADS, seq_len, HEAD_DIM), BF16),
            pltpu.VMEM((ATT_WIDTH, seq_len), BF16),
            pltpu.VMEM((ATT_WIDTH, seq_len), F32),
            pltpu.VMEM((seq_len, CONV_WIDTH), BF16),
            pltpu.VMEM((hps * n_chunks, max_loc + past, LAT_Q_CHUNK), F32),
            pltpu.VMEM((hps * n_chunks, max_loc + past, LAT_Q_CHUNK), BF16),
            pltpu.VMEM((hps * n_chunks, 1, LAT_Q_CHUNK), F32),
        ],
        compiler_params=pltpu.CompilerParams(
            dimension_semantics=("arbitrary", "arbitrary"), vmem_limit_bytes=VMEM_LIMIT_BYTES),
        name="lat_mixer",
    )(xs, mod, w_in, w_out, conv_w, ln_g, ln_b, cache_kT, cache_vT, tbl)


def _bias_table(rpb):
    depth, heads, n_dr, n_dc = rpb.shape
    n_blk = n_dr + 1
    lane0 = GRID_W - WIN_COLS
    n_pad = -(-n_dr // SUBLANES) * SUBLANES
    rpb_pad = jnp.pad(rpb[..., ::-1], ((0, 0), (0, 0), (0, n_pad - n_dr), (lane0, LANES - lane0 - n_dc)))
    rpb_pad = rpb_pad.reshape(depth * heads, n_pad, LANES)

    def body(r_ref, o_ref):
        cp = lax.broadcasted_iota(jnp.int32, (GRID_W, LANES), 0)
        lane = lax.broadcasted_iota(jnp.int32, (GRID_W, LANES), 1)
        low_half = lane < GRID_W
        c = jnp.where(low_half, lane, lane - GRID_W)
        col_start = jnp.clip(c - WIN_COLS // 2, 0, GRID_W - WIN_COLS)
        valid = (cp >= col_start) & (cp < col_start + WIN_COLS)
        def skewed(hd, dr, shift):
            row = jnp.broadcast_to(r_ref[hd, dr:dr + 1, :], (GRID_W, LANES))
            return pltpu.roll(row, shift, 1, stride=1, stride_axis=0)

        for hd in range(heads):
            for d in range(n_blk):
                lo = skewed(hd, d, LANES - GRID_W + 1) if d < n_dr else None
                hi = skewed(hd, d - 1, 1) if d >= 1 else None
                if lo is None:
                    blk = jnp.where(valid & ~low_half, hi, NEG_INF)
                elif hi is None:
                    blk = jnp.where(valid & low_half, lo, NEG_INF)
                else:
                    blk = jnp.where(valid, jnp.where(low_half, lo, hi), NEG_INF)
                o_ref[hd, d * GRID_W:(d + 1) * GRID_W, :] = blk

    tbl = pl.pallas_call(
        body,
        grid=(depth,),
        in_specs=[pl.BlockSpec((heads, n_pad, LANES), lambda i: (i, 0, 0))],
        out_specs=pl.BlockSpec((heads, n_blk * GRID_W, LANES), lambda i: (i, 0, 0)),
        out_shape=jax.ShapeDtypeStruct((depth * heads, n_blk * GRID_W, LANES), F32),
        compiler_params=pltpu.CompilerParams(dimension_semantics=("arbitrary",)),
        name="bias_table",
    )(rpb_pad)
    return tbl.reshape(depth, heads, n_blk * GRID_W, LANES)


def _route_t(logits):
    row = lax.broadcasted_iota(jnp.int32, logits.shape, 0)
    row_f = row.astype(F32)
    big = jnp.float32(LANES)

    def first_row(cond):
        return jnp.min(jnp.where(cond, row_f, big), axis=0, keepdims=True)

    gmask = (row >= N_EXPERTS) & (row < N_EXPERTS + N_GROUPS)
    gl = jnp.where(gmask, logits, NEG_INF)
    gexp = jnp.exp(gl - jnp.max(gl, axis=0, keepdims=True))
    gprob = gexp / jnp.sum(gexp, axis=0, keepdims=True)
    g_p = jnp.max(gprob, axis=0, keepdims=True)
    g_idx = first_row(gmask & (gprob == g_p)) - N_EXPERTS

    row_group = jnp.floor(row_f * (1.0 / EXPERTS_PER_GROUP))
    emask = (row < N_EXPERTS) & (row_group == g_idx)
    el = jnp.where(emask, logits, NEG_INF)
    eexp = jnp.exp(el - jnp.max(el, axis=0, keepdims=True))
    eprob = eexp / jnp.sum(eexp, axis=0, keepdims=True)
    p1 = jnp.max(eprob, axis=0, keepdims=True)
    i1 = first_row(emask & (eprob == p1))
    rest = emask & (row_f != i1)
    p2 = jnp.max(jnp.where(rest, eprob, -1.0), axis=0, keepdims=True)
    i2 = first_row(rest & (eprob == p2))
    denom = p1 + p2
    gate = (jnp.where(row_f == i1, g_p * p1 / denom, 0.0)
            + jnp.where(row_f == i2, g_p * p2 / denom, 0.0))
    return gate, g_idx


def _split3(v):
    hi = v.astype(BF16).astype(F32)
    mid = (v - hi).astype(BF16).astype(F32)
    return hi, mid, v - hi - mid


def _moe_route_tile(x_ref, mod_ref, wrT_ref, brT_ref, h_scr, rec_scr, recT_scr, o_ref, tier_smem):
    tm = x_ref.shape[0]
    seg = tm // mod_ref.shape[0]
    h = jnp.concatenate(
        [x_ref[s * seg:(s + 1) * seg, :] * (1.0 + mod_ref[s, :, 4 * D_MODEL:5 * D_MODEL])
         + mod_ref[s, :, 3 * D_MODEL:4 * D_MODEL] for s in range(tm // seg)], axis=0)
    h_hi = h.astype(BF16)
    h_lo = (h - h_hi.astype(F32)).astype(BF16)
    wr = wrT_ref[...]
    wr_hi = wr.astype(BF16)
    wr_lo = (wr - wr_hi.astype(F32)).astype(BF16)
    logits = (_dot_nt(wr_hi, h_hi) + _dot_nt(wr_hi, h_lo) + _dot_nt(wr_lo, h_hi))[0:ROUTE_ROWS, :]
    gate, g_idx = _route_t(logits + brT_ref[0:ROUTE_ROWS, :])

    row_f = lax.broadcasted_iota(jnp.int32, (ROUTE_ROWS, tm), 0).astype(F32)
    gate4 = jnp.concatenate(
        [jnp.sum(jnp.where(row_f == EXPERTS_PER_GROUP * g_idx + j, gate, 0.0), axis=0, keepdims=True)
         for j in range(EXPERTS_PER_GROUP)], axis=0)

    grp = lax.broadcasted_iota(jnp.int32, (SUBLANES, tm), 0).astype(F32)
    onehot = jnp.where(grp == g_idx, 1.0, 0.0)
    ri = lax.broadcasted_iota(jnp.int32, (MOE_BLOCK, MOE_BLOCK), 0)
    ci = lax.broadcasted_iota(jnp.int32, (MOE_BLOCK, MOE_BLOCK), 1)
    earlier = jnp.where(ri < ci, 1.0, 0.0).astype(BF16)
    ranks = []
    most = jnp.zeros((SUBLANES, 1), F32)
    for b in range(tm // MOE_BLOCK):
        oh_b = onehot[:, b * MOE_BLOCK:(b + 1) * MOE_BLOCK]
        ranks.append(_dot(oh_b.astype(BF16), earlier))
        most = jnp.maximum(most, jnp.sum(oh_b, axis=1, keepdims=True))
    rank = jnp.sum(onehot * jnp.concatenate(ranks, axis=1), axis=0, keepdims=True)

    recT = jnp.concatenate([*_split3(gate4), gate4, g_idx, rank,
                            jnp.zeros((LANES - REC_RANK - 1, tm), F32)], axis=0)
    rec = recT.T
    h_scr[:, 0:D_MODEL] = h_hi
    h_scr[:, D_MODEL:] = rec.astype(BF16)
    rec_scr[...] = rec
    recT_scr[...] = recT[REC_GROUP:REC_GROUP + SUBLANES, :]
    grp_row = lax.broadcasted_iota(jnp.int32, (SUBLANES, 1), 0)
    for gp in range(N_GROUPS):
        fullest = jnp.max(jnp.where(grp_row == gp, most, 0.0))
        tier_smem[gp] = sum((fullest > slot).astype(jnp.int32) for slot in MOE_SLOTS)
    o_ref[...] = jnp.zeros_like(o_ref)


def _moe_kernel(layer, x_ref, mod_ref, wrT_ref, brT_ref, win_hbm, wout_hbm, g_ref, b_ref, o_ref,
                h_scr, rec_scr, recT_scr, act_scr, win_buf, wout_buf, w_sem, tier_smem):
    g = pl.program_id(1)
    g_f = g.astype(F32)
    tm = x_ref.shape[0]
    n_blocks = tm // MOE_BLOCK
    step = pl.program_id(0) * N_GROUPS + g
    buf = step % 2

    def weight_copies(group, buf):
        experts_of_group = pl.ds(group * EXPERTS_PER_GROUP, EXPERTS_PER_GROUP)
        return (pltpu.make_async_copy(win_hbm.at[layer, experts_of_group], win_buf.at[buf], w_sem.at[0, buf]),
                pltpu.make_async_copy(wout_hbm.at[layer, group], wout_buf.at[buf], w_sem.at[1, buf]))

    @pl.when(step == 0)
    def _():
        for copy in weight_copies(0, 0):
            copy.start()

    @pl.when(step + 1 < pl.num_programs(0) * N_GROUPS)
    def _():
        for copy in weight_copies((g + 1) % N_GROUPS, 1 - buf):
            copy.start()

    @pl.when(g == 0)
    def _():
        _moe_route_tile(x_ref, mod_ref, wrT_ref, brT_ref, h_scr, rec_scr, recT_scr, o_ref, tier_smem)

    for copy in weight_copies(g, buf):
        copy.wait()
    win_ref = win_buf.at[buf]
    wout_ref = wout_buf.at[buf]

    def experts(xb, gates, gate_lane, rows):
        for e in range(EXPERTS_PER_GROUP):
            hid = _dot(xb, win_ref[e].astype(BF16))
            act = _silu(hid[:, :D_EXPERT]) * hid[:, D_EXPERT:] * gates[:, gate_lane + e:gate_lane + e + 1]
            act_scr[rows, e * D_EXPERT:(e + 1) * D_EXPERT] = act.astype(BF16)
        return _dot(act_scr[rows, :], wout_ref[...].astype(BF16))

    def compact(slot):
        n_rows = n_blocks * slot
        slot_row = lax.broadcasted_iota(jnp.int32, (slot, MOE_BLOCK), 0).astype(F32)
        gathered = []
        for b in range(n_blocks):
            tok = slice(b * MOE_BLOCK, (b + 1) * MOE_BLOCK)
            pick = (recT_scr[0:1, tok] == g_f) & (recT_scr[1:2, tok] == slot_row)
            gathered.append(_dot(jnp.where(pick, 1.0, 0.0).astype(BF16), h_scr[tok, :]))
        xg = jnp.concatenate(gathered, axis=0)
        ge = xg[:, D_MODEL:]
        gates = ((ge + pltpu.roll(ge, LANES - EXPERTS_PER_GROUP, 1))
                 + pltpu.roll(ge, LANES - 2 * EXPERTS_PER_GROUP, 1))
        y = experts(xg[:, :D_MODEL].astype(BF16), gates, REC_SPLIT, slice(0, n_rows)).astype(BF16)
        slot_col = lax.broadcasted_iota(jnp.int32, (MOE_BLOCK, SCATTER_K), 1).astype(F32)
        for b in range(n_blocks):
            tok = slice(b * MOE_BLOCK, (b + 1) * MOE_BLOCK)
            first = min(b * slot, n_rows - SCATTER_K)
            place = ((rec_scr[tok, REC_GROUP:REC_GROUP + 1] == g_f)
                     & (rec_scr[tok, REC_RANK:REC_RANK + 1] + (b * slot - first) == slot_col))
            o_ref[tok, :] += _dot(jnp.where(place, 1.0, 0.0).astype(BF16), y[first:first + SCATTER_K, :])

    for tier, slot in enumerate(MOE_SLOTS):
        pl.when(tier_smem[g] == tier)(functools.partial(compact, slot))

    @pl.when(tier_smem[g] == len(MOE_SLOTS))
    def _():
        for b in range(n_blocks):
            tok = slice(b * MOE_BLOCK, (b + 1) * MOE_BLOCK)
            gates = jnp.where(rec_scr[tok, REC_GROUP:REC_GROUP + 1] == g_f, rec_scr[tok, :], 0.0)
            o_ref[tok, :] += experts(h_scr[tok, 0:D_MODEL], gates, REC_GATE, slice(0, MOE_BLOCK))

    @pl.when(g == N_GROUPS - 1)
    def _():
        seg = tm // mod_ref.shape[0]
        for s in range(tm // seg):
            tok = slice(s * seg, (s + 1) * seg)
            gf = mod_ref[s, :, 5 * D_MODEL:6 * D_MODEL]
            o_ref[tok, :] = _layer_norm(ALPHA * x_ref[tok, :] + gf * o_ref[tok, :], g_ref[...], b_ref[...])


def _moe(l, x, mod_rows, w_routerT, b_routerT, w_exp_in, w_exp_out, ln_g, ln_b):
    n = x.shape[0]
    tm = MOE_TOKENS
    n_seg = mod_rows.shape[2]
    n_blocks = tm // MOE_BLOCK
    assert tm % MOE_BLOCK == 0
    for slot in MOE_SLOTS:
        assert slot % BF16_ROWS == 0 and slot <= SCATTER_K <= n_blocks * slot
    n_rows = n_blocks * max(MOE_SLOTS)
    d_act = EXPERTS_PER_GROUP * D_EXPERT
    return pl.pallas_call(
        functools.partial(_moe_kernel, l),
        grid=(n // tm, N_GROUPS),
        in_specs=[
            pl.BlockSpec((tm, D_MODEL), lambda i, g: (i, 0)),
            pl.BlockSpec((None, None, n_seg, 1, 6 * D_MODEL), lambda i, g: (l, i, 0, 0, 0)),
            pl.BlockSpec((None, LANES, D_MODEL), lambda i, g: (l, 0, 0)),
            pl.BlockSpec((None, LANES, 1), lambda i, g: (l, 0, 0)),
            pl.BlockSpec(memory_space=pl.ANY),
            pl.BlockSpec(memory_space=pl.ANY),
            pl.BlockSpec((None, 1, D_MODEL), lambda i, g: (l, 0, 0)),
            pl.BlockSpec((None, 1, D_MODEL), lambda i, g: (l, 0, 0)),
        ],
        out_specs=pl.BlockSpec((tm, D_MODEL), lambda i, g: (i, 0)),
        out_shape=jax.ShapeDtypeStruct(x.shape, F32),
        scratch_shapes=[
            pltpu.VMEM((tm, D_MODEL + GATE_COLS), BF16),
            pltpu.VMEM((tm, LANES), F32),
            pltpu.VMEM((SUBLANES, tm), F32),
            pltpu.VMEM((max(n_rows, MOE_BLOCK), d_act), BF16),
            pltpu.VMEM((2, EXPERTS_PER_GROUP, D_MODEL, 2 * D_EXPERT), F32),
            pltpu.VMEM((2, d_act, D_MODEL), F32),
            pltpu.SemaphoreType.DMA((2, 2)),
            pltpu.SMEM((N_GROUPS,), jnp.int32),
        ],
        compiler_params=pltpu.CompilerParams(
            dimension_semantics=("arbitrary", "arbitrary"), vmem_limit_bytes=VMEM_LIMIT_BYTES),
        name="hier_moe",
    )(x, mod_rows, w_routerT, b_routerT, w_exp_in, w_exp_out, ln_g, ln_b)


def kernel(x_prompt, x_sample, cache_k, cache_v, c, c_ctx, w_ada, b_ada, w_in, conv_w, rpb, w_out,
           ln1_g, ln1_b, w_router_group, b_router_group, w_router_expert, b_router_expert,
           w_expert_in, w_expert_out, ln2_g, ln2_b):
    batch, seq_len, _ = x_prompt.shape
    dec_batch, dec_seq, _ = x_sample.shape
    assert dec_batch + 1 <= SUBLANES and MOE_TOKENS % dec_seq == 0
    assert (dec_batch * dec_seq) % MOE_TOKENS == 0 and (batch * seq_len) % MOE_TOKENS == 0

    cond = jnp.concatenate([c_ctx[None, :], c, jnp.zeros((SUBLANES - 1 - dec_batch, D_MODEL), F32)], axis=0)
    mod = _modulation(cond, w_ada, b_ada)
    n_ctx_tiles = batch * seq_len // MOE_TOKENS
    mod_rows_ctx = jnp.broadcast_to(mod[:, 0:1, None, None, :], (DEPTH, n_ctx_tiles, 1, 1, 6 * D_MODEL))
    mod_rows_lat = mod[:, 1:1 + dec_batch].reshape(DEPTH, -1, MOE_TOKENS // dec_seq, 1, 6 * D_MODEL)

    w_in_bf = w_in.astype(BF16)
    w_out_bf = w_out.astype(BF16)
    pad = jnp.zeros((DEPTH, LANES - N_EXPERTS - N_GROUPS, D_MODEL), F32)
    w_routerT = jnp.concatenate(
        [jnp.swapaxes(w_router_expert, 1, 2), jnp.swapaxes(w_router_group, 1, 2), pad], axis=1)
    b_routerT = jnp.concatenate([b_router_expert, b_router_group, pad[:, :, 0]], axis=-1)[:, :, None]
    tbl = _bias_table(rpb)
    cache_kT = jnp.swapaxes(cache_k, -1, -2)
    cache_vT = jnp.swapaxes(cache_v, -1, -2)
    ln1_g3, ln1_b3 = ln1_g[:, None, :], ln1_b[:, None, :]
    ln2_g3, ln2_b3 = ln2_g[:, None, :], ln2_b[:, None, :]

    w_exp_out = w_expert_out.reshape(DEPTH, N_GROUPS, EXPERTS_PER_GROUP * D_EXPERT, D_MODEL)

    xp, xs = x_prompt, x_sample
    kv_bufs = [jnp.zeros((batch, DEPTH, N_HEADS, HEAD_DIM, seq_len), F32)] * 2
    for l in range(DEPTH):
        moe = functools.partial(_moe, l, w_routerT=w_routerT, b_routerT=b_routerT, w_exp_in=w_expert_in,
                                w_exp_out=w_exp_out, ln_g=ln2_g3, ln_b=ln2_b3)
        xp, *kv_bufs = _ctx_mixer(l, xp, mod, w_in_bf, w_out_bf, conv_w, ln1_g3, ln1_b3, kv_bufs)
        xp = moe(xp.reshape(-1, D_MODEL), mod_rows_ctx).reshape(xp.shape)
        xs = _lat_mixer(l, xs, mod, w_in_bf, w_out_bf, conv_w, ln1_g3, ln1_b3, cache_kT, cache_vT, tbl)
        xs = moe(xs.reshape(-1, D_MODEL), mod_rows_lat).reshape(xs.shape)
    new_kT, new_vT = kv_bufs
    return (xp, xs, jnp.swapaxes(new_kT, -1, -2), jnp.swapaxes(new_vT, -1, -2))
```

```python
import functools

import jax
import jax.numpy as jnp
from jax import lax
from jax.experimental import pallas as pl
from jax.experimental.pallas import tpu as pltpu

D_MODEL = 1024
DEPTH = 4
GRID_W = 64
ATT_WIDTH = D_MODEL // 2
CONV_WIDTH = D_MODEL - ATT_WIDTH
HEAD_DIM = 64
N_HEADS = ATT_WIDTH // HEAD_DIM
WIN_ROWS = 8
WIN_COLS = 16
N_GROUPS = 4
EXPERTS_PER_GROUP = 4
N_EXPERTS = N_GROUPS * EXPERTS_PER_GROUP
D_EXPERT = D_MODEL // 4
ALPHA = (2 * DEPTH) ** 0.25
LN_EPS = 1e-5
NEG_INF = -1e30
QK_SCALE = HEAD_DIM ** -0.5

F32 = jnp.float32
BF16 = jnp.bfloat16

LANES = 128
SUBLANES = 8
BF16_ROWS = 16
VMEM_LIMIT_BYTES = 56 * 1024 * 1024

CTX_SEQ_PER_STEP = 2
LAT_Q_CHUNK = 128
LAT_HEADS_PER_STEP = 2
MERGE_ROWS = 256
MOE_TOKENS = 1024
MOE_BLOCK = 512
MOE_SLOTS = (144, 192, 256)
SCATTER_K = 256
GATE_COLS = 128
ROUTE_ROWS = 24
REC_SPLIT, REC_GATE, REC_GROUP, REC_RANK = 0, 12, 16, 17
MOD_COLS = 2048


def _dot(a, b):
    return jnp.dot(a, b, preferred_element_type=F32)


def _dot_nt(a, b):
    return lax.dot_general(a, b, (((1,), (1,)), ((), ())), preferred_element_type=F32)


def _silu(x):
    return x * (1.0 / (1.0 + jnp.exp(-x)))


def _layer_norm(r, g, b):
    mu = jnp.mean(r, axis=-1, keepdims=True)
    d = r - mu
    var = jnp.mean(d * d, axis=-1, keepdims=True)
    return d * lax.rsqrt(var + LN_EPS) * g + b


def _mod_kernel(cond_ref, w_ref, b_ref, o_ref):
    s = _silu(cond_ref[...]).astype(BF16)
    o_ref[...] = _dot(s, w_ref[...].astype(BF16)) + b_ref[...]


def _modulation(cond, w_ada, b_ada):
    n_out = w_ada.shape[-1]
    return pl.pallas_call(
        _mod_kernel,
        grid=(DEPTH, n_out // MOD_COLS),
        in_specs=[
            pl.BlockSpec((SUBLANES, D_MODEL), lambda l, j: (0, 0)),
            pl.BlockSpec((None, D_MODEL, MOD_COLS), lambda l, j: (l, 0, j)),
            pl.BlockSpec((None, 1, MOD_COLS), lambda l, j: (l, 0, j)),
        ],
        out_specs=pl.BlockSpec((None, SUBLANES, MOD_COLS), lambda l, j: (l, 0, j)),
        out_shape=jax.ShapeDtypeStruct((DEPTH, SUBLANES, n_out), F32),
        compiler_params=pltpu.CompilerParams(
            dimension_semantics=("arbitrary", "arbitrary"),
            vmem_limit_bytes=VMEM_LIMIT_BYTES),
        name="adaln_modulation",
    )(cond, w_ada, b_ada.reshape(DEPTH, 1, n_out))


def _project(x, mod, win_ref, convw_ref, qT_scr, k_scr, vT_scr, conv_scr, seq_len, kT_out=None, vT_out=None):
    m = x.shape[0]
    sa = mod[:, 0:D_MODEL]
    ca = mod[:, D_MODEL:2 * D_MODEL]
    h = (x * (1.0 + ca) + sa).astype(BF16)

    qT_scr[...] = (_dot(h, win_ref[:, 0:ATT_WIDTH]) * QK_SCALE).T.astype(BF16)
    zk = _dot(h, win_ref[:, ATT_WIDTH:2 * ATT_WIDTH])
    for hd in range(N_HEADS):
        k_scr[hd] = zk[:, hd * HEAD_DIM:(hd + 1) * HEAD_DIM].astype(BF16)
    zvT = _dot(h, win_ref[:, 2 * ATT_WIDTH:3 * ATT_WIDTH]).T
    vT_scr[...] = zvT.astype(BF16)
    if kT_out is not None:
        zkT = zk.T
        for out, zT in ((kT_out, zkT), (vT_out, zvT)):
            for s in range(m // seq_len):
                for hd in range(N_HEADS):
                    out[s, hd] = zT[hd * HEAD_DIM:(hd + 1) * HEAD_DIM, s * seq_len:(s + 1) * seq_len]

    c0 = 3 * ATT_WIDTH
    bg = _dot(h, win_ref[:, c0:c0 + CONV_WIDTH])
    cg = _dot(h, win_ref[:, c0 + CONV_WIDTH:c0 + 2 * CONV_WIDTH])
    u = _dot(h, win_ref[:, c0 + 2 * CONV_WIDTH:c0 + 3 * CONV_WIDTH])
    y = cg * u
    t = lax.broadcasted_iota(jnp.int32, (m, 1), 0) % seq_len
    y_prev = jnp.where(t == 0, 0.0, pltpu.roll(y, 1, 0))
    y_next = jnp.where(t == seq_len - 1, 0.0, pltpu.roll(y, m - 1, 0))
    cw = convw_ref[...]
    conv = cw[0:1, :] * y_prev + cw[1:2, :] * y + cw[2:3, :] * y_next
    conv_scr[...] = (bg * conv).astype(BF16)


def _merge_and_norm(x, mod, attT_scr, conv_scr, wout_ref, g_ref, b_ref, o_ref):
    ga = mod[:, 2 * D_MODEL:3 * D_MODEL]
    out = []
    for r in range(0, x.shape[0], MERGE_ROWS):
        rows = slice(r, r + MERGE_ROWS)
        att = attT_scr[:, rows].T.astype(BF16)
        mix = _dot(att, wout_ref[0:ATT_WIDTH, :]) + _dot(conv_scr[rows, :], wout_ref[ATT_WIDTH:, :])
        out.append(_layer_norm(ALPHA * x[rows, :] + ga * mix, g_ref[...], b_ref[...]))
    o_ref[...] = jnp.concatenate(out, axis=0).reshape(o_ref.shape)


def _softmax_keys(parts):
    mx = functools.reduce(jnp.maximum, [jnp.max(p, axis=0, keepdims=True) for p in parts])
    es = [jnp.exp(p - mx) for p in parts]
    inv = 1.0 / functools.reduce(jnp.add, [jnp.sum(e, axis=0, keepdims=True) for e in es])
    return [e.astype(BF16) for e in es], inv


def _head_rows(hd):
    return pl.ds(pl.multiple_of(hd * HEAD_DIM, HEAD_DIM), HEAD_DIM)


def _weight_spec(shape, index_map):
    return pl.BlockSpec(shape, index_map, pipeline_mode=pl.Buffered(1))


def _ctx_mixer_kernel(l, creates_cache, x_ref, mod_ref, win_ref, wout_ref, convw_ref, g_ref, b_ref, *rest):
    if creates_cache:
        o_ref, kT_all, vT_all, qT_scr, k_scr, vT_scr, attT_scr, conv_scr, sT_scr, pT_scr, inv_scr = rest
        for ref in (kT_all, vT_all):
            for other in range(DEPTH):
                if other != l:
                    ref[:, other] = jnp.zeros(ref.shape[:1] + ref.shape[2:], F32)
        kT_ref, vT_ref = kT_all.at[:, l], vT_all.at[:, l]
    else:
        _, _, o_ref, kT_ref, vT_ref, qT_scr, k_scr, vT_scr, attT_scr, conv_scr, sT_scr, pT_scr, inv_scr = rest
    sb, seq_len, _ = x_ref.shape
    x = x_ref[...].reshape(sb * seq_len, D_MODEL)
    mod = mod_ref[0:1, :]
    _project(x, mod, win_ref, convw_ref, qT_scr, k_scr, vT_scr, conv_scr, seq_len, kT_ref, vT_ref)

    pairs = [(hd, s) for hd in range(N_HEADS) for s in range(sb)]
    chan = lambda hd: slice(hd * HEAD_DIM, (hd + 1) * HEAD_DIM)
    tok = lambda s: slice(s * seq_len, (s + 1) * seq_len)
    for i, (hd, s) in enumerate(pairs):
        sT_scr[i] = _dot(k_scr[hd, tok(s), :], qT_scr[chan(hd), tok(s)])
    for i in range(len(pairs)):
        (pT_scr[i],), inv_scr[i] = _softmax_keys([sT_scr[i]])
    for i, (hd, s) in enumerate(pairs):
        attT_scr[chan(hd), tok(s)] = _dot(vT_scr[chan(hd), tok(s)], pT_scr[i]) * inv_scr[i]

    _merge_and_norm(x, mod, attT_scr, conv_scr, wout_ref, g_ref, b_ref, o_ref)


def _ctx_mixer(l, xp, mod, w_in, w_out, conv_w, ln_g, ln_b, kv_bufs):
    batch, seq_len, _ = xp.shape
    sb = CTX_SEQ_PER_STEP
    m = sb * seq_len
    creates_cache = kv_bufs is None
    kv_shape = jax.ShapeDtypeStruct((batch, DEPTH, N_HEADS, HEAD_DIM, seq_len), F32)
    if creates_cache:
        kv_spec = pl.BlockSpec((sb, DEPTH, N_HEADS, HEAD_DIM, seq_len), lambda i: (i, 0, 0, 0, 0))
    else:
        kv_spec = pl.BlockSpec((sb, None, N_HEADS, HEAD_DIM, seq_len), lambda i: (i, l, 0, 0, 0))
    in_specs = [
        pl.BlockSpec((sb, seq_len, D_MODEL), lambda i: (i, 0, 0)),
        pl.BlockSpec((None, SUBLANES, 6 * D_MODEL), lambda i: (l, 0, 0)),
        _weight_spec((None, D_MODEL, 3 * ATT_WIDTH + 3 * CONV_WIDTH), lambda i: (l, 0, 0)),
        _weight_spec((None, D_MODEL, D_MODEL), lambda i: (l, 0, 0)),
        pl.BlockSpec((None, 3, CONV_WIDTH), lambda i: (l, 0, 0)),
        pl.BlockSpec((None, 1, D_MODEL), lambda i: (l, 0, 0)),
        pl.BlockSpec((None, 1, D_MODEL), lambda i: (l, 0, 0)),
    ]
    args = [xp, mod, w_in, w_out, conv_w, ln_g, ln_b]
    aliases = {}
    if not creates_cache:
        aliases = {len(args): 1, len(args) + 1: 2}
        in_specs += [pl.BlockSpec(memory_space=pl.ANY)] * 2
        args += list(kv_bufs)
    return pl.pallas_call(
        functools.partial(_ctx_mixer_kernel, l, creates_cache),
        grid=(batch // sb,),
        in_specs=in_specs,
        out_specs=[pl.BlockSpec((sb, seq_len, D_MODEL), lambda i: (i, 0, 0)), kv_spec, kv_spec],
        out_shape=[jax.ShapeDtypeStruct(xp.shape, F32), kv_shape, kv_shape],
        input_output_aliases=aliases,
        scratch_shapes=[
            pltpu.VMEM((ATT_WIDTH, m), BF16),
            pltpu.VMEM((N_HEADS, m, HEAD_DIM), BF16),
            pltpu.VMEM((ATT_WIDTH, m), BF16),
            pltpu.VMEM((ATT_WIDTH, m), F32),
            pltpu.VMEM((m, CONV_WIDTH), BF16),
            pltpu.VMEM((N_HEADS * sb, seq_len, seq_len), F32),
            pltpu.VMEM((N_HEADS * sb, seq_len, seq_len), BF16),
            pltpu.VMEM((N_HEADS * sb, 1, seq_len), F32),
        ],
        compiler_params=pltpu.CompilerParams(
            dimension_semantics=("arbitrary",), vmem_limit_bytes=VMEM_LIMIT_BYTES),
        name="ctx_mixer",
    )(*args)


def _window_start(r, rows):
    return min(max(r - WIN_ROWS // 2, 0), rows - WIN_ROWS)


def _chunk_key_rows(r0, chunk_rows, rows):
    lo = _window_start(r0, rows) // 2 * 2
    hi = -(-(_window_start(r0 + chunk_rows - 1, rows) + WIN_ROWS) // 2) * 2
    return lo, hi


def _lat_mixer_kernel(x_ref, mod_ref, win_ref, wout_ref, convw_ref, g_ref, b_ref,
                      ckT_ref, cvT_ref, tbl_ref, o_ref,
                      qT_scr, k_scr, vT_scr, attT_scr, conv_scr, sT_scr, pT_scr, inv_scr):
    b = pl.program_id(0)
    step = pl.program_id(1)
    seq_len = x_ref.shape[0]
    rows = seq_len // GRID_W
    mod = mod_ref[pl.ds(1 + b, 1), :]

    @pl.when(step == 0)
    def _():
        _project(x_ref[...], mod, win_ref, convw_ref, qT_scr, k_scr, vT_scr, conv_scr, seq_len)

    low_half = lax.broadcasted_iota(jnp.int32, (GRID_W, 2 * GRID_W), 1) < GRID_W

    def bias_block(j, r_pair, rk):
        inside = [_window_start(r, rows) <= rk < _window_start(r, rows) + WIN_ROWS for r in (r_pair, r_pair + 1)]
        d = rk - r_pair + WIN_ROWS - 1
        neg = jnp.full((GRID_W, 2 * GRID_W), NEG_INF, F32)
        if not any(inside):
            return neg
        blk = tbl_ref[j, d * GRID_W:(d + 1) * GRID_W, :]
        if all(inside):
            return blk
        return jnp.where(low_half, blk, neg) if inside[0] else jnp.where(low_half, neg, blk)

    chunk_rows = LAT_Q_CHUNK // GRID_W
    n_chunks = seq_len // LAT_Q_CHUNK
    past = ckT_ref.shape[-1]
    max_loc = sT_scr.shape[1] - past
    chunks = []
    for qc in range(n_chunks):
        r0 = qc * chunk_rows
        k_lo, k_hi = _chunk_key_rows(r0, chunk_rows, rows)
        chunks.append((qc, r0, k_lo, k_hi, slice(qc * LAT_Q_CHUNK, (qc + 1) * LAT_Q_CHUNK)))
    for j in range(LAT_HEADS_PER_STEP):
        hd = step * LAT_HEADS_PER_STEP + j
        chan = _head_rows(hd)
        ckT = ckT_ref[j]
        ck = jnp.concatenate([ckT, jnp.zeros_like(ckT)], axis=0).T.astype(BF16)
        cvT = cvT_ref[j].astype(BF16)
        for qc, r0, k_lo, k_hi, qcols in chunks:
            i = j * n_chunks + qc
            qT = qT_scr[chan, qcols]
            bias = jnp.concatenate(
                [jnp.concatenate([bias_block(j, r0 + jj, rk) for jj in range(0, chunk_rows, 2)], axis=1)
                 for rk in range(k_lo, k_hi)], axis=0)
            sT_scr[i, 0:(k_hi - k_lo) * GRID_W] = _dot(k_scr[hd, k_lo * GRID_W:k_hi * GRID_W, :], qT) + bias
            sT_scr[i, max_loc:] = _dot(ck, jnp.concatenate([qT, jnp.zeros_like(qT)], axis=0))
        for qc, r0, k_lo, k_hi, qcols in chunks:
            i = j * n_chunks + qc
            n_loc = (k_hi - k_lo) * GRID_W
            (pT_scr[i, 0:n_loc], pT_scr[i, max_loc:]), inv_scr[i] = _softmax_keys(
                [sT_scr[i, 0:n_loc], sT_scr[i, max_loc:]])
        for qc, r0, k_lo, k_hi, qcols in chunks:
            i = j * n_chunks + qc
            n_loc = (k_hi - k_lo) * GRID_W
            attT_scr[chan, qcols] = (_dot(vT_scr[chan, k_lo * GRID_W:k_hi * GRID_W], pT_scr[i, 0:n_loc])
                                     + _dot(cvT, pT_scr[i, max_loc:])) * inv_scr[i]

    @pl.when(step == pl.num_programs(1) - 1)
    def _():
        _merge_and_norm(x_ref[...], mod, attT_scr, conv_scr, wout_ref, g_ref, b_ref, o_ref)


def _lat_mixer(l, xs, mod, w_in, w_out, conv_w, ln_g, ln_b, cache_kT, cache_vT, tbl):
    batch, seq_len, _ = xs.shape
    past = cache_kT.shape[-1]
    rows, chunk_rows, n_chunks = seq_len // GRID_W, LAT_Q_CHUNK // GRID_W, seq_len // LAT_Q_CHUNK
    max_loc = GRID_W * max(hi - lo for lo, hi in
                           (_chunk_key_rows(qc * chunk_rows, chunk_rows, rows) for qc in range(n_chunks)))
    hps = LAT_HEADS_PER_STEP
    cache_spec = pl.BlockSpec((None, None, hps, HEAD_DIM, past), lambda b, h: (b, l, h, 0, 0))
    return pl.pallas_call(
        _lat_mixer_kernel,
        grid=(batch, N_HEADS // hps),
        in_specs=[
            pl.BlockSpec((None, seq_len, D_MODEL), lambda b, h: (b, 0, 0)),
            pl.BlockSpec((None, SUBLANES, 6 * D_MODEL), lambda b, h: (l, 0, 0)),
            _weight_spec((None, D_MODEL, 3 * ATT_WIDTH + 3 * CONV_WIDTH), lambda b, h: (l, 0, 0)),
            _weight_spec((None, D_MODEL, D_MODEL), lambda b, h: (l, 0, 0)),
            pl.BlockSpec((None, 3, CONV_WIDTH), lambda b, h: (l, 0, 0)),
            pl.BlockSpec((None, 1, D_MODEL), lambda b, h: (l, 0, 0)),
            pl.BlockSpec((None, 1, D_MODEL), lambda b, h: (l, 0, 0)),
            cache_spec, cache_spec,
            pl.BlockSpec((None, hps) + tbl.shape[2:], lambda b, h: (l, h, 0, 0)),
        ],
        out_specs=pl.BlockSpec((None, seq_len, D_MODEL), lambda b, h: (b, 0, 0)),
        out_shape=jax.ShapeDtypeStruct(xs.shape, F32),
        scratch_shapes=[
            pltpu.VMEM((ATT_WIDTH, seq_len), BF16),
            pltpu.VMEM((N_HEADS, seq_len, HEAD_DIM), BF16),
            pltpu.VMEM((ATT_WIDTH, seq_len), BF16),
            pltpu.VMEM((ATT_WIDTH, seq_len), F32),
            pltpu.VMEM((seq_len, CONV_WIDTH), BF16),
            pltpu.VMEM((hps * n_chunks, max_loc + past, LAT_Q_CHUNK), F32),
            pltpu.VMEM((hps * n_chunks, max_loc + past, LAT_Q_CHUNK), BF16),
            pltpu.VMEM((hps * n_chunks, 1, LAT_Q_CHUNK), F32),
        ],
        compiler_params=pltpu.CompilerParams(
            dimension_semantics=("arbitrary", "arbitrary"), vmem_limit_bytes=VMEM_LIMIT_BYTES),
        name="lat_mixer",
    )(xs, mod, w_in, w_out, conv_w, ln_g, ln_b, cache_kT, cache_vT, tbl)


def _bias_table(rpb):
    depth, heads, n_dr, n_dc = rpb.shape
    n_blk = n_dr + 1
    lane0 = GRID_W - WIN_COLS
    n_pad = -(-n_dr // SUBLANES) * SUBLANES
    rpb_pad = jnp.pad(rpb[..., ::-1], ((0, 0), (0, 0), (0, n_pad - n_dr), (lane0, LANES - lane0 - n_dc)))
    rpb_pad = rpb_pad.reshape(depth * heads, n_pad, LANES)

    def body(r_ref, o_ref):
        cp = lax.broadcasted_iota(jnp.int32, (GRID_W, LANES), 0)
        lane = lax.broadcasted_iota(jnp.int32, (GRID_W, LANES), 1)
        low_half = lane < GRID_W
        c = jnp.where(low_half, lane, lane - GRID_W)
        col_start = jnp.clip(c - WIN_COLS // 2, 0, GRID_W - WIN_COLS)
        valid = (cp >= col_start) & (cp < col_start + WIN_COLS)
        def skewed(hd, dr, shift):
            row = jnp.broadcast_to(r_ref[hd, dr:dr + 1, :], (GRID_W, LANES))
            return pltpu.roll(row, shift, 1, stride=1, stride_axis=0)

        for hd in range(heads):
            for d in range(n_blk):
                lo = skewed(hd, d, LANES - GRID_W + 1) if d < n_dr else None
                hi = skewed(hd, d - 1, 1) if d >= 1 else None
                if lo is None:
                    blk = jnp.where(valid & ~low_half, hi, NEG_INF)
                elif hi is None:
                    blk = jnp.where(valid & low_half, lo, NEG_INF)
                else:
                    blk = jnp.where(valid, jnp.where(low_half, lo, hi), NEG_INF)
                o_ref[hd, d * GRID_W:(d + 1) * GRID_W, :] = blk

    tbl = pl.pallas_call(
        body,
        grid=(depth,),
        in_specs=[pl.BlockSpec((heads, n_pad, LANES), lambda i: (i, 0, 0))],
        out_specs=pl.BlockSpec((heads, n_blk * GRID_W, LANES), lambda i: (i, 0, 0)),
        out_shape=jax.ShapeDtypeStruct((depth * heads, n_blk * GRID_W, LANES), F32),
        compiler_params=pltpu.CompilerParams(dimension_semantics=("arbitrary",)),
        name="bias_table",
    )(rpb_pad)
    return tbl.reshape(depth, heads, n_blk * GRID_W, LANES)


def _route_t(logits):
    row = lax.broadcasted_iota(jnp.int32, logits.shape, 0)
    row_f = row.astype(F32)
    big = jnp.float32(LANES)

    def first_row(cond):
        return jnp.min(jnp.where(cond, row_f, big), axis=0, keepdims=True)

    gmask = (row >= N_EXPERTS) & (row < N_EXPERTS + N_GROUPS)
    gl = jnp.where(gmask, logits, NEG_INF)
    gexp = jnp.exp(gl - jnp.max(gl, axis=0, keepdims=True))
    gprob = gexp / jnp.sum(gexp, axis=0, keepdims=True)
    g_p = jnp.max(gprob, axis=0, keepdims=True)
    g_idx = first_row(gmask & (gprob == g_p)) - N_EXPERTS

    row_group = jnp.floor(row_f * (1.0 / EXPERTS_PER_GROUP))
    emask = (row < N_EXPERTS) & (row_group == g_idx)
    el = jnp.where(emask, logits, NEG_INF)
    eexp = jnp.exp(el - jnp.max(el, axis=0, keepdims=True))
    eprob = eexp / jnp.sum(eexp, axis=0, keepdims=True)
    p1 = jnp.max(eprob, axis=0, keepdims=True)
    i1 = first_row(emask & (eprob == p1))
    rest = emask & (row_f != i1)
    p2 = jnp.max(jnp.where(rest, eprob, -1.0), axis=0, keepdims=True)
    i2 = first_row(rest & (eprob == p2))
    denom = p1 + p2
    gate = (jnp.where(row_f == i1, g_p * p1 / denom, 0.0)
            + jnp.where(row_f == i2, g_p * p2 / denom, 0.0))
    return gate, g_idx


def _split3(v):
    hi = v.astype(BF16).astype(F32)
    mid = (v - hi).astype(BF16).astype(F32)
    return hi, mid, v - hi - mid


def _moe_route_tile(x_ref, mod_ref, wrT_ref, brT_ref, h_scr, rec_scr, recT_scr, o_ref, tier_smem):
    tm = x_ref.shape[0]
    seg = tm // mod_ref.shape[0]
    h = jnp.concatenate(
        [x_ref[s * seg:(s + 1) * seg, :] * (1.0 + mod_ref[s, :, 4 * D_MODEL:5 * D_MODEL])
         + mod_ref[s, :, 3 * D_MODEL:4 * D_MODEL] for s in range(tm // seg)], axis=0)
    h_hi = h.astype(BF16)
    h_lo = (h - h_hi.astype(F32)).astype(BF16)
    wr = wrT_ref[...]
    wr_hi = wr.astype(BF16)
    wr_lo = (wr - wr_hi.astype(F32)).astype(BF16)
    logits = (_dot_nt(wr_hi, h_hi) + _dot_nt(wr_hi, h_lo) + _dot_nt(wr_lo, h_hi))[0:ROUTE_ROWS, :]
    gate, g_idx = _route_t(logits + brT_ref[0:ROUTE_ROWS, :])

    row_f = lax.broadcasted_iota(jnp.int32, (ROUTE_ROWS, tm), 0).astype(F32)
    gate4 = jnp.concatenate(
        [jnp.sum(jnp.where(row_f == EXPERTS_PER_GROUP * g_idx + j, gate, 0.0), axis=0, keepdims=True)
         for j in range(EXPERTS_PER_GROUP)], axis=0)

    grp = lax.broadcasted_iota(jnp.int32, (SUBLANES, tm), 0).astype(F32)
    onehot = jnp.where(grp == g_idx, 1.0, 0.0)
    ri = lax.broadcasted_iota(jnp.int32, (MOE_BLOCK, MOE_BLOCK), 0)
    ci = lax.broadcasted_iota(jnp.int32, (MOE_BLOCK, MOE_BLOCK), 1)
    earlier = jnp.where(ri < ci, 1.0, 0.0).astype(BF16)
    ranks = []
    most = jnp.zeros((SUBLANES, 1), F32)
    for b in range(tm // MOE_BLOCK):
        oh_b = onehot[:, b * MOE_BLOCK:(b + 1) * MOE_BLOCK]
        ranks.append(_dot(oh_b.astype(BF16), earlier))
        most = jnp.maximum(most, jnp.sum(oh_b, axis=1, keepdims=True))
    rank = jnp.sum(onehot * jnp.concatenate(ranks, axis=1), axis=0, keepdims=True)

    recT = jnp.concatenate([*_split3(gate4), gate4, g_idx, rank,
                            jnp.zeros((LANES - REC_RANK - 1, tm), F32)], axis=0)
    rec = recT.T
    h_scr[:, 0:D_MODEL] = h_hi
    h_scr[:, D_MODEL:] = rec.astype(BF16)
    rec_scr[...] = rec
    recT_scr[...] = recT[REC_GROUP:REC_GROUP + SUBLANES, :]
    grp_row = lax.broadcasted_iota(jnp.int32, (SUBLANES, 1), 0)
    for gp in range(N_GROUPS):
        fullest = jnp.max(jnp.where(grp_row == gp, most, 0.0))
        tier_smem[gp] = sum((fullest > slot).astype(jnp.int32) for slot in MOE_SLOTS)
    o_ref[...] = jnp.zeros_like(o_ref)


def _moe_kernel(layer, x_ref, mod_ref, wrT_ref, brT_ref, win_hbm, wout_hbm, g_ref, b_ref, o_ref,
                h_scr, rec_scr, recT_scr, act_scr, win_buf, wout_buf, w_sem, tier_smem):
    g = pl.program_id(1)
    g_f = g.astype(F32)
    tm = x_ref.shape[0]
    n_blocks = tm // MOE_BLOCK
    step = pl.program_id(0) * N_GROUPS + g
    buf = step % 2

    def weight_copies(group, buf):
        experts_of_group = pl.ds(group * EXPERTS_PER_GROUP, EXPERTS_PER_GROUP)
        return (pltpu.make_async_copy(win_hbm.at[layer, experts_of_group], win_buf.at[buf], w_sem.at[0, buf]),
                pltpu.make_async_copy(wout_hbm.at[layer, group], wout_buf.at[buf], w_sem.at[1, buf]))

    @pl.when(step == 0)
    def _():
        for copy in weight_copies(0, 0):
            copy.start()

    @pl.when(step + 1 < pl.num_programs(0) * N_GROUPS)
    def _():
        for copy in weight_copies((g + 1) % N_GROUPS, 1 - buf):
            copy.start()

    @pl.when(g == 0)
    def _():
        _moe_route_tile(x_ref, mod_ref, wrT_ref, brT_ref, h_scr, rec_scr, recT_scr, o_ref, tier_smem)

    for copy in weight_copies(g, buf):
        copy.wait()
    win_ref = win_buf.at[buf]
    wout_ref = wout_buf.at[buf]

    def experts(xb, gates, gate_lane, rows):
        for e in range(EXPERTS_PER_GROUP):
            hid = _dot(xb, win_ref[e].astype(BF16))
            act = _silu(hid[:, :D_EXPERT]) * hid[:, D_EXPERT:] * gates[:, gate_lane + e:gate_lane + e + 1]
            act_scr[rows, e * D_EXPERT:(e + 1) * D_EXPERT] = act.astype(BF16)
        return _dot(act_scr[rows, :], wout_ref[...].astype(BF16))

    def compact(slot):
        n_rows = n_blocks * slot
        slot_row = lax.broadcasted_iota(jnp.int32, (slot, MOE_BLOCK), 0).astype(F32)
        gathered = []
        for b in range(n_blocks):
            tok = slice(b * MOE_BLOCK, (b + 1) * MOE_BLOCK)
            pick = (recT_scr[0:1, tok] == g_f) & (recT_scr[1:2, tok] == slot_row)
            gathered.append(_dot(jnp.where(pick, 1.0, 0.0).astype(BF16), h_scr[tok, :]))
        xg = jnp.concatenate(gathered, axis=0)
        ge = xg[:, D_MODEL:]
        gates = ((ge + pltpu.roll(ge, LANES - EXPERTS_PER_GROUP, 1))
                 + pltpu.roll(ge, LANES - 2 * EXPERTS_PER_GROUP, 1))
        y = experts(xg[:, :D_MODEL].astype(BF16), gates, REC_SPLIT, slice(0, n_rows)).astype(BF16)
        slot_col = lax.broadcasted_iota(jnp.int32, (MOE_BLOCK, SCATTER_K), 1).astype(F32)
        for b in range(n_blocks):
            tok = slice(b * MOE_BLOCK, (b + 1) * MOE_BLOCK)
            first = min(b * slot, n_rows - SCATTER_K)
            place = ((rec_scr[tok, REC_GROUP:REC_GROUP + 1] == g_f)
                     & (rec_scr[tok, REC_RANK:REC_RANK + 1] + (b * slot - first) == slot_col))
            o_ref[tok, :] += _dot(jnp.where(place, 1.0, 0.0).astype(BF16), y[first:first + SCATTER_K, :])

    for tier, slot in enumerate(MOE_SLOTS):
        pl.when(tier_smem[g] == tier)(functools.partial(compact, slot))

    @pl.when(tier_smem[g] == len(MOE_SLOTS))
    def _():
        for b in range(n_blocks):
            tok = slice(b * MOE_BLOCK, (b + 1) * MOE_BLOCK)
            gates = jnp.where(rec_scr[tok, REC_GROUP:REC_GROUP + 1] == g_f, rec_scr[tok, :], 0.0)
            o_ref[tok, :] += experts(h_scr[tok, 0:D_MODEL], gates, REC_GATE, slice(0, MOE_BLOCK))

    @pl.when(g == N_GROUPS - 1)
    def _():
        seg = tm // mod_ref.shape[0]
        for s in range(tm // seg):
            tok = slice(s * seg, (s + 1) * seg)
            gf = mod_ref[s, :, 5 * D_MODEL:6 * D_MODEL]
            o_ref[tok, :] = _layer_norm(ALPHA * x_ref[tok, :] + gf * o_ref[tok, :], g_ref[...], b_ref[...])


def _moe(l, x, mod_rows, w_routerT, b_routerT, w_exp_in, w_exp_out, ln_g, ln_b):
    n = x.shape[0]
    tm = MOE_TOKENS
    n_seg = mod_rows.shape[2]
    n_blocks = tm // MOE_BLOCK
    assert tm % MOE_BLOCK == 0
    for slot in MOE_SLOTS:
        assert slot % BF16_ROWS == 0 and slot <= SCATTER_K <= n_blocks * slot
    n_rows = n_blocks * max(MOE_SLOTS)
    d_act = EXPERTS_PER_GROUP * D_EXPERT
    return pl.pallas_call(
        functools.partial(_moe_kernel, l),
        grid=(n // tm, N_GROUPS),
        in_specs=[
            pl.BlockSpec((tm, D_MODEL), lambda i, g: (i, 0)),
            pl.BlockSpec((None, None, n_seg, 1, 6 * D_MODEL), lambda i, g: (l, i, 0, 0, 0)),
            pl.BlockSpec((None, LANES, D_MODEL), lambda i, g: (l, 0, 0)),
            pl.BlockSpec((None, LANES, 1), lambda i, g: (l, 0, 0)),
            pl.BlockSpec(memory_space=pl.ANY),
            pl.BlockSpec(memory_space=pl.ANY),
            pl.BlockSpec((None, 1, D_MODEL), lambda i, g: (l, 0, 0)),
            pl.BlockSpec((None, 1, D_MODEL), lambda i, g: (l, 0, 0)),
        ],
        out_specs=pl.BlockSpec((tm, D_MODEL), lambda i, g: (i, 0)),
        out_shape=jax.ShapeDtypeStruct(x.shape, F32),
        scratch_shapes=[
            pltpu.VMEM((tm, D_MODEL + GATE_COLS), BF16),
            pltpu.VMEM((tm, LANES), F32),
            pltpu.VMEM((SUBLANES, tm), F32),
            pltpu.VMEM((max(n_rows, MOE_BLOCK), d_act), BF16),
            pltpu.VMEM((2, EXPERTS_PER_GROUP, D_MODEL, 2 * D_EXPERT), F32),
            pltpu.VMEM((2, d_act, D_MODEL), F32),
            pltpu.SemaphoreType.DMA((2, 2)),
            pltpu.SMEM((N_GROUPS,), jnp.int32),
        ],
        compiler_params=pltpu.CompilerParams(
            dimension_semantics=("arbitrary", "arbitrary"), vmem_limit_bytes=VMEM_LIMIT_BYTES),
        name="hier_moe",
    )(x, mod_rows, w_routerT, b_routerT, w_exp_in, w_exp_out, ln_g, ln_b)


def kernel(x_prompt, x_sample, cache_k, cache_v, c, c_ctx, w_ada, b_ada, w_in, conv_w, rpb, w_out,
           ln1_g, ln1_b, w_router_group, b_router_group, w_router_expert, b_router_expert,
           w_expert_in, w_expert_out, ln2_g, ln2_b):
    batch, seq_len, _ = x_prompt.shape
    dec_batch, dec_seq, _ = x_sample.shape
    assert dec_batch + 1 <= SUBLANES and MOE_TOKENS % dec_seq == 0
    assert (dec_batch * dec_seq) % MOE_TOKENS == 0 and (batch * seq_len) % MOE_TOKENS == 0

    cond = jnp.concatenate([c_ctx[None, :], c, jnp.zeros((SUBLANES - 1 - dec_batch, D_MODEL), F32)], axis=0)
    mod = _modulation(cond, w_ada, b_ada)
    n_ctx_tiles = batch * seq_len // MOE_TOKENS
    mod_rows_ctx = jnp.broadcast_to(mod[:, 0:1, None, None, :], (DEPTH, n_ctx_tiles, 1, 1, 6 * D_MODEL))
    mod_rows_lat = mod[:, 1:1 + dec_batch].reshape(DEPTH, -1, MOE_TOKENS // dec_seq, 1, 6 * D_MODEL)

    w_in_bf = w_in.astype(BF16)
    w_out_bf = w_out.astype(BF16)
    pad = jnp.zeros((DEPTH, LANES - N_EXPERTS - N_GROUPS, D_MODEL), F32)
    w_routerT = jnp.concatenate(
        [jnp.swapaxes(w_router_expert, 1, 2), jnp.swapaxes(w_router_group, 1, 2), pad], axis=1)
    b_routerT = jnp.concatenate([b_router_expert, b_router_group, pad[:, :, 0]], axis=-1)[:, :, None]
    tbl = _bias_table(rpb)
    cache_kT = jnp.swapaxes(cache_k, -1, -2)
    cache_vT = jnp.swapaxes(cache_v, -1, -2)
    ln1_g3, ln1_b3 = ln1_g[:, None, :], ln1_b[:, None, :]
    ln2_g3, ln2_b3 = ln2_g[:, None, :], ln2_b[:, None, :]

    w_exp_out = w_expert_out.reshape(DEPTH, N_GROUPS, EXPERTS_PER_GROUP * D_EXPERT, D_MODEL)

    xp, xs = x_prompt, x_sample
    kv_bufs = None
    for l in range(DEPTH):
        moe = functools.partial(_moe, l, w_routerT=w_routerT, b_routerT=b_routerT, w_exp_in=w_expert_in,
                                w_exp_out=w_exp_out, ln_g=ln2_g3, ln_b=ln2_b3)
        xp, *kv_bufs = _ctx_mixer(l, xp, mod, w_in_bf, w_out_bf, conv_w, ln1_g3, ln1_b3, kv_bufs)
        xp = moe(xp.reshape(-1, D_MODEL), mod_rows_ctx).reshape(xp.shape)
        xs = _lat_mixer(l, xs, mod, w_in_bf, w_out_bf, conv_w, ln1_g3, ln1_b3, cache_kT, cache_vT, tbl)
        xs = moe(xs.reshape(-1, D_MODEL), mod_rows_lat).reshape(xs.shape)
    new_kT, new_vT = kv_bufs
    return (xp, xs, jnp.swapaxes(new_kT, -1, -2), jnp.swapaxes(new_vT, -1, -2))
```

```python
import functools

import jax
import jax.numpy as jnp
from jax import lax
from jax.experimental import pallas as pl
from jax.experimental.pallas import tpu as pltpu

D_MODEL = 1024
DEPTH = 4
GRID_W = 64
ATT_WIDTH = D_MODEL // 2
CONV_WIDTH = D_MODEL - ATT_WIDTH
HEAD_DIM = 64
N_HEADS = ATT_WIDTH // HEAD_DIM
WIN_ROWS = 8
WIN_COLS = 16
N_GROUPS = 4
EXPERTS_PER_GROUP = 4
N_EXPERTS = N_GROUPS * EXPERTS_PER_GROUP
D_EXPERT = D_MODEL // 4
ALPHA = (2 * DEPTH) ** 0.25
LN_EPS = 1e-5
NEG_INF = -1e30
QK_SCALE = HEAD_DIM ** -0.5

F32 = jnp.float32
BF16 = jnp.bfloat16

LANES = 128
SUBLANES = 8
BF16_ROWS = 16
VMEM_LIMIT_BYTES = 58 * 1024 * 1024

CTX_SEQ_PER_STEP = 2
LAT_Q_CHUNK = 128
LAT_HEADS_PER_STEP = 2
MERGE_ROWS = 256
MOE_TOKENS = 1024
MOE_BLOCK = 512
MOE_SLOTS = (144, 192, 256)
SCATTER_K = 256
GATE_COLS = 128
ROUTE_ROWS = 24
REC_SPLIT, REC_GATE, REC_GROUP, REC_RANK = 0, 12, 16, 17
MOD_COLS = 2048


def _dot(a, b):
    return jnp.dot(a, b, preferred_element_type=F32)


def _dot_nt(a, b):
    return lax.dot_general(a, b, (((1,), (1,)), ((), ())), preferred_element_type=F32)


def _silu(x):
    return x * (1.0 / (1.0 + jnp.exp(-x)))


def _layer_norm(r, g, b):
    mu = jnp.mean(r, axis=-1, keepdims=True)
    d = r - mu
    var = jnp.mean(d * d, axis=-1, keepdims=True)
    return d * lax.rsqrt(var + LN_EPS) * g + b


def _mod_kernel(cond_ref, w_ref, b_ref, o_ref):
    s = _silu(cond_ref[...]).astype(BF16)
    o_ref[...] = _dot(s, w_ref[...].astype(BF16)) + b_ref[...]


def _modulation(cond, w_ada, b_ada):
    n_out = w_ada.shape[-1]
    return pl.pallas_call(
        _mod_kernel,
        grid=(DEPTH, n_out // MOD_COLS),
        in_specs=[
            pl.BlockSpec((SUBLANES, D_MODEL), lambda l, j: (0, 0)),
            pl.BlockSpec((None, D_MODEL, MOD_COLS), lambda l, j: (l, 0, j)),
            pl.BlockSpec((None, 1, MOD_COLS), lambda l, j: (l, 0, j)),
        ],
        out_specs=pl.BlockSpec((None, SUBLANES, MOD_COLS), lambda l, j: (l, 0, j)),
        out_shape=jax.ShapeDtypeStruct((DEPTH, SUBLANES, n_out), F32),
        compiler_params=pltpu.CompilerParams(
            dimension_semantics=("arbitrary", "arbitrary"),
            vmem_limit_bytes=VMEM_LIMIT_BYTES),
        name="adaln_modulation",
    )(cond, w_ada, b_ada.reshape(DEPTH, 1, n_out))


def _project(x, mod, win_ref, convw_ref, qT_scr, k_scr, vT_scr, conv_scr, seq_len, kT_out=None, vT_out=None):
    m = x.shape[0]
    sa = mod[:, 0:D_MODEL]
    ca = mod[:, D_MODEL:2 * D_MODEL]
    h = (x * (1.0 + ca) + sa).astype(BF16)

    qT_scr[...] = (_dot(h, win_ref[:, 0:ATT_WIDTH]) * QK_SCALE).T.astype(BF16)
    zk = _dot(h, win_ref[:, ATT_WIDTH:2 * ATT_WIDTH])
    for hd in range(N_HEADS):
        k_scr[hd] = zk[:, hd * HEAD_DIM:(hd + 1) * HEAD_DIM].astype(BF16)
    zvT = _dot(h, win_ref[:, 2 * ATT_WIDTH:3 * ATT_WIDTH]).T
    vT_scr[...] = zvT.astype(BF16)
    if kT_out is not None:
        zkT = zk.T
        for out, zT in ((kT_out, zkT), (vT_out, zvT)):
            for s in range(m // seq_len):
                for hd in range(N_HEADS):
                    out[s, hd] = zT[hd * HEAD_DIM:(hd + 1) * HEAD_DIM, s * seq_len:(s + 1) * seq_len]

    c0 = 3 * ATT_WIDTH
    bg = _dot(h, win_ref[:, c0:c0 + CONV_WIDTH])
    cg = _dot(h, win_ref[:, c0 + CONV_WIDTH:c0 + 2 * CONV_WIDTH])
    u = _dot(h, win_ref[:, c0 + 2 * CONV_WIDTH:c0 + 3 * CONV_WIDTH])
    y = cg * u
    t = lax.broadcasted_iota(jnp.int32, (m, 1), 0) % seq_len
    y_prev = jnp.where(t == 0, 0.0, pltpu.roll(y, 1, 0))
    y_next = jnp.where(t == seq_len - 1, 0.0, pltpu.roll(y, m - 1, 0))
    cw = convw_ref[...]
    conv = cw[0:1, :] * y_prev + cw[1:2, :] * y + cw[2:3, :] * y_next
    conv_scr[...] = (bg * conv).astype(BF16)


def _merge_and_norm(x, mod, attT_scr, conv_scr, wout_ref, g_ref, b_ref, o_ref):
    ga = mod[:, 2 * D_MODEL:3 * D_MODEL]
    out = []
    for r in range(0, x.shape[0], MERGE_ROWS):
        rows = slice(r, r + MERGE_ROWS)
        att = attT_scr[:, rows].T.astype(BF16)
        mix = _dot(att, wout_ref[0:ATT_WIDTH, :]) + _dot(conv_scr[rows, :], wout_ref[ATT_WIDTH:, :])
        out.append(_layer_norm(ALPHA * x[rows, :] + ga * mix, g_ref[...], b_ref[...]))
    o_ref[...] = jnp.concatenate(out, axis=0).reshape(o_ref.shape)


def _softmax_keys(parts):
    mx = functools.reduce(jnp.maximum, [jnp.max(p, axis=0, keepdims=True) for p in parts])
    es = [jnp.exp(p - mx) for p in parts]
    inv = 1.0 / functools.reduce(jnp.add, [jnp.sum(e, axis=0, keepdims=True) for e in es])
    return [e.astype(BF16) for e in es], inv


def _head_rows(hd):
    return pl.ds(pl.multiple_of(hd * HEAD_DIM, HEAD_DIM), HEAD_DIM)


def _weight_spec(shape, index_map):
    return pl.BlockSpec(shape, index_map, pipeline_mode=pl.Buffered(1))


def _ctx_mixer_kernel(l, creates_cache, x_ref, mod_ref, win_ref, wout_ref, convw_ref, g_ref, b_ref, *rest):
    if creates_cache:
        o_ref, kT_all, vT_all, qT_scr, k_scr, vT_scr, attT_scr, conv_scr, sT_scr, pT_scr, inv_scr = rest
        for ref in (kT_all, vT_all):
            for other in range(DEPTH):
                if other != l:
                    ref[:, other] = jnp.zeros(ref.shape[:1] + ref.shape[2:], F32)
        kT_ref, vT_ref = kT_all.at[:, l], vT_all.at[:, l]
    else:
        _, _, o_ref, kT_ref, vT_ref, qT_scr, k_scr, vT_scr, attT_scr, conv_scr, sT_scr, pT_scr, inv_scr = rest
    sb, seq_len, _ = o_ref.shape
    x = x_ref[...]
    mod = mod_ref[0:1, :]
    _project(x, mod, win_ref, convw_ref, qT_scr, k_scr, vT_scr, conv_scr, seq_len, kT_ref, vT_ref)

    pairs = [(hd, s) for hd in range(N_HEADS) for s in range(sb)]
    chan = lambda hd: slice(hd * HEAD_DIM, (hd + 1) * HEAD_DIM)
    tok = lambda s: slice(s * seq_len, (s + 1) * seq_len)
    for i, (hd, s) in enumerate(pairs):
        sT_scr[i] = _dot(k_scr[hd, tok(s), :], qT_scr[chan(hd), tok(s)])
    for i in range(len(pairs)):
        (pT_scr[i],), inv_scr[i] = _softmax_keys([sT_scr[i]])
    for i, (hd, s) in enumerate(pairs):
        attT_scr[chan(hd), tok(s)] = _dot(vT_scr[chan(hd), tok(s)], pT_scr[i]) * inv_scr[i]

    _merge_and_norm(x, mod, attT_scr, conv_scr, wout_ref, g_ref, b_ref, o_ref)


def _ctx_mixer(l, tokens, xp_shape, mod, w_in, w_out, conv_w, ln_g, ln_b, kv_bufs):
    batch, seq_len, _ = xp_shape
    sb = CTX_SEQ_PER_STEP
    m = sb * seq_len
    x2d, first_row = tokens
    assert first_row % m == 0
    first_blk = first_row // m
    creates_cache = kv_bufs is None
    kv_shape = jax.ShapeDtypeStruct((batch, DEPTH, N_HEADS, HEAD_DIM, seq_len), F32)
    if creates_cache:
        kv_spec = pl.BlockSpec((sb, DEPTH, N_HEADS, HEAD_DIM, seq_len), lambda i: (i, 0, 0, 0, 0))
    else:
        kv_spec = pl.BlockSpec((sb, None, N_HEADS, HEAD_DIM, seq_len), lambda i: (i, l, 0, 0, 0))
    in_specs = [
        pl.BlockSpec((m, D_MODEL), lambda i: (first_blk + i, 0)),
        pl.BlockSpec((None, SUBLANES, 6 * D_MODEL), lambda i: (l, 0, 0)),
        _weight_spec((None, D_MODEL, 3 * ATT_WIDTH + 3 * CONV_WIDTH), lambda i: (l, 0, 0)),
        _weight_spec((None, D_MODEL, D_MODEL), lambda i: (l, 0, 0)),
        pl.BlockSpec((None, 3, CONV_WIDTH), lambda i: (l, 0, 0)),
        pl.BlockSpec((None, 1, D_MODEL), lambda i: (l, 0, 0)),
        pl.BlockSpec((None, 1, D_MODEL), lambda i: (l, 0, 0)),
    ]
    args = [x2d, mod, w_in, w_out, conv_w, ln_g, ln_b]
    aliases = {}
    if not creates_cache:
        aliases = {len(args): 1, len(args) + 1: 2}
        in_specs += [pl.BlockSpec(memory_space=pl.ANY)] * 2
        args += list(kv_bufs)
    return pl.pallas_call(
        functools.partial(_ctx_mixer_kernel, l, creates_cache),
        grid=(batch // sb,),
        in_specs=in_specs,
        out_specs=[pl.BlockSpec((sb, seq_len, D_MODEL), lambda i: (i, 0, 0)), kv_spec, kv_spec],
        out_shape=[jax.ShapeDtypeStruct(xp_shape, F32), kv_shape, kv_shape],
        input_output_aliases=aliases,
        scratch_shapes=[
            pltpu.VMEM((ATT_WIDTH, m), BF16),
            pltpu.VMEM((N_HEADS, m, HEAD_DIM), BF16),
            pltpu.VMEM((ATT_WIDTH, m), BF16),
            pltpu.VMEM((ATT_WIDTH, m), F32),
            pltpu.VMEM((m, CONV_WIDTH), BF16),
            pltpu.VMEM((N_HEADS * sb, seq_len, seq_len), F32),
            pltpu.VMEM((N_HEADS * sb, seq_len, seq_len), BF16),
            pltpu.VMEM((N_HEADS * sb, 1, seq_len), F32),
        ],
        compiler_params=pltpu.CompilerParams(
            dimension_semantics=("arbitrary",), vmem_limit_bytes=VMEM_LIMIT_BYTES),
        name="ctx_mixer",
    )(*args)


def _window_start(r, rows):
    return min(max(r - WIN_ROWS // 2, 0), rows - WIN_ROWS)


def _chunk_key_rows(r0, chunk_rows, rows):
    lo = _window_start(r0, rows) // 2 * 2
    hi = -(-(_window_start(r0 + chunk_rows - 1, rows) + WIN_ROWS) // 2) * 2
    return lo, hi


def _lat_mixer_kernel(x_ref, mod_ref, win_ref, wout_ref, convw_ref, g_ref, b_ref,
                      ckT_ref, cvT_ref, tbl_ref, o_ref,
                      qT_scr, k_scr, vT_scr, attT_scr, conv_scr, sT_scr, pT_scr, inv_scr):
    b = pl.program_id(0)
    step = pl.program_id(1)
    seq_len = x_ref.shape[0]
    rows = seq_len // GRID_W
    mod = mod_ref[pl.ds(1 + b, 1), :]

    @pl.when(step == 0)
    def _():
        _project(x_ref[...], mod, win_ref, convw_ref, qT_scr, k_scr, vT_scr, conv_scr, seq_len)

    low_half = lax.broadcasted_iota(jnp.int32, (GRID_W, 2 * GRID_W), 1) < GRID_W

    def bias_block(j, r_pair, rk):
        inside = [_window_start(r, rows) <= rk < _window_start(r, rows) + WIN_ROWS for r in (r_pair, r_pair + 1)]
        d = rk - r_pair + WIN_ROWS - 1
        neg = jnp.full((GRID_W, 2 * GRID_W), NEG_INF, F32)
        if not any(inside):
            return neg
        blk = tbl_ref[j, d * GRID_W:(d + 1) * GRID_W, :]
        if all(inside):
            return blk
        return jnp.where(low_half, blk, neg) if inside[0] else jnp.where(low_half, neg, blk)

    chunk_rows = LAT_Q_CHUNK // GRID_W
    n_chunks = seq_len // LAT_Q_CHUNK
    past = ckT_ref.shape[-1]
    max_loc = sT_scr.shape[1] - past
    chunks = []
    for qc in range(n_chunks):
        r0 = qc * chunk_rows
        k_lo, k_hi = _chunk_key_rows(r0, chunk_rows, rows)
        chunks.append((qc, r0, k_lo, k_hi, slice(qc * LAT_Q_CHUNK, (qc + 1) * LAT_Q_CHUNK)))
    for j in range(LAT_HEADS_PER_STEP):
        hd = step * LAT_HEADS_PER_STEP + j
        chan = _head_rows(hd)
        ckT = ckT_ref[j]
        ck = jnp.concatenate([ckT, jnp.zeros_like(ckT)], axis=0).T.astype(BF16)
        cvT = cvT_ref[j].astype(BF16)
        for qc, r0, k_lo, k_hi, qcols in chunks:
            i = j * n_chunks + qc
            qT = qT_scr[chan, qcols]
            bias = jnp.concatenate(
                [jnp.concatenate([bias_block(j, r0 + jj, rk) for jj in range(0, chunk_rows, 2)], axis=1)
                 for rk in range(k_lo, k_hi)], axis=0)
            sT_scr[i, 0:(k_hi - k_lo) * GRID_W] = _dot(k_scr[hd, k_lo * GRID_W:k_hi * GRID_W, :], qT) + bias
            sT_scr[i, max_loc:] = _dot(ck, jnp.concatenate([qT, jnp.zeros_like(qT)], axis=0))
        for qc, r0, k_lo, k_hi, qcols in chunks:
            i = j * n_chunks + qc
            n_loc = (k_hi - k_lo) * GRID_W
            (pT_scr[i, 0:n_loc], pT_scr[i, max_loc:]), inv_scr[i] = _softmax_keys(
                [sT_scr[i, 0:n_loc], sT_scr[i, max_loc:]])
        for qc, r0, k_lo, k_hi, qcols in chunks:
            i = j * n_chunks + qc
            n_loc = (k_hi - k_lo) * GRID_W
            attT_scr[chan, qcols] = (_dot(vT_scr[chan, k_lo * GRID_W:k_hi * GRID_W], pT_scr[i, 0:n_loc])
                                     + _dot(cvT, pT_scr[i, max_loc:])) * inv_scr[i]

    @pl.when(step == pl.num_programs(1) - 1)
    def _():
        _merge_and_norm(x_ref[...], mod, attT_scr, conv_scr, wout_ref, g_ref, b_ref, o_ref)


def _lat_mixer(l, tokens, xs_shape, mod, w_in, w_out, conv_w, ln_g, ln_b, cache_kT, cache_vT, tbl):
    batch, seq_len, _ = xs_shape
    x2d, first_row = tokens
    assert first_row % seq_len == 0
    first_blk = first_row // seq_len
    past = cache_kT.shape[-1]
    rows, chunk_rows, n_chunks = seq_len // GRID_W, LAT_Q_CHUNK // GRID_W, seq_len // LAT_Q_CHUNK
    max_loc = GRID_W * max(hi - lo for lo, hi in
                           (_chunk_key_rows(qc * chunk_rows, chunk_rows, rows) for qc in range(n_chunks)))
    hps = LAT_HEADS_PER_STEP
    cache_spec = pl.BlockSpec((None, None, hps, HEAD_DIM, past), lambda b, h: (b, l, h, 0, 0))
    return pl.pallas_call(
        _lat_mixer_kernel,
        grid=(batch, N_HEADS // hps),
        in_specs=[
            pl.BlockSpec((seq_len, D_MODEL), lambda b, h: (first_blk + b, 0)),
            pl.BlockSpec((None, SUBLANES, 6 * D_MODEL), lambda b, h: (l, 0, 0)),
            _weight_spec((None, D_MODEL, 3 * ATT_WIDTH + 3 * CONV_WIDTH), lambda b, h: (l, 0, 0)),
            _weight_spec((None, D_MODEL, D_MODEL), lambda b, h: (l, 0, 0)),
            pl.BlockSpec((None, 3, CONV_WIDTH), lambda b, h: (l, 0, 0)),
            pl.BlockSpec((None, 1, D_MODEL), lambda b, h: (l, 0, 0)),
            pl.BlockSpec((None, 1, D_MODEL), lambda b, h: (l, 0, 0)),
            cache_spec, cache_spec,
            pl.BlockSpec((None, hps) + tbl.shape[2:], lambda b, h: (l, h, 0, 0)),
        ],
        out_specs=pl.BlockSpec((None, seq_len, D_MODEL), lambda b, h: (b, 0, 0)),
        out_shape=jax.ShapeDtypeStruct(xs_shape, F32),
        scratch_shapes=[
            pltpu.VMEM((ATT_WIDTH, seq_len), BF16),
            pltpu.VMEM((N_HEADS, seq_len, HEAD_DIM), BF16),
            pltpu.VMEM((ATT_WIDTH, seq_len), BF16),
            pltpu.VMEM((ATT_WIDTH, seq_len), F32),
            pltpu.VMEM((seq_len, CONV_WIDTH), BF16),
            pltpu.VMEM((hps * n_chunks, max_loc + past, LAT_Q_CHUNK), F32),
            pltpu.VMEM((hps * n_chunks, max_loc + past, LAT_Q_CHUNK), BF16),
            pltpu.VMEM((hps * n_chunks, 1, LAT_Q_CHUNK), F32),
        ],
        compiler_params=pltpu.CompilerParams(
            dimension_semantics=("arbitrary", "arbitrary"), vmem_limit_bytes=VMEM_LIMIT_BYTES),
        name="lat_mixer",
    )(x2d, mod, w_in, w_out, conv_w, ln_g, ln_b, cache_kT, cache_vT, tbl)


def _bias_table(rpb):
    depth, heads, n_dr, n_dc = rpb.shape
    n_blk = n_dr + 1
    lane0 = GRID_W - WIN_COLS
    n_pad = -(-n_dr // SUBLANES) * SUBLANES
    rpb_pad = jnp.pad(rpb[..., ::-1], ((0, 0), (0, 0), (0, n_pad - n_dr), (lane0, LANES - lane0 - n_dc)))
    rpb_pad = rpb_pad.reshape(depth * heads, n_pad, LANES)

    def body(r_ref, o_ref):
        cp = lax.broadcasted_iota(jnp.int32, (GRID_W, LANES), 0)
        lane = lax.broadcasted_iota(jnp.int32, (GRID_W, LANES), 1)
        low_half = lane < GRID_W
        c = jnp.where(low_half, lane, lane - GRID_W)
        col_start = jnp.clip(c - WIN_COLS // 2, 0, GRID_W - WIN_COLS)
        valid = (cp >= col_start) & (cp < col_start + WIN_COLS)
        def skewed(hd, dr, shift):
            row = jnp.broadcast_to(r_ref[hd, dr:dr + 1, :], (GRID_W, LANES))
            return pltpu.roll(row, shift, 1, stride=1, stride_axis=0)

        for hd in range(heads):
            for d in range(n_blk):
                lo = skewed(hd, d, LANES - GRID_W + 1) if d < n_dr else None
                hi = skewed(hd, d - 1, 1) if d >= 1 else None
                if lo is None:
                    blk = jnp.where(valid & ~low_half, hi, NEG_INF)
                elif hi is None:
                    blk = jnp.where(valid & low_half, lo, NEG_INF)
                else:
                    blk = jnp.where(valid, jnp.where(low_half, lo, hi), NEG_INF)
                o_ref[hd, d * GRID_W:(d + 1) * GRID_W, :] = blk

    tbl = pl.pallas_call(
        body,
        grid=(depth,),
        in_specs=[pl.BlockSpec((heads, n_pad, LANES), lambda i: (i, 0, 0))],
        out_specs=pl.BlockSpec((heads, n_blk * GRID_W, LANES), lambda i: (i, 0, 0)),
        out_shape=jax.ShapeDtypeStruct((depth * heads, n_blk * GRID_W, LANES), F32),
        compiler_params=pltpu.CompilerParams(dimension_semantics=("arbitrary",)),
        name="bias_table",
    )(rpb_pad)
    return tbl.reshape(depth, heads, n_blk * GRID_W, LANES)


def _route_t(logits):
    row = lax.broadcasted_iota(jnp.int32, logits.shape, 0)
    row_f = row.astype(F32)
    big = jnp.float32(LANES)

    def first_row(cond):
        return jnp.min(jnp.where(cond, row_f, big), axis=0, keepdims=True)

    gmask = (row >= N_EXPERTS) & (row < N_EXPERTS + N_GROUPS)
    gl = jnp.where(gmask, logits, NEG_INF)
    gexp = jnp.exp(gl - jnp.max(gl, axis=0, keepdims=True))
    gprob = gexp / jnp.sum(gexp, axis=0, keepdims=True)
    g_p = jnp.max(gprob, axis=0, keepdims=True)
    g_idx = first_row(gmask & (gprob == g_p)) - N_EXPERTS

    row_group = jnp.floor(row_f * (1.0 / EXPERTS_PER_GROUP))
    emask = (row < N_EXPERTS) & (row_group == g_idx)
    el = jnp.where(emask, logits, NEG_INF)
    eexp = jnp.exp(el - jnp.max(el, axis=0, keepdims=True))
    eprob = eexp / jnp.sum(eexp, axis=0, keepdims=True)
    p1 = jnp.max(eprob, axis=0, keepdims=True)
    i1 = first_row(emask & (eprob == p1))
    rest = emask & (row_f != i1)
    p2 = jnp.max(jnp.where(rest, eprob, -1.0), axis=0, keepdims=True)
    i2 = first_row(rest & (eprob == p2))
    denom = p1 + p2
    gate = (jnp.where(row_f == i1, g_p * p1 / denom, 0.0)
            + jnp.where(row_f == i2, g_p * p2 / denom, 0.0))
    return gate, g_idx


def _split3(v):
    hi = v.astype(BF16).astype(F32)
    mid = (v - hi).astype(BF16).astype(F32)
    return hi, mid, v - hi - mid


def _moe_route_tile(x_ref, mod_ref, wrT_ref, brT_ref, h_scr, rec_scr, recT_scr, o_ref, tier_smem):
    tm = x_ref.shape[0]
    seg = tm // mod_ref.shape[0]
    h = jnp.concatenate(
        [x_ref[s * seg:(s + 1) * seg, :] * (1.0 + mod_ref[s, :, 4 * D_MODEL:5 * D_MODEL])
         + mod_ref[s, :, 3 * D_MODEL:4 * D_MODEL] for s in range(tm // seg)], axis=0)
    h_hi = h.astype(BF16)
    h_lo = (h - h_hi.astype(F32)).astype(BF16)
    wr = wrT_ref[...]
    wr_hi = wr.astype(BF16)
    wr_lo = (wr - wr_hi.astype(F32)).astype(BF16)
    logits = (_dot_nt(wr_hi, h_hi) + _dot_nt(wr_hi, h_lo) + _dot_nt(wr_lo, h_hi))[0:ROUTE_ROWS, :]
    gate, g_idx = _route_t(logits + brT_ref[0:ROUTE_ROWS, :])

    row_f = lax.broadcasted_iota(jnp.int32, (ROUTE_ROWS, tm), 0).astype(F32)
    gate4 = jnp.concatenate(
        [jnp.sum(jnp.where(row_f == EXPERTS_PER_GROUP * g_idx + j, gate, 0.0), axis=0, keepdims=True)
         for j in range(EXPERTS_PER_GROUP)], axis=0)

    grp = lax.broadcasted_iota(jnp.int32, (SUBLANES, tm), 0).astype(F32)
    onehot = jnp.where(grp == g_idx, 1.0, 0.0)
    ri = lax.broadcasted_iota(jnp.int32, (MOE_BLOCK, MOE_BLOCK), 0)
    ci = lax.broadcasted_iota(jnp.int32, (MOE_BLOCK, MOE_BLOCK), 1)
    earlier = jnp.where(ri < ci, 1.0, 0.0).astype(BF16)
    ranks = []
    most = jnp.zeros((SUBLANES, 1), F32)
    for b in range(tm // MOE_BLOCK):
        oh_b = onehot[:, b * MOE_BLOCK:(b + 1) * MOE_BLOCK]
        ranks.append(_dot(oh_b.astype(BF16), earlier))
        most = jnp.maximum(most, jnp.sum(oh_b, axis=1, keepdims=True))
    rank = jnp.sum(onehot * jnp.concatenate(ranks, axis=1), axis=0, keepdims=True)

    recT = jnp.concatenate([*_split3(gate4), gate4, g_idx, rank,
                            jnp.zeros((LANES - REC_RANK - 1, tm), F32)], axis=0)
    rec = recT.T
    h_scr[:, 0:D_MODEL] = h_hi
    h_scr[:, D_MODEL:] = rec.astype(BF16)
    rec_scr[...] = rec
    recT_scr[...] = recT[REC_GROUP:REC_GROUP + SUBLANES, :]
    grp_row = lax.broadcasted_iota(jnp.int32, (SUBLANES, 1), 0)
    for gp in range(N_GROUPS):
        fullest = jnp.max(jnp.where(grp_row == gp, most, 0.0))
        tier_smem[gp] = sum((fullest > slot).astype(jnp.int32) for slot in MOE_SLOTS)
    o_ref[...] = jnp.zeros_like(o_ref)


class _TileOf:
    def __init__(self, refs, first_tiles, tile):
        self.refs, self.first_tiles, self.tile = refs, first_tiles, tile
        self.shape = refs[0].shape

    def __getitem__(self, idx):
        val = self.refs[0][idx]
        for ref, first in zip(self.refs[1:], self.first_tiles[1:]):
            val = jnp.where(self.tile >= first, ref[idx], val)
        return val


def _moe_kernel(layer, first_tiles, *refs):
    (mod_ref, wrT_ref, brT_ref, win_hbm, wout_hbm, g_ref, b_ref, o_ref,
     h_scr, rec_scr, recT_scr, act_scr, win_buf, wout_buf, w_sem, tier_smem) = refs[len(first_tiles):]
    x_ref = _TileOf(refs[:len(first_tiles)], first_tiles, pl.program_id(0))
    g = pl.program_id(1)
    g_f = g.astype(F32)
    tm = x_ref.shape[0]
    n_blocks = tm // MOE_BLOCK
    step = pl.program_id(0) * N_GROUPS + g
    buf = step % 2

    def weight_copies(group, buf):
        experts_of_group = pl.ds(group * EXPERTS_PER_GROUP, EXPERTS_PER_GROUP)
        return (pltpu.make_async_copy(win_hbm.at[layer, experts_of_group], win_buf.at[buf], w_sem.at[0, buf]),
                pltpu.make_async_copy(wout_hbm.at[layer, group], wout_buf.at[buf], w_sem.at[1, buf]))

    @pl.when(step == 0)
    def _():
        for copy in weight_copies(0, 0):
            copy.start()

    @pl.when(step + 1 < pl.num_programs(0) * N_GROUPS)
    def _():
        for copy in weight_copies((g + 1) % N_GROUPS, 1 - buf):
            copy.start()

    @pl.when(g == 0)
    def _():
        _moe_route_tile(x_ref, mod_ref, wrT_ref, brT_ref, h_scr, rec_scr, recT_scr, o_ref, tier_smem)

    for copy in weight_copies(g, buf):
        copy.wait()
    win_ref = win_buf.at[buf]
    wout_ref = wout_buf.at[buf]

    def experts(xb, gates, gate_lane, rows):
        for e in range(EXPERTS_PER_GROUP):
            hid = _dot(xb, win_ref[e].astype(BF16))
            act = _silu(hid[:, :D_EXPERT]) * hid[:, D_EXPERT:] * gates[:, gate_lane + e:gate_lane + e + 1]
            act_scr[rows, e * D_EXPERT:(e + 1) * D_EXPERT] = act.astype(BF16)
        return _dot(act_scr[rows, :], wout_ref[...].astype(BF16))

    def compact(slot):
        n_rows = n_blocks * slot
        slot_row = lax.broadcasted_iota(jnp.int32, (slot, MOE_BLOCK), 0).astype(F32)
        gathered = []
        for b in range(n_blocks):
            tok = slice(b * MOE_BLOCK, (b + 1) * MOE_BLOCK)
            pick = (recT_scr[0:1, tok] == g_f) & (recT_scr[1:2, tok] == slot_row)
            gathered.append(_dot(jnp.where(pick, 1.0, 0.0).astype(BF16), h_scr[tok, :]))
        xg = jnp.concatenate(gathered, axis=0)
        ge = xg[:, D_MODEL:]
        gates = ((ge + pltpu.roll(ge, LANES - EXPERTS_PER_GROUP, 1))
                 + pltpu.roll(ge, LANES - 2 * EXPERTS_PER_GROUP, 1))
        y = experts(xg[:, :D_MODEL].astype(BF16), gates, REC_SPLIT, slice(0, n_rows)).astype(BF16)
        slot_col = lax.broadcasted_iota(jnp.int32, (MOE_BLOCK, SCATTER_K), 1).astype(F32)
        for b in range(n_blocks):
            tok = slice(b * MOE_BLOCK, (b + 1) * MOE_BLOCK)
            first = min(b * slot, n_rows - SCATTER_K)
            place = ((rec_scr[tok, REC_GROUP:REC_GROUP + 1] == g_f)
                     & (rec_scr[tok, REC_RANK:REC_RANK + 1] + (b * slot - first) == slot_col))
            o_ref[tok, :] += _dot(jnp.where(place, 1.0, 0.0).astype(BF16), y[first:first + SCATTER_K, :])

    for tier, slot in enumerate(MOE_SLOTS):
        pl.when(tier_smem[g] == tier)(functools.partial(compact, slot))

    @pl.when(tier_smem[g] == len(MOE_SLOTS))
    def _():
        for b in range(n_blocks):
            tok = slice(b * MOE_BLOCK, (b + 1) * MOE_BLOCK)
            gates = jnp.where(rec_scr[tok, REC_GROUP:REC_GROUP + 1] == g_f, rec_scr[tok, :], 0.0)
            o_ref[tok, :] += experts(h_scr[tok, 0:D_MODEL], gates, REC_GATE, slice(0, MOE_BLOCK))

    @pl.when(g == N_GROUPS - 1)
    def _():
        seg = tm // mod_ref.shape[0]
        for s in range(tm // seg):
            tok = slice(s * seg, (s + 1) * seg)
            gf = mod_ref[s, :, 5 * D_MODEL:6 * D_MODEL]
            o_ref[tok, :] = _layer_norm(ALPHA * x_ref[tok, :] + gf * o_ref[tok, :], g_ref[...], b_ref[...])


def _moe(l, xs, mod_rows, w_routerT, b_routerT, w_exp_in, w_exp_out, ln_g, ln_b):
    tm = MOE_TOKENS
    tiles = [x.shape[0] // tm for x in xs]
    first_tiles = tuple(sum(tiles[:k]) for k in range(len(xs)))
    n = tm * sum(tiles)
    x_specs = [pl.BlockSpec((tm, D_MODEL), lambda i, g, first=first, n_k=n_k: (jnp.clip(i - first, 0, n_k - 1), 0))
               for first, n_k in zip(first_tiles, tiles)]
    n_seg = mod_rows.shape[2]
    n_blocks = tm // MOE_BLOCK
    assert tm % MOE_BLOCK == 0
    for slot in MOE_SLOTS:
        assert slot % BF16_ROWS == 0 and slot <= SCATTER_K <= n_blocks * slot
    n_rows = n_blocks * max(MOE_SLOTS)
    d_act = EXPERTS_PER_GROUP * D_EXPERT
    return pl.pallas_call(
        functools.partial(_moe_kernel, l, first_tiles),
        grid=(n // tm, N_GROUPS),
        in_specs=x_specs + [
            pl.BlockSpec((None, None, n_seg, 1, 6 * D_MODEL), lambda i, g: (l, i, 0, 0, 0)),
            pl.BlockSpec((None, LANES, D_MODEL), lambda i, g: (l, 0, 0)),
            pl.BlockSpec((None, LANES, 1), lambda i, g: (l, 0, 0)),
            pl.BlockSpec(memory_space=pl.ANY),
            pl.BlockSpec(memory_space=pl.ANY),
            pl.BlockSpec((None, 1, D_MODEL), lambda i, g: (l, 0, 0)),
            pl.BlockSpec((None, 1, D_MODEL), lambda i, g: (l, 0, 0)),
        ],
        out_specs=pl.BlockSpec((tm, D_MODEL), lambda i, g: (i, 0)),
        out_shape=jax.ShapeDtypeStruct((n, D_MODEL), F32),
        scratch_shapes=[
            pltpu.VMEM((tm, D_MODEL + GATE_COLS), BF16),
            pltpu.VMEM((tm, LANES), F32),
            pltpu.VMEM((SUBLANES, tm), F32),
            pltpu.VMEM((max(n_rows, MOE_BLOCK), d_act), BF16),
            pltpu.VMEM((2, EXPERTS_PER_GROUP, D_MODEL, 2 * D_EXPERT), F32),
            pltpu.VMEM((2, d_act, D_MODEL), F32),
            pltpu.SemaphoreType.DMA((2, 2)),
            pltpu.SMEM((N_GROUPS,), jnp.int32),
        ],
        compiler_params=pltpu.CompilerParams(
            dimension_semantics=("arbitrary", "arbitrary"), vmem_limit_bytes=VMEM_LIMIT_BYTES),
        name="hier_moe",
    )(*xs, mod_rows, w_routerT, b_routerT, w_exp_in, w_exp_out, ln_g, ln_b)


def kernel(x_prompt, x_sample, cache_k, cache_v, c, c_ctx, w_ada, b_ada, w_in, conv_w, rpb, w_out,
           ln1_g, ln1_b, w_router_group, b_router_group, w_router_expert, b_router_expert,
           w_expert_in, w_expert_out, ln2_g, ln2_b):
    batch, seq_len, _ = x_prompt.shape
    dec_batch, dec_seq, _ = x_sample.shape
    assert dec_batch + 1 <= SUBLANES and MOE_TOKENS % dec_seq == 0
    assert (dec_batch * dec_seq) % MOE_TOKENS == 0 and (batch * seq_len) % MOE_TOKENS == 0

    cond = jnp.concatenate([c_ctx[None, :], c, jnp.zeros((SUBLANES - 1 - dec_batch, D_MODEL), F32)], axis=0)
    mod = _modulation(cond, w_ada, b_ada)
    n_ctx_tiles = batch * seq_len // MOE_TOKENS
    mod_rows_ctx = jnp.broadcast_to(mod[:, 0:1, None, None, :], (DEPTH, n_ctx_tiles, 1, 1, 6 * D_MODEL))
    mod_rows_lat = mod[:, 1:1 + dec_batch].reshape(DEPTH, -1, MOE_TOKENS // dec_seq, 1, 6 * D_MODEL)

    w_in_bf = w_in.astype(BF16)
    w_out_bf = w_out.astype(BF16)
    pad = jnp.zeros((DEPTH, LANES - N_EXPERTS - N_GROUPS, D_MODEL), F32)
    w_routerT = jnp.concatenate(
        [jnp.swapaxes(w_router_expert, 1, 2), jnp.swapaxes(w_router_group, 1, 2), pad], axis=1)
    b_routerT = jnp.concatenate([b_router_expert, b_router_group, pad[:, :, 0]], axis=-1)[:, :, None]
    tbl = _bias_table(rpb)
    cache_kT = jnp.swapaxes(cache_k, -1, -2)
    cache_vT = jnp.swapaxes(cache_v, -1, -2)
    ln1_g3, ln1_b3 = ln1_g[:, None, :], ln1_b[:, None, :]
    ln2_g3, ln2_b3 = ln2_g[:, None, :], ln2_b[:, None, :]

    w_exp_out = w_expert_out.reshape(DEPTH, N_GROUPS, EXPERTS_PER_GROUP * D_EXPERT, D_MODEL)

    xp, xs = x_prompt, x_sample
    kv_bufs = None
    n_ctx = batch * seq_len
    ctx_tokens = (xp.reshape(n_ctx, D_MODEL), 0)
    lat_tokens = (xs.reshape(-1, D_MODEL), 0)
    mod_rows_both = jnp.concatenate([mod_rows_ctx, mod_rows_lat], axis=1)
    for l in range(DEPTH):
        moe = functools.partial(_moe, l, w_routerT=w_routerT, b_routerT=b_routerT, w_exp_in=w_expert_in,
                                w_exp_out=w_exp_out, ln_g=ln2_g3, ln_b=ln2_b3)
        xp, *kv_bufs = _ctx_mixer(l, ctx_tokens, x_prompt.shape, mod, w_in_bf, w_out_bf, conv_w,
                                  ln1_g3, ln1_b3, kv_bufs)
        xs = _lat_mixer(l, lat_tokens, x_sample.shape, mod, w_in_bf, w_out_bf, conv_w, ln1_g3, ln1_b3,
                        cache_kT, cache_vT, tbl)
        xp, xs = xp.reshape(n_ctx, D_MODEL), xs.reshape(-1, D_MODEL)
        if l + 1 < DEPTH:
            both = moe([xp, xs], mod_rows_both)
            ctx_tokens, lat_tokens = (both, 0), (both, n_ctx)
        else:
            xp = moe([xp], mod_rows_ctx).reshape(x_prompt.shape)
            xs = moe([xs], mod_rows_lat).reshape(x_sample.shape)
    new_kT, new_vT = kv_bufs
    return (xp, xs, jnp.swapaxes(new_kT, -1, -2), jnp.swapaxes(new_vT, -1, -2))
```

```python
import functools

import jax
import jax.numpy as jnp
from jax import lax
from jax.experimental import pallas as pl
from jax.experimental.pallas import tpu as pltpu

D_MODEL = 1024
DEPTH = 4
GRID_W = 64
ATT_WIDTH = D_MODEL // 2
CONV_WIDTH = D_MODEL - ATT_WIDTH
HEAD_DIM = 64
N_HEADS = ATT_WIDTH // HEAD_DIM
WIN_ROWS = 8
WIN_COLS = 16
N_GROUPS = 4
EXPERTS_PER_GROUP = 4
N_EXPERTS = N_GROUPS * EXPERTS_PER_GROUP
D_EXPERT = D_MODEL // 4
ALPHA = (2 * DEPTH) ** 0.25
LN_EPS = 1e-5
NEG_INF = -1e30
QK_SCALE = HEAD_DIM ** -0.5

F32 = jnp.float32
BF16 = jnp.bfloat16

LANES = 128
SUBLANES = 8
BF16_ROWS = 16
VMEM_LIMIT_BYTES = 58 * 1024 * 1024

CTX_SEQ_PER_STEP = 2
LAT_Q_CHUNK = 128
LAT_HEADS_PER_STEP = 2
MERGE_ROWS = 256
MOE_TOKENS = 1024
MOE_BLOCK = 512
MOE_SLOTS = ()
SCATTER_K = 256
GATE_COLS = 128
ROUTE_ROWS = 24
REC_SPLIT, REC_GATE, REC_GROUP, REC_RANK = 0, 12, 16, 17
MOD_COLS = 2048


def _dot(a, b):
    return jnp.dot(a, b, preferred_element_type=F32)


def _dot_nt(a, b):
    return lax.dot_general(a, b, (((1,), (1,)), ((), ())), preferred_element_type=F32)


def _silu(x):
    return x * (1.0 / (1.0 + jnp.exp(-x)))


def _layer_norm(r, g, b):
    mu = jnp.mean(r, axis=-1, keepdims=True)
    d = r - mu
    var = jnp.mean(d * d, axis=-1, keepdims=True)
    return d * lax.rsqrt(var + LN_EPS) * g + b


def _mod_kernel(cond_ref, w_ref, b_ref, o_ref):
    s = _silu(cond_ref[...]).astype(BF16)
    o_ref[...] = _dot(s, w_ref[...].astype(BF16)) + b_ref[...]


def _modulation(cond, w_ada, b_ada):
    n_out = w_ada.shape[-1]
    return pl.pallas_call(
        _mod_kernel,
        grid=(DEPTH, n_out // MOD_COLS),
        in_specs=[
            pl.BlockSpec((SUBLANES, D_MODEL), lambda l, j: (0, 0)),
            pl.BlockSpec((None, D_MODEL, MOD_COLS), lambda l, j: (l, 0, j)),
            pl.BlockSpec((None, 1, MOD_COLS), lambda l, j: (l, 0, j)),
        ],
        out_specs=pl.BlockSpec((None, SUBLANES, MOD_COLS), lambda l, j: (l, 0, j)),
        out_shape=jax.ShapeDtypeStruct((DEPTH, SUBLANES, n_out), F32),
        compiler_params=pltpu.CompilerParams(
            dimension_semantics=("arbitrary", "arbitrary"),
            vmem_limit_bytes=VMEM_LIMIT_BYTES),
        name="adaln_modulation",
    )(cond, w_ada, b_ada.reshape(DEPTH, 1, n_out))


def _project(x, mod, win_ref, convw_ref, qT_scr, k_scr, vT_scr, conv_scr, seq_len, kT_out=None, vT_out=None):
    m = x.shape[0]
    sa = mod[:, 0:D_MODEL]
    ca = mod[:, D_MODEL:2 * D_MODEL]
    h = (x * (1.0 + ca) + sa).astype(BF16)

    qT_scr[...] = (_dot(h, win_ref[:, 0:ATT_WIDTH]) * QK_SCALE).T.astype(BF16)
    zk = _dot(h, win_ref[:, ATT_WIDTH:2 * ATT_WIDTH])
    for hd in range(N_HEADS):
        k_scr[hd] = zk[:, hd * HEAD_DIM:(hd + 1) * HEAD_DIM].astype(BF16)
    zvT = _dot(h, win_ref[:, 2 * ATT_WIDTH:3 * ATT_WIDTH]).T
    vT_scr[...] = zvT.astype(BF16)
    if kT_out is not None:
        zkT = zk.T
        for out, zT in ((kT_out, zkT), (vT_out, zvT)):
            for s in range(m // seq_len):
                for hd in range(N_HEADS):
                    out[s, hd] = zT[hd * HEAD_DIM:(hd + 1) * HEAD_DIM, s * seq_len:(s + 1) * seq_len]

    c0 = 3 * ATT_WIDTH
    bg = _dot(h, win_ref[:, c0:c0 + CONV_WIDTH])
    cg = _dot(h, win_ref[:, c0 + CONV_WIDTH:c0 + 2 * CONV_WIDTH])
    u = _dot(h, win_ref[:, c0 + 2 * CONV_WIDTH:c0 + 3 * CONV_WIDTH])
    y = cg * u
    t = lax.broadcasted_iota(jnp.int32, (m, 1), 0) % seq_len
    y_prev = jnp.where(t == 0, 0.0, pltpu.roll(y, 1, 0))
    y_next = jnp.where(t == seq_len - 1, 0.0, pltpu.roll(y, m - 1, 0))
    cw = convw_ref[...]
    conv = cw[0:1, :] * y_prev + cw[1:2, :] * y + cw[2:3, :] * y_next
    conv_scr[...] = (bg * conv).astype(BF16)


def _merge_and_norm(x, mod, attT_scr, conv_scr, wout_ref, g_ref, b_ref, o_ref):
    ga = mod[:, 2 * D_MODEL:3 * D_MODEL]
    out = []
    for r in range(0, x.shape[0], MERGE_ROWS):
        rows = slice(r, r + MERGE_ROWS)
        att = attT_scr[:, rows].T.astype(BF16)
        mix = _dot(att, wout_ref[0:ATT_WIDTH, :]) + _dot(conv_scr[rows, :], wout_ref[ATT_WIDTH:, :])
        out.append(_layer_norm(ALPHA * x[rows, :] + ga * mix, g_ref[...], b_ref[...]))
    o_ref[...] = jnp.concatenate(out, axis=0).reshape(o_ref.shape)


def _softmax_keys(parts):
    mx = functools.reduce(jnp.maximum, [jnp.max(p, axis=0, keepdims=True) for p in parts])
    es = [jnp.exp(p - mx) for p in parts]
    inv = 1.0 / functools.reduce(jnp.add, [jnp.sum(e, axis=0, keepdims=True) for e in es])
    return [e.astype(BF16) for e in es], inv


def _head_rows(hd):
    return pl.ds(pl.multiple_of(hd * HEAD_DIM, HEAD_DIM), HEAD_DIM)


def _weight_spec(shape, index_map):
    return pl.BlockSpec(shape, index_map, pipeline_mode=pl.Buffered(1))


def _ctx_mixer_kernel(l, creates_cache, x_ref, mod_ref, win_ref, wout_ref, convw_ref, g_ref, b_ref, *rest):
    if creates_cache:
        o_ref, kT_all, vT_all, qT_scr, k_scr, vT_scr, attT_scr, conv_scr, sT_scr, pT_scr, inv_scr = rest
        for ref in (kT_all, vT_all):
            for other in range(DEPTH):
                if other != l:
                    ref[:, other] = jnp.zeros(ref.shape[:1] + ref.shape[2:], F32)
        kT_ref, vT_ref = kT_all.at[:, l], vT_all.at[:, l]
    else:
        _, _, o_ref, kT_ref, vT_ref, qT_scr, k_scr, vT_scr, attT_scr, conv_scr, sT_scr, pT_scr, inv_scr = rest
    sb, seq_len, _ = o_ref.shape
    x = x_ref[...]
    mod = mod_ref[0:1, :]
    _project(x, mod, win_ref, convw_ref, qT_scr, k_scr, vT_scr, conv_scr, seq_len, kT_ref, vT_ref)

    pairs = [(hd, s) for hd in range(N_HEADS) for s in range(sb)]
    chan = lambda hd: slice(hd * HEAD_DIM, (hd + 1) * HEAD_DIM)
    tok = lambda s: slice(s * seq_len, (s + 1) * seq_len)
    for i, (hd, s) in enumerate(pairs):
        sT_scr[i] = _dot(k_scr[hd, tok(s), :], qT_scr[chan(hd), tok(s)])
    for i in range(len(pairs)):
        (pT_scr[i],), inv_scr[i] = _softmax_keys([sT_scr[i]])
    for i, (hd, s) in enumerate(pairs):
        attT_scr[chan(hd), tok(s)] = _dot(vT_scr[chan(hd), tok(s)], pT_scr[i]) * inv_scr[i]

    _merge_and_norm(x, mod, attT_scr, conv_scr, wout_ref, g_ref, b_ref, o_ref)


def _ctx_mixer(l, tokens, xp_shape, mod, w_in, w_out, conv_w, ln_g, ln_b, kv_bufs):
    batch, seq_len, _ = xp_shape
    sb = CTX_SEQ_PER_STEP
    m = sb * seq_len
    x2d, first_row = tokens
    assert first_row % m == 0
    first_blk = first_row // m
    creates_cache = kv_bufs is None
    kv_shape = jax.ShapeDtypeStruct((batch, DEPTH, N_HEADS, HEAD_DIM, seq_len), F32)
    if creates_cache:
        kv_spec = pl.BlockSpec((sb, DEPTH, N_HEADS, HEAD_DIM, seq_len), lambda i: (i, 0, 0, 0, 0))
    else:
        kv_spec = pl.BlockSpec((sb, None, N_HEADS, HEAD_DIM, seq_len), lambda i: (i, l, 0, 0, 0))
    in_specs = [
        pl.BlockSpec((m, D_MODEL), lambda i: (first_blk + i, 0)),
        pl.BlockSpec((None, SUBLANES, 6 * D_MODEL), lambda i: (l, 0, 0)),
        _weight_spec((None, D_MODEL, 3 * ATT_WIDTH + 3 * CONV_WIDTH), lambda i: (l, 0, 0)),
        _weight_spec((None, D_MODEL, D_MODEL), lambda i: (l, 0, 0)),
        pl.BlockSpec((None, 3, CONV_WIDTH), lambda i: (l, 0, 0)),
        pl.BlockSpec((None, 1, D_MODEL), lambda i: (l, 0, 0)),
        pl.BlockSpec((None, 1, D_MODEL), lambda i: (l, 0, 0)),
    ]
    args = [x2d, mod, w_in, w_out, conv_w, ln_g, ln_b]
    aliases = {}
    if not creates_cache:
        aliases = {len(args): 1, len(args) + 1: 2}
        in_specs += [pl.BlockSpec(memory_space=pl.ANY)] * 2
        args += list(kv_bufs)
    return pl.pallas_call(
        functools.partial(_ctx_mixer_kernel, l, creates_cache),
        grid=(batch // sb,),
        in_specs=in_specs,
        out_specs=[pl.BlockSpec((sb, seq_len, D_MODEL), lambda i: (i, 0, 0)), kv_spec, kv_spec],
        out_shape=[jax.ShapeDtypeStruct(xp_shape, F32), kv_shape, kv_shape],
        input_output_aliases=aliases,
        scratch_shapes=[
            pltpu.VMEM((ATT_WIDTH, m), BF16),
            pltpu.VMEM((N_HEADS, m, HEAD_DIM), BF16),
            pltpu.VMEM((ATT_WIDTH, m), BF16),
            pltpu.VMEM((ATT_WIDTH, m), F32),
            pltpu.VMEM((m, CONV_WIDTH), BF16),
            pltpu.VMEM((N_HEADS * sb, seq_len, seq_len), F32),
            pltpu.VMEM((N_HEADS * sb, seq_len, seq_len), BF16),
            pltpu.VMEM((N_HEADS * sb, 1, seq_len), F32),
        ],
        compiler_params=pltpu.CompilerParams(
            dimension_semantics=("arbitrary",), vmem_limit_bytes=VMEM_LIMIT_BYTES),
        name="ctx_mixer",
    )(*args)


def _window_start(r, rows):
    return min(max(r - WIN_ROWS // 2, 0), rows - WIN_ROWS)


def _chunk_key_rows(r0, chunk_rows, rows):
    lo = _window_start(r0, rows) // 2 * 2
    hi = -(-(_window_start(r0 + chunk_rows - 1, rows) + WIN_ROWS) // 2) * 2
    return lo, hi


def _lat_mixer_kernel(x_ref, mod_ref, win_ref, wout_ref, convw_ref, g_ref, b_ref,
                      ckT_ref, cvT_ref, tbl_ref, o_ref,
                      qT_scr, k_scr, vT_scr, attT_scr, conv_scr, sT_scr, pT_scr, inv_scr):
    b = pl.program_id(0)
    step = pl.program_id(1)
    seq_len = x_ref.shape[0]
    rows = seq_len // GRID_W
    mod = mod_ref[pl.ds(1 + b, 1), :]

    @pl.when(step == 0)
    def _():
        _project(x_ref[...], mod, win_ref, convw_ref, qT_scr, k_scr, vT_scr, conv_scr, seq_len)

    low_half = lax.broadcasted_iota(jnp.int32, (GRID_W, 2 * GRID_W), 1) < GRID_W

    def bias_block(j, r_pair, rk):
        inside = [_window_start(r, rows) <= rk < _window_start(r, rows) + WIN_ROWS for r in (r_pair, r_pair + 1)]
        d = rk - r_pair + WIN_ROWS - 1
        neg = jnp.full((GRID_W, 2 * GRID_W), NEG_INF, F32)
        if not any(inside):
            return neg
        blk = tbl_ref[j, d * GRID_W:(d + 1) * GRID_W, :]
        if all(inside):
            return blk
        return jnp.where(low_half, blk, neg) if inside[0] else jnp.where(low_half, neg, blk)

    chunk_rows = LAT_Q_CHUNK // GRID_W
    n_chunks = seq_len // LAT_Q_CHUNK
    past = ckT_ref.shape[-1]
    max_loc = sT_scr.shape[1] - past
    chunks = []
    for qc in range(n_chunks):
        r0 = qc * chunk_rows
        k_lo, k_hi = _chunk_key_rows(r0, chunk_rows, rows)
        chunks.append((qc, r0, k_lo, k_hi, slice(qc * LAT_Q_CHUNK, (qc + 1) * LAT_Q_CHUNK)))
    for j in range(LAT_HEADS_PER_STEP):
        hd = step * LAT_HEADS_PER_STEP + j
        chan = _head_rows(hd)
        ckT = ckT_ref[j]
        ck = jnp.concatenate([ckT, jnp.zeros_like(ckT)], axis=0).T.astype(BF16)
        cvT = cvT_ref[j].astype(BF16)
        for qc, r0, k_lo, k_hi, qcols in chunks:
            i = j * n_chunks + qc
            qT = qT_scr[chan, qcols]
            bias = jnp.concatenate(
                [jnp.concatenate([bias_block(j, r0 + jj, rk) for jj in range(0, chunk_rows, 2)], axis=1)
                 for rk in range(k_lo, k_hi)], axis=0)
            sT_scr[i, 0:(k_hi - k_lo) * GRID_W] = _dot(k_scr[hd, k_lo * GRID_W:k_hi * GRID_W, :], qT) + bias
            sT_scr[i, max_loc:] = _dot(ck, jnp.concatenate([qT, jnp.zeros_like(qT)], axis=0))
        for qc, r0, k_lo, k_hi, qcols in chunks:
            i = j * n_chunks + qc
            n_loc = (k_hi - k_lo) * GRID_W
            (pT_scr[i, 0:n_loc], pT_scr[i, max_loc:]), inv_scr[i] = _softmax_keys(
                [sT_scr[i, 0:n_loc], sT_scr[i, max_loc:]])
        for qc, r0, k_lo, k_hi, qcols in chunks:
            i = j * n_chunks + qc
            n_loc = (k_hi - k_lo) * GRID_W
            attT_scr[chan, qcols] = (_dot(vT_scr[chan, k_lo * GRID_W:k_hi * GRID_W], pT_scr[i, 0:n_loc])
                                     + _dot(cvT, pT_scr[i, max_loc:])) * inv_scr[i]

    @pl.when(step == pl.num_programs(1) - 1)
    def _():
        _merge_and_norm(x_ref[...], mod, attT_scr, conv_scr, wout_ref, g_ref, b_ref, o_ref)


def _lat_mixer(l, tokens, xs_shape, mod, w_in, w_out, conv_w, ln_g, ln_b, cache_kT, cache_vT, tbl):
    batch, seq_len, _ = xs_shape
    x2d, first_row = tokens
    assert first_row % seq_len == 0
    first_blk = first_row // seq_len
    past = cache_kT.shape[-1]
    rows, chunk_rows, n_chunks = seq_len // GRID_W, LAT_Q_CHUNK // GRID_W, seq_len // LAT_Q_CHUNK
    max_loc = GRID_W * max(hi - lo for lo, hi in
                           (_chunk_key_rows(qc * chunk_rows, chunk_rows, rows) for qc in range(n_chunks)))
    hps = LAT_HEADS_PER_STEP
    cache_spec = pl.BlockSpec((None, None, hps, HEAD_DIM, past), lambda b, h: (b, l, h, 0, 0))
    return pl.pallas_call(
        _lat_mixer_kernel,
        grid=(batch, N_HEADS // hps),
        in_specs=[
            pl.BlockSpec((seq_len, D_MODEL), lambda b, h: (first_blk + b, 0)),
            pl.BlockSpec((None, SUBLANES, 6 * D_MODEL), lambda b, h: (l, 0, 0)),
            _weight_spec((None, D_MODEL, 3 * ATT_WIDTH + 3 * CONV_WIDTH), lambda b, h: (l, 0, 0)),
            _weight_spec((None, D_MODEL, D_MODEL), lambda b, h: (l, 0, 0)),
            pl.BlockSpec((None, 3, CONV_WIDTH), lambda b, h: (l, 0, 0)),
            pl.BlockSpec((None, 1, D_MODEL), lambda b, h: (l, 0, 0)),
            pl.BlockSpec((None, 1, D_MODEL), lambda b, h: (l, 0, 0)),
            cache_spec, cache_spec,
            pl.BlockSpec((None, hps) + tbl.shape[2:], lambda b, h: (l, h, 0, 0)),
        ],
        out_specs=pl.BlockSpec((None, seq_len, D_MODEL), lambda b, h: (b, 0, 0)),
        out_shape=jax.ShapeDtypeStruct(xs_shape, F32),
        scratch_shapes=[
            pltpu.VMEM((ATT_WIDTH, seq_len), BF16),
            pltpu.VMEM((N_HEADS, seq_len, HEAD_DIM), BF16),
            pltpu.VMEM((ATT_WIDTH, seq_len), BF16),
            pltpu.VMEM((ATT_WIDTH, seq_len), F32),
            pltpu.VMEM((seq_len, CONV_WIDTH), BF16),
            pltpu.VMEM((hps * n_chunks, max_loc + past, LAT_Q_CHUNK), F32),
            pltpu.VMEM((hps * n_chunks, max_loc + past, LAT_Q_CHUNK), BF16),
            pltpu.VMEM((hps * n_chunks, 1, LAT_Q_CHUNK), F32),
        ],
        compiler_params=pltpu.CompilerParams(
            dimension_semantics=("arbitrary", "arbitrary"), vmem_limit_bytes=VMEM_LIMIT_BYTES),
        name="lat_mixer",
    )(x2d, mod, w_in, w_out, conv_w, ln_g, ln_b, cache_kT, cache_vT, tbl)


def _bias_table(rpb):
    depth, heads, n_dr, n_dc = rpb.shape
    n_blk = n_dr + 1
    lane0 = GRID_W - WIN_COLS
    n_pad = -(-n_dr // SUBLANES) * SUBLANES
    rpb_pad = jnp.pad(rpb[..., ::-1], ((0, 0), (0, 0), (0, n_pad - n_dr), (lane0, LANES - lane0 - n_dc)))
    rpb_pad = rpb_pad.reshape(depth * heads, n_pad, LANES)

    def body(r_ref, o_ref):
        cp = lax.broadcasted_iota(jnp.int32, (GRID_W, LANES), 0)
        lane = lax.broadcasted_iota(jnp.int32, (GRID_W, LANES), 1)
        low_half = lane < GRID_W
        c = jnp.where(low_half, lane, lane - GRID_W)
        col_start = jnp.clip(c - WIN_COLS // 2, 0, GRID_W - WIN_COLS)
        valid = (cp >= col_start) & (cp < col_start + WIN_COLS)
        def skewed(hd, dr, shift):
            row = jnp.broadcast_to(r_ref[hd, dr:dr + 1, :], (GRID_W, LANES))
            return pltpu.roll(row, shift, 1, stride=1, stride_axis=0)

        for hd in range(heads):
            for d in range(n_blk):
                lo = skewed(hd, d, LANES - GRID_W + 1) if d < n_dr else None
                hi = skewed(hd, d - 1, 1) if d >= 1 else None
                if lo is None:
                    blk = jnp.where(valid & ~low_half, hi, NEG_INF)
                elif hi is None:
                    blk = jnp.where(valid & low_half, lo, NEG_INF)
                else:
                    blk = jnp.where(valid, jnp.where(low_half, lo, hi), NEG_INF)
                o_ref[hd, d * GRID_W:(d + 1) * GRID_W, :] = blk

    tbl = pl.pallas_call(
        body,
        grid=(depth,),
        in_specs=[pl.BlockSpec((heads, n_pad, LANES), lambda i: (i, 0, 0))],
        out_specs=pl.BlockSpec((heads, n_blk * GRID_W, LANES), lambda i: (i, 0, 0)),
        out_shape=jax.ShapeDtypeStruct((depth * heads, n_blk * GRID_W, LANES), F32),
        compiler_params=pltpu.CompilerParams(dimension_semantics=("arbitrary",)),
        name="bias_table",
    )(rpb_pad)
    return tbl.reshape(depth, heads, n_blk * GRID_W, LANES)


def _route_t(logits):
    row = lax.broadcasted_iota(jnp.int32, logits.shape, 0)
    row_f = row.astype(F32)
    big = jnp.float32(LANES)

    def first_row(cond):
        return jnp.min(jnp.where(cond, row_f, big), axis=0, keepdims=True)

    gmask = (row >= N_EXPERTS) & (row < N_EXPERTS + N_GROUPS)
    gl = jnp.where(gmask, logits, NEG_INF)
    gexp = jnp.exp(gl - jnp.max(gl, axis=0, keepdims=True))
    gprob = gexp / jnp.sum(gexp, axis=0, keepdims=True)
    g_p = jnp.max(gprob, axis=0, keepdims=True)
    g_idx = first_row(gmask & (gprob == g_p)) - N_EXPERTS

    row_group = jnp.floor(row_f * (1.0 / EXPERTS_PER_GROUP))
    emask = (row < N_EXPERTS) & (row_group == g_idx)
    el = jnp.where(emask, logits, NEG_INF)
    eexp = jnp.exp(el - jnp.max(el, axis=0, keepdims=True))
    eprob = eexp / jnp.sum(eexp, axis=0, keepdims=True)
    p1 = jnp.max(eprob, axis=0, keepdims=True)
    i1 = first_row(emask & (eprob == p1))
    rest = emask & (row_f != i1)
    p2 = jnp.max(jnp.where(rest, eprob, -1.0), axis=0, keepdims=True)
    i2 = first_row(rest & (eprob == p2))
    denom = p1 + p2
    gate = (jnp.where(row_f == i1, g_p * p1 / denom, 0.0)
            + jnp.where(row_f == i2, g_p * p2 / denom, 0.0))
    return gate, g_idx


def _split3(v):
    hi = v.astype(BF16).astype(F32)
    mid = (v - hi).astype(BF16).astype(F32)
    return hi, mid, v - hi - mid


def _moe_route_tile(x_ref, mod_ref, wrT_ref, brT_ref, h_scr, rec_scr, recT_scr, o_ref, tier_smem):
    tm = x_ref.shape[0]
    seg = tm // mod_ref.shape[0]
    h = jnp.concatenate(
        [x_ref[s * seg:(s + 1) * seg, :] * (1.0 + mod_ref[s, :, 4 * D_MODEL:5 * D_MODEL])
         + mod_ref[s, :, 3 * D_MODEL:4 * D_MODEL] for s in range(tm // seg)], axis=0)
    h_hi = h.astype(BF16)
    h_lo = (h - h_hi.astype(F32)).astype(BF16)
    wr = wrT_ref[...]
    wr_hi = wr.astype(BF16)
    wr_lo = (wr - wr_hi.astype(F32)).astype(BF16)
    logits = (_dot_nt(wr_hi, h_hi) + _dot_nt(wr_hi, h_lo) + _dot_nt(wr_lo, h_hi))[0:ROUTE_ROWS, :]
    gate, g_idx = _route_t(logits + brT_ref[0:ROUTE_ROWS, :])

    row_f = lax.broadcasted_iota(jnp.int32, (ROUTE_ROWS, tm), 0).astype(F32)
    gate4 = jnp.concatenate(
        [jnp.sum(jnp.where(row_f == EXPERTS_PER_GROUP * g_idx + j, gate, 0.0), axis=0, keepdims=True)
         for j in range(EXPERTS_PER_GROUP)], axis=0)

    grp = lax.broadcasted_iota(jnp.int32, (SUBLANES, tm), 0).astype(F32)
    onehot = jnp.where(grp == g_idx, 1.0, 0.0)
    ri = lax.broadcasted_iota(jnp.int32, (MOE_BLOCK, MOE_BLOCK), 0)
    ci = lax.broadcasted_iota(jnp.int32, (MOE_BLOCK, MOE_BLOCK), 1)
    earlier = jnp.where(ri < ci, 1.0, 0.0).astype(BF16)
    ranks = []
    most = jnp.zeros((SUBLANES, 1), F32)
    for b in range(tm // MOE_BLOCK):
        oh_b = onehot[:, b * MOE_BLOCK:(b + 1) * MOE_BLOCK]
        ranks.append(_dot(oh_b.astype(BF16), earlier))
        most = jnp.maximum(most, jnp.sum(oh_b, axis=1, keepdims=True))
    rank = jnp.sum(onehot * jnp.concatenate(ranks, axis=1), axis=0, keepdims=True)

    recT = jnp.concatenate([*_split3(gate4), gate4, g_idx, rank,
                            jnp.zeros((LANES - REC_RANK - 1, tm), F32)], axis=0)
    rec = recT.T
    h_scr[:, 0:D_MODEL] = h_hi
    h_scr[:, D_MODEL:] = rec.astype(BF16)
    rec_scr[...] = rec
    recT_scr[...] = recT[REC_GROUP:REC_GROUP + SUBLANES, :]
    grp_row = lax.broadcasted_iota(jnp.int32, (SUBLANES, 1), 0)
    for gp in range(N_GROUPS):
        fullest = jnp.max(jnp.where(grp_row == gp, most, 0.0))
        tier_smem[gp] = sum((fullest > slot).astype(jnp.int32) for slot in MOE_SLOTS)
    o_ref[...] = jnp.zeros_like(o_ref)


class _TileOf:
    def __init__(self, refs, first_tiles, tile):
        self.refs, self.first_tiles, self.tile = refs, first_tiles, tile
        self.shape = refs[0].shape

    def __getitem__(self, idx):
        val = self.refs[0][idx]
        for ref, first in zip(self.refs[1:], self.first_tiles[1:]):
            val = jnp.where(self.tile >= first, ref[idx], val)
        return val


def _moe_kernel(layer, first_tiles, *refs):
    (mod_ref, wrT_ref, brT_ref, win_hbm, wout_hbm, g_ref, b_ref, o_ref,
     h_scr, rec_scr, recT_scr, act_scr, win_buf, wout_buf, w_sem, tier_smem) = refs[len(first_tiles):]
    x_ref = _TileOf(refs[:len(first_tiles)], first_tiles, pl.program_id(0))
    g = pl.program_id(1)
    g_f = g.astype(F32)
    tm = x_ref.shape[0]
    n_blocks = tm // MOE_BLOCK
    step = pl.program_id(0) * N_GROUPS + g
    buf = step % 2

    def weight_copies(group, buf):
        experts_of_group = pl.ds(group * EXPERTS_PER_GROUP, EXPERTS_PER_GROUP)
        return (pltpu.make_async_copy(win_hbm.at[layer, experts_of_group], win_buf.at[buf], w_sem.at[0, buf]),
                pltpu.make_async_copy(wout_hbm.at[layer, group], wout_buf.at[buf], w_sem.at[1, buf]))

    @pl.when(step == 0)
    def _():
        for copy in weight_copies(0, 0):
            copy.start()

    @pl.when(step + 1 < pl.num_programs(0) * N_GROUPS)
    def _():
        for copy in weight_copies((g + 1) % N_GROUPS, 1 - buf):
            copy.start()

    @pl.when(g == 0)
    def _():
        _moe_route_tile(x_ref, mod_ref, wrT_ref, brT_ref, h_scr, rec_scr, recT_scr, o_ref, tier_smem)

    for copy in weight_copies(g, buf):
        copy.wait()
    win_ref = win_buf.at[buf]
    wout_ref = wout_buf.at[buf]

    def experts(xb, gates, gate_lane, rows):
        for e in range(EXPERTS_PER_GROUP):
            hid = _dot(xb, win_ref[e].astype(BF16))
            act = _silu(hid[:, :D_EXPERT]) * hid[:, D_EXPERT:] * gates[:, gate_lane + e:gate_lane + e + 1]
            act_scr[rows, e * D_EXPERT:(e + 1) * D_EXPERT] = act.astype(BF16)
        return _dot(act_scr[rows, :], wout_ref[...].astype(BF16))

    def compact(slot):
        n_rows = n_blocks * slot
        slot_row = lax.broadcasted_iota(jnp.int32, (slot, MOE_BLOCK), 0).astype(F32)
        gathered = []
        for b in range(n_blocks):
            tok = slice(b * MOE_BLOCK, (b + 1) * MOE_BLOCK)
            pick = (recT_scr[0:1, tok] == g_f) & (recT_scr[1:2, tok] == slot_row)
            gathered.append(_dot(jnp.where(pick, 1.0, 0.0).astype(BF16), h_scr[tok, :]))
        xg = jnp.concatenate(gathered, axis=0)
        ge = xg[:, D_MODEL:]
        gates = ((ge + pltpu.roll(ge, LANES - EXPERTS_PER_GROUP, 1))
                 + pltpu.roll(ge, LANES - 2 * EXPERTS_PER_GROUP, 1))
        y = experts(xg[:, :D_MODEL].astype(BF16), gates, REC_SPLIT, slice(0, n_rows)).astype(BF16)
        slot_col = lax.broadcasted_iota(jnp.int32, (MOE_BLOCK, SCATTER_K), 1).astype(F32)
        for b in range(n_blocks):
            tok = slice(b * MOE_BLOCK, (b + 1) * MOE_BLOCK)
            first = min(b * slot, n_rows - SCATTER_K)
            place = ((rec_scr[tok, REC_GROUP:REC_GROUP + 1] == g_f)
                     & (rec_scr[tok, REC_RANK:REC_RANK + 1] + (b * slot - first) == slot_col))
            o_ref[tok, :] += _dot(jnp.where(place, 1.0, 0.0).astype(BF16), y[first:first + SCATTER_K, :])

    for tier, slot in enumerate(MOE_SLOTS):
        pl.when(tier_smem[g] == tier)(functools.partial(compact, slot))

    @pl.when(tier_smem[g] == len(MOE_SLOTS))
    def _():
        for b in range(n_blocks):
            tok = slice(b * MOE_BLOCK, (b + 1) * MOE_BLOCK)
            gates = jnp.where(rec_scr[tok, REC_GROUP:REC_GROUP + 1] == g_f, rec_scr[tok, :], 0.0)
            o_ref[tok, :] += experts(h_scr[tok, 0:D_MODEL], gates, REC_GATE, slice(0, MOE_BLOCK))

    @pl.when(g == N_GROUPS - 1)
    def _():
        seg = tm // mod_ref.shape[0]
        for s in range(tm // seg):
            tok = slice(s * seg, (s + 1) * seg)
            gf = mod_ref[s, :, 5 * D_MODEL:6 * D_MODEL]
            o_ref[tok, :] = _layer_norm(ALPHA * x_ref[tok, :] + gf * o_ref[tok, :], g_ref[...], b_ref[...])


def _moe(l, xs, mod_rows, w_routerT, b_routerT, w_exp_in, w_exp_out, ln_g, ln_b):
    tm = MOE_TOKENS
    tiles = [x.shape[0] // tm for x in xs]
    first_tiles = tuple(sum(tiles[:k]) for k in range(len(xs)))
    n = tm * sum(tiles)
    x_specs = [pl.BlockSpec((tm, D_MODEL), lambda i, g, first=first, n_k=n_k: (jnp.clip(i - first, 0, n_k - 1), 0))
               for first, n_k in zip(first_tiles, tiles)]
    n_seg = mod_rows.shape[2]
    n_blocks = tm // MOE_BLOCK
    assert tm % MOE_BLOCK == 0
    for slot in MOE_SLOTS:
        assert slot % BF16_ROWS == 0 and slot <= SCATTER_K <= n_blocks * slot
    n_rows = n_blocks * max(MOE_SLOTS, default=0)
    d_act = EXPERTS_PER_GROUP * D_EXPERT
    return pl.pallas_call(
        functools.partial(_moe_kernel, l, first_tiles),
        grid=(n // tm, N_GROUPS),
        in_specs=x_specs + [
            pl.BlockSpec((None, None, n_seg, 1, 6 * D_MODEL), lambda i, g: (l, i, 0, 0, 0)),
            pl.BlockSpec((None, LANES, D_MODEL), lambda i, g: (l, 0, 0)),
            pl.BlockSpec((None, LANES, 1), lambda i, g: (l, 0, 0)),
            pl.BlockSpec(memory_space=pl.ANY),
            pl.BlockSpec(memory_space=pl.ANY),
            pl.BlockSpec((None, 1, D_MODEL), lambda i, g: (l, 0, 0)),
            pl.BlockSpec((None, 1, D_MODEL), lambda i, g: (l, 0, 0)),
        ],
        out_specs=pl.BlockSpec((tm, D_MODEL), lambda i, g: (i, 0)),
        out_shape=jax.ShapeDtypeStruct((n, D_MODEL), F32),
        scratch_shapes=[
            pltpu.VMEM((tm, D_MODEL + GATE_COLS), BF16),
            pltpu.VMEM((tm, LANES), F32),
            pltpu.VMEM((SUBLANES, tm), F32),
            pltpu.VMEM((max(n_rows, MOE_BLOCK), d_act), BF16),
            pltpu.VMEM((2, EXPERTS_PER_GROUP, D_MODEL, 2 * D_EXPERT), F32),
            pltpu.VMEM((2, d_act, D_MODEL), F32),
            pltpu.SemaphoreType.DMA((2, 2)),
            pltpu.SMEM((N_GROUPS,), jnp.int32),
        ],
        compiler_params=pltpu.CompilerParams(
            dimension_semantics=("arbitrary", "arbitrary"), vmem_limit_bytes=VMEM_LIMIT_BYTES),
        name="hier_moe",
    )(*xs, mod_rows, w_routerT, b_routerT, w_exp_in, w_exp_out, ln_g, ln_b)


def kernel(x_prompt, x_sample, cache_k, cache_v, c, c_ctx, w_ada, b_ada, w_in, conv_w, rpb, w_out,
           ln1_g, ln1_b, w_router_group, b_router_group, w_router_expert, b_router_expert,
           w_expert_in, w_expert_out, ln2_g, ln2_b):
    batch, seq_len, _ = x_prompt.shape
    dec_batch, dec_seq, _ = x_sample.shape
    assert dec_batch + 1 <= SUBLANES and MOE_TOKENS % dec_seq == 0
    assert (dec_batch * dec_seq) % MOE_TOKENS == 0 and (batch * seq_len) % MOE_TOKENS == 0

    cond = jnp.concatenate([c_ctx[None, :], c, jnp.zeros((SUBLANES - 1 - dec_batch, D_MODEL), F32)], axis=0)
    mod = _modulation(cond, w_ada, b_ada)
    n_ctx_tiles = batch * seq_len // MOE_TOKENS
    mod_rows_ctx = jnp.broadcast_to(mod[:, 0:1, None, None, :], (DEPTH, n_ctx_tiles, 1, 1, 6 * D_MODEL))
    mod_rows_lat = mod[:, 1:1 + dec_batch].reshape(DEPTH, -1, MOE_TOKENS // dec_seq, 1, 6 * D_MODEL)

    w_in_bf = w_in.astype(BF16)
    w_out_bf = w_out.astype(BF16)
    pad = jnp.zeros((DEPTH, LANES - N_EXPERTS - N_GROUPS, D_MODEL), F32)
    w_routerT = jnp.concatenate(
        [jnp.swapaxes(w_router_expert, 1, 2), jnp.swapaxes(w_router_group, 1, 2), pad], axis=1)
    b_routerT = jnp.concatenate([b_router_expert, b_router_group, pad[:, :, 0]], axis=-1)[:, :, None]
    tbl = _bias_table(rpb)
    cache_kT = jnp.swapaxes(cache_k, -1, -2)
    cache_vT = jnp.swapaxes(cache_v, -1, -2)
    ln1_g3, ln1_b3 = ln1_g[:, None, :], ln1_b[:, None, :]
    ln2_g3, ln2_b3 = ln2_g[:, None, :], ln2_b[:, None, :]

    w_exp_out = w_expert_out.reshape(DEPTH, N_GROUPS, EXPERTS_PER_GROUP * D_EXPERT, D_MODEL)

    xp, xs = x_prompt, x_sample
    kv_bufs = None
    n_ctx = batch * seq_len
    ctx_tokens = (xp.reshape(n_ctx, D_MODEL), 0)
    lat_tokens = (xs.reshape(-1, D_MODEL), 0)
    mod_rows_both = jnp.concatenate([mod_rows_ctx, mod_rows_lat], axis=1)
    for l in range(DEPTH):
        moe = functools.partial(_moe, l, w_routerT=w_routerT, b_routerT=b_routerT, w_exp_in=w_expert_in,
                                w_exp_out=w_exp_out, ln_g=ln2_g3, ln_b=ln2_b3)
        xp, *kv_bufs = _ctx_mixer(l, ctx_tokens, x_prompt.shape, mod, w_in_bf, w_out_bf, conv_w,
                                  ln1_g3, ln1_b3, kv_bufs)
        xs = _lat_mixer(l, lat_tokens, x_sample.shape, mod, w_in_bf, w_out_bf, conv_w, ln1_g3, ln1_b3,
                        cache_kT, cache_vT, tbl)
        xp, xs = xp.reshape(n_ctx, D_MODEL), xs.reshape(-1, D_MODEL)
        if l + 1 < DEPTH:
            both = moe([xp, xs], mod_rows_both)
            ctx_tokens, lat_tokens = (both, 0), (both, n_ctx)
        else:
            xp = moe([xp], mod_rows_ctx).reshape(x_prompt.shape)
            xs = moe([xs], mod_rows_lat).reshape(x_sample.shape)
    new_kT, new_vT = kv_bufs
    return (xp, xs, jnp.swapaxes(new_kT, -1, -2), jnp.swapaxes(new_vT, -1, -2))
```
